```python
import jax, jax.numpy as jnp
from jax import lax
import numpy as np

D_MODEL = 1024
BATCH = 8
SEQ = 8192
DEPTH = 2
DEC_BATCH = 8
DEC_SEQ = 2048
PAST_LEN = 128

FOURIER_GROUPS = 4
FOURIER_GROUP_DIM = 128
FOURIER_DIM = FOURIER_GROUPS * FOURIER_GROUP_DIM
N_HEADS = 8
QK_NOPE_DIM = 64
QK_ROPE_DIM = 32
V_HEAD_DIM = 64
Q_LORA_RANK = 384
KV_LORA_RANK = 256
ATTN_DIM = N_HEADS * V_HEAD_DIM
QK_DIM = QK_NOPE_DIM + QK_ROPE_DIM
ROPE_BASE = 10000.0
Q_BLOCK = 128
N_BRANCHES = 2
D_IN = FOURIER_DIM + Q_LORA_RANK + KV_LORA_RANK + QK_ROPE_DIM + N_BRANCHES * D_MODEL
SPLITS = [FOURIER_DIM,
          FOURIER_DIM + Q_LORA_RANK,
          FOURIER_DIM + Q_LORA_RANK + KV_LORA_RANK,
          FOURIER_DIM + Q_LORA_RANK + KV_LORA_RANK + QK_ROPE_DIM]
N_GROUPS = 4
EXPERTS_PER_GROUP = 8
N_EXPERTS = N_GROUPS * EXPERTS_PER_GROUP
TOP_K = 2
D_EXPERT = 256
EPS = 1e-6

kernel_name = "fnet_mla_hier_moe_encoder"


def _rmsnorm(x, g):
    xf = x.astype(jnp.float32)
    y = xf * lax.rsqrt(jnp.mean(xf * xf, axis=-1, keepdims=True) + EPS)
    return (y * g.astype(jnp.float32)).astype(x.dtype)


def _rope_tables(seq_len, dtype):
    inv = 1.0 / (ROPE_BASE ** (jnp.arange(0, QK_ROPE_DIM, 2, dtype=jnp.float32) / QK_ROPE_DIM))
    ang = jnp.arange(seq_len, dtype=jnp.float32)[:, None] * inv[None, :]
    return jnp.cos(ang).astype(dtype), jnp.sin(ang).astype(dtype)


def _rotate(x, cos, sin):
    x1, x2 = jnp.split(x, 2, axis=-1)
    return jnp.concatenate([x1 * cos - x2 * sin, x1 * sin + x2 * cos], axis=-1)


def _fourier_mix(u):
    b, s, _ = u.shape
    ug = u.reshape(b, s, FOURIER_GROUPS, FOURIER_GROUP_DIM).astype(jnp.float32)
    f = jnp.fft.fft2(ug, axes=(1, 3), norm="ortho")
    return jnp.real(f).reshape(b, s, FOURIER_DIM).astype(u.dtype)


def _mla(u_q, u_kv, u_kr, q_norm_g, kv_norm_g, w_uq, w_ukv):
    b, s, _ = u_q.shape
    q = jnp.einsum('bsr,rf->bsf', _rmsnorm(u_q, q_norm_g), w_uq).reshape(b, s, N_HEADS, QK_DIM)
    kv = jnp.einsum('bsr,rf->bsf', _rmsnorm(u_kv, kv_norm_g), w_ukv).reshape(b, s, N_HEADS, QK_NOPE_DIM + V_HEAD_DIM)
    q_nope, q_rope = q[..., :QK_NOPE_DIM], q[..., QK_NOPE_DIM:]
    k_nope, v = kv[..., :QK_NOPE_DIM], kv[..., QK_NOPE_DIM:]
    cos, sin = _rope_tables(s, u_q.dtype)
    q_rope = _rotate(q_rope, cos[:, None, :], sin[:, None, :])
    k_rope = _rotate(u_kr, cos, sin)
    scale = QK_DIM ** -0.5
    nb = s // Q_BLOCK

    def blockify(t):
        return jnp.moveaxis(t.reshape(b, nb, Q_BLOCK, t.shape[2], t.shape[3]), 1, 0)

    def attend(qb):
        qn, qr = qb
        sc = (jnp.einsum('bqhd,bkhd->bhqk', qn, k_nope)
              + jnp.einsum('bqhr,bkr->bhqk', qr, k_rope))
        p = jax.nn.softmax(sc.astype(jnp.float32) * scale, axis=-1).astype(v.dtype)
        return jnp.einsum('bhqk,bkhd->bqhd', p, v)

    o = lax.map(attend, (blockify(q_nope), blockify(q_rope)))
    return jnp.moveaxis(o, 0, 1).reshape(b, s, ATTN_DIM)


def _mixer(xn, w_in, q_norm_g, kv_norm_g, w_uq, w_ukv, w_fourier_out, w_attn_out, w_out):
    h = jnp.einsum('bsd,df->bsf', xn, w_in)
    u_f, u_q, u_kv, u_kr, u_g = jnp.split(h, SPLITS, axis=-1)
    y_f = jnp.einsum('bsc,cd->bsd', _fourier_mix(u_f), w_fourier_out)
    y_a = jnp.einsum('bsc,cd->bsd', _mla(u_q, u_kv, u_kr, q_norm_g, kv_norm_g, w_uq, w_ukv), w_attn_out)
    g = jax.nn.sigmoid(u_g.astype(jnp.float32)).astype(xn.dtype)
    g_f, g_a = jnp.split(g, 2, axis=-1)
    merged = g_f * y_f + g_a * y_a
    return jnp.einsum('bsd,de->bse', merged, w_out)


def _hier_moe(xn, w_grp, b_grp, w_exp, b_exp, w_gate, w_up, w_down):
    b, s, d = xn.shape
    n = b * s
    t = xn.reshape(n, d)
    grp_prob = jax.nn.softmax((t @ w_grp + b_grp).astype(jnp.float32), axis=-1)
    grp_p, grp_idx = lax.top_k(grp_prob, 1)
    exp_logits = (t @ w_exp + b_exp).astype(jnp.float32).reshape(n, N_GROUPS, EXPERTS_PER_GROUP)
    sel = jnp.take_along_axis(exp_logits, grp_idx[:, :, None], axis=1)[:, 0]
    exp_prob = jax.nn.softmax(sel, axis=-1)
    top_p, top_i = lax.top_k(exp_prob, TOP_K)
    top_p = top_p / jnp.sum(top_p, axis=-1, keepdims=True)
    expert_id = grp_idx * EXPERTS_PER_GROUP + top_i
    combine = jnp.sum(jax.nn.one_hot(expert_id, N_EXPERTS, dtype=jnp.float32)
                      * (grp_p * top_p)[..., None], axis=1).astype(t.dtype)
    y = jnp.zeros_like(t)
    for e in range(N_EXPERTS):
        hdn = jax.nn.silu(t @ w_gate[e]) * (t @ w_up[e])
        y = y + combine[:, e:e + 1] * (hdn @ w_down[e])
    return y.reshape(b, s, d)


def _trunk(x, attn_norm_g, w_in, q_norm_g, kv_norm_g, w_uq, w_ukv, w_fourier_out, w_attn_out, w_out,
           ffn_norm_g, w_grp, b_grp, w_exp, b_exp, w_gate, w_up, w_down, final_norm_g):
    for l in range(DEPTH):
        xn = _rmsnorm(x, attn_norm_g[l])
        x = x + _mixer(xn, w_in[l], q_norm_g[l], kv_norm_g[l], w_uq[l], w_ukv[l],
                       w_fourier_out[l], w_attn_out[l], w_out[l])
        xn = _rmsnorm(x, ffn_norm_g[l])
        x = x + _hier_moe(xn, w_grp[l], b_grp[l], w_exp[l], b_exp[l], w_gate[l], w_up[l], w_down[l])
    return _rmsnorm(x, final_norm_g)


def setup_inputs(seed: int = 0) -> dict:
    key = jax.random.key(seed)
    ks = jax.random.split(key, 24)
    f32 = jnp.float32

    def nrm(k, shape, fan_in):
        return jax.random.normal(k, shape, f32) * (fan_in ** -0.5)

    def gain(k, shape):
        return 1.0 + 0.05 * jax.random.normal(k, shape, f32)

    L = DEPTH
    return {
        "x_prompt": jax.random.normal(ks[0], (BATCH, SEQ, D_MODEL), f32),
        "x_sample": jax.random.normal(ks[1], (DEC_BATCH, DEC_SEQ, D_MODEL), f32),
        "attn_norm_g": gain(ks[2], (L, D_MODEL)),
        "w_in": nrm(ks[3], (L, D_MODEL, D_IN), D_MODEL),
        "q_norm_g": gain(ks[4], (L, Q_LORA_RANK)),
        "kv_norm_g": gain(ks[5], (L, KV_LORA_RANK)),
        "w_uq": nrm(ks[6], (L, Q_LORA_RANK, N_HEADS * QK_DIM), Q_LORA_RANK),
        "w_ukv": nrm(ks[7], (L, KV_LORA_RANK, N_HEADS * (QK_NOPE_DIM + V_HEAD_DIM)), KV_LORA_RANK),
        "w_fourier_out": nrm(ks[8], (L, FOURIER_DIM, D_MODEL), FOURIER_DIM),
        "w_attn_out": nrm(ks[9], (L, ATTN_DIM, D_MODEL), ATTN_DIM),
        "w_out": nrm(ks[10], (L, D_MODEL, D_MODEL), D_MODEL),
        "ffn_norm_g": gain(ks[11], (L, D_MODEL)),
        "w_grp": nrm(ks[12], (L, D_MODEL, N_GROUPS), D_MODEL),
        "b_grp": 0.01 * jax.random.normal(ks[13], (L, N_GROUPS), f32),
        "w_exp": nrm(ks[14], (L, D_MODEL, N_EXPERTS), D_MODEL),
        "b_exp": 0.01 * jax.random.normal(ks[15], (L, N_EXPERTS), f32),
        "w_gate": nrm(ks[16], (L, N_EXPERTS, D_MODEL, D_EXPERT), D_MODEL),
        "w_up": nrm(ks[17], (L, N_EXPERTS, D_MODEL, D_EXPERT), D_MODEL),
        "w_down": nrm(ks[18], (L, N_EXPERTS, D_EXPERT, D_MODEL), D_EXPERT),
        "final_norm_g": gain(ks[19], (D_MODEL,)),
    }


def reference(x_prompt, x_sample, attn_norm_g, w_in, q_norm_g, kv_norm_g, w_uq, w_ukv, w_fourier_out,
              w_attn_out, w_out, ffn_norm_g, w_grp, b_grp, w_exp, b_exp, w_gate, w_up, w_down, final_norm_g):
    y_prompt = _trunk(x_prompt, attn_norm_g, w_in, q_norm_g, kv_norm_g, w_uq, w_ukv, w_fourier_out,
                      w_attn_out, w_out, ffn_norm_g, w_grp, b_grp, w_exp, b_exp, w_gate, w_up, w_down,
                      final_norm_g)
    y_sample = _trunk(x_sample, attn_norm_g, w_in, q_norm_g, kv_norm_g, w_uq, w_ukv, w_fourier_out,
                      w_attn_out, w_out, ffn_norm_g, w_grp, b_grp, w_exp, b_exp, w_gate, w_up, w_down,
                      final_norm_g)
    return (y_prompt, y_sample)
```

```python
import functools
import math

import numpy as np
import jax
import jax.numpy as jnp
from jax import lax
from jax.experimental import pallas as pl
from jax.experimental.pallas import tpu as pltpu

D_MODEL = 1024
FOURIER_GROUPS = 4
FOURIER_GROUP_DIM = 128
FOURIER_DIM = FOURIER_GROUPS * FOURIER_GROUP_DIM
N_HEADS = 8
QK_NOPE_DIM = 64
QK_ROPE_DIM = 32
V_HEAD_DIM = 64
Q_LORA_RANK = 384
KV_LORA_RANK = 256
QK_DIM = QK_NOPE_DIM + QK_ROPE_DIM
ROPE_BASE = 10000.0
N_GROUPS = 4
EXPERTS_PER_GROUP = 8
N_EXPERTS = N_GROUPS * EXPERTS_PER_GROUP
D_EXPERT = 256
EPS = 1e-6

LANES = 128
HEAD_PAD = N_HEADS * LANES
ROPE_HALF = QK_ROPE_DIM // 2
ROUTER_LANES = LANES
EXPERT_LANE0 = N_GROUPS
VMEM_LIMIT = 56 * 1024 * 1024

BF16 = jnp.bfloat16
F32 = jnp.float32


def _dot(a, b):
    return jnp.dot(a, b, preferred_element_type=F32)


def _rms(x, g):
    return x * lax.rsqrt(jnp.mean(x * x, axis=-1, keepdims=True) + EPS) * g


def _params(*sem):
    return pltpu.CompilerParams(dimension_semantics=sem, vmem_limit_bytes=VMEM_LIMIT)


def _full(shape):
    return pl.BlockSpec(shape, lambda *_: (0,) * len(shape))


def _in_proj_kernel(x_ref, g_ref, wf_ref, wc_ref, wql_ref, wkvl_ref, wkr_ref, wgate_ref, qg_ref, kvg_ref,
                    wuq_ref, wuk_ref, wuv_ref, ekr_ref, cos_ref, sin1_ref, sin2_ref,
                    a_ref, b_ref, q_ref, k_ref, v_ref, gate_ref, *, q_scale):
    xb = _rms(x_ref[...], g_ref[...]).astype(BF16)
    uf = _dot(xb, wf_ref[...]).astype(BF16)
    ab = _dot(uf, wc_ref[...])
    a_ref[...] = ab[:, :FOURIER_DIM].astype(BF16)
    b_ref[...] = ab[:, FOURIER_DIM:].astype(BF16)
    qn = _rms(_dot(xb, wql_ref[...]), qg_ref[...]).astype(BF16)
    kvn = _rms(_dot(xb, wkvl_ref[...]), kvg_ref[...]).astype(BF16)
    ukr = _dot(xb, wkr_ref[...]).astype(BF16)
    q = _dot(qn, wuq_ref[...])
    k = _dot(kvn, wuk_ref[...]) + _dot(ukr, ekr_ref[...])
    v_ref[...] = _dot(kvn, wuv_ref[...]).astype(BF16)
    cos, sin1, sin2 = cos_ref[...], sin1_ref[...], sin2_ref[...]
    q_up = pltpu.roll(q, HEAD_PAD - ROPE_HALF, 1)
    q_dn = pltpu.roll(q, ROPE_HALF, 1)
    k_up = pltpu.roll(k, HEAD_PAD - ROPE_HALF, 1)
    k_dn = pltpu.roll(k, ROPE_HALF, 1)
    for h in range(N_HEADS):
        sl = slice(h * LANES, (h + 1) * LANES)
        q_ref[:, sl] = ((q[:, sl] * cos + q_up[:, sl] * sin1 + q_dn[:, sl] * sin2) * q_scale).astype(BF16)
        k_ref[:, sl] = (k[:, sl] * cos + k_up[:, sl] * sin1 + k_dn[:, sl] * sin2).astype(BF16)
    gate_ref[...] = jax.nn.sigmoid(_dot(xb, wgate_ref[...])).astype(BF16)


def _in_proj(x2d, seq, lw, rope):
    t = x2d.shape[0]
    tm = min(512, seq)
    n_seq_tiles = seq // tm
    row = lambda w: pl.BlockSpec((tm, w), lambda i: (i, 0))
    pos = pl.BlockSpec((tm, LANES), lambda i: (i % n_seq_tiles, 0))
    weights = [lw["attn_g"], lw["wf"], lw["wc"], lw["wql"], lw["wkvl"], lw["wkr"], lw["wgate"], lw["qg"], lw["kvg"],
               lw["wuq"], lw["wuk"], lw["wuv"], lw["ekr"]]
    out_w = [FOURIER_DIM, FOURIER_DIM, HEAD_PAD, HEAD_PAD, HEAD_PAD, 2 * D_MODEL]
    return pl.pallas_call(
        functools.partial(_in_proj_kernel, q_scale=(QK_DIM ** -0.5) * math.log2(math.e)),
        grid=(t // tm,),
        in_specs=[row(D_MODEL)] + [_full(w.shape) for w in weights] + [pos, pos, pos],
        out_specs=[row(w) for w in out_w],
        out_shape=[jax.ShapeDtypeStruct((t, w), BF16) for w in out_w],
        compiler_params=_params("parallel"),
        name="in_proj",
    )(x2d, *weights, *rope)


def _seq_dft_kernel(cs_ref, nss_ref, a_ref, b_ref, y_ref, acc_ref, *, scale):
    kk = pl.program_id(2)

    @pl.when(kk == 0)
    def _():
        acc_ref[...] = jnp.zeros_like(acc_ref)

    acc_ref[...] += _dot(cs_ref[...], a_ref[0]) + _dot(nss_ref[...], b_ref[0])

    @pl.when(kk == pl.num_programs(2) - 1)
    def _():
        y_ref[0] = (acc_ref[...] * scale).astype(BF16)


def _seq_dft(a3, b3, cs, nss):
    bsz, seq, _ = a3.shape
    tm = min(1024, seq)
    tk = min(1024, seq)
    mat = pl.BlockSpec((tm, tk), lambda b, i, k: (i, k))
    rhs = pl.BlockSpec((1, tk, FOURIER_DIM), lambda b, i, k: (b, k, 0))
    return pl.pallas_call(
        functools.partial(_seq_dft_kernel, scale=(seq * FOURIER_GROUP_DIM) ** -0.5),
        grid=(bsz, seq // tm, seq // tk),
        in_specs=[mat, mat, rhs, rhs],
        out_specs=pl.BlockSpec((1, tm, FOURIER_DIM), lambda b, i, k: (b, i, 0)),
        out_shape=jax.ShapeDtypeStruct((bsz, seq, FOURIER_DIM), BF16),
        scratch_shapes=[pltpu.VMEM((tm, FOURIER_DIM), F32)],
        compiler_params=_params("parallel", "parallel", "arbitrary"),
        name="seq_dft",
    )(cs, nss, a3, b3)


def _attn_kernel(q_ref, k_ref, v_ref, o_ref, m_ref, l_ref, acc_ref, *, tk, nk):
    q = q_ref[0]
    m_ref[...] = jnp.full_like(m_ref, -jnp.inf)
    l_ref[...] = jnp.zeros_like(l_ref)
    acc_ref[...] = jnp.zeros_like(acc_ref)

    def body(kc, carry):
        ks = pl.multiple_of(kc * tk, tk)
        kb = k_ref[0, pl.ds(ks, tk), :]
        vb = v_ref[0, pl.ds(ks, tk), :]
        s = lax.dot_general(q, kb, (((1,), (1,)), ((), ())), preferred_element_type=F32)
        m_prev = m_ref[...]
        m_new = jnp.maximum(m_prev, jnp.max(s, axis=1, keepdims=True))
        alpha = jnp.exp2(m_prev - m_new)
        p = jnp.exp2(s - m_new)
        l_ref[...] = alpha * l_ref[...] + jnp.sum(p, axis=1, keepdims=True)
        acc_ref[...] = alpha * acc_ref[...] + _dot(p.astype(BF16), vb)
        m_ref[...] = m_new
        return carry

    lax.fori_loop(0, nk, body, 0)
    o_ref[0] = (acc_ref[...] / l_ref[...]).astype(BF16)


def _attention(q3, k3, v3):
    bsz, seq, _ = q3.shape
    tq = min(512, seq)
    tk = min(512, seq)
    qo = pl.BlockSpec((1, tq, LANES), lambda b, h, i: (b, i, h))
    kv = pl.BlockSpec((1, seq, LANES), lambda b, h, i: (b, 0, h))
    return pl.pallas_call(
        functools.partial(_attn_kernel, tk=tk, nk=seq // tk),
        grid=(bsz, N_HEADS, seq // tq),
        in_specs=[qo, kv, kv],
        out_specs=qo,
        out_shape=jax.ShapeDtypeStruct((bsz, seq, HEAD_PAD), BF16),
        scratch_shapes=[pltpu.VMEM((tq, 1), F32), pltpu.VMEM((tq, 1), F32), pltpu.VMEM((tq, LANES), F32)],
        compiler_params=_params("parallel", "parallel", "arbitrary"),
        name="attn",
    )(q3, k3, v3)


def _merge_kernel(x_ref, f_ref, o_ref, gate_ref, wfo_ref, wao_ref, wout_ref, fg_ref, wrh_ref, wrl_ref, br_ref,
                  x1_ref, xn_ref, comb_ref):
    y_f = _dot(f_ref[...], wfo_ref[...])
    y_a = _dot(o_ref[...], wao_ref[...])
    gate = gate_ref[...].astype(F32)
    merged = gate[:, :D_MODEL] * y_f + gate[:, D_MODEL:] * y_a
    x1 = x_ref[...] + _dot(merged.astype(BF16), wout_ref[...])
    x1_ref[...] = x1
    xn = _rms(x1, fg_ref[...])
    xh = xn.astype(BF16)
    xn_ref[...] = xh
    xl = (xn - xh.astype(F32)).astype(BF16)
    logit = _dot(xh, wrh_ref[...]) + _dot(xl, wrh_ref[...]) + _dot(xh, wrl_ref[...]) + br_ref[...]
    lane = lax.broadcasted_iota(jnp.int32, logit.shape, 1)
    neg = jnp.float32(-jnp.inf)

    def top(mask):
        val = jnp.max(jnp.where(mask, logit, neg), axis=1, keepdims=True)
        idx = jnp.min(jnp.where(mask & (logit == val), lane, ROUTER_LANES), axis=1, keepdims=True)
        return val, idx

    is_grp = lane < N_GROUPS
    g_max, g_idx = top(is_grp)
    g_sum = jnp.sum(jnp.where(is_grp, jnp.exp(logit - g_max), 0.0), axis=1, keepdims=True)
    grp_p = 1.0 / g_sum
    lo = EXPERT_LANE0 + g_idx * EXPERTS_PER_GROUP
    in_grp = (lane >= lo) & (lane < lo + EXPERTS_PER_GROUP)
    l1, i1 = top(in_grp)
    l2, i2 = top(in_grp & (lane != i1))
    e2 = jnp.exp(l2 - l1)
    w1 = 1.0 / (1.0 + e2)
    w2 = e2 / (1.0 + e2)
    comb_ref[...] = jnp.where(lane == i1, grp_p * w1, jnp.where(lane == i2, grp_p * w2, 0.0))


def _merge(x2d, f2d, o2d, gate2d, lw):
    t = x2d.shape[0]
    tm = min(512, t)
    row = lambda w: pl.BlockSpec((tm, w), lambda i: (i, 0))
    weights = [lw["wfo"], lw["wao"], lw["wout"], lw["ffn_g"], lw["wr_hi"], lw["wr_lo"], lw["br"]]
    return pl.pallas_call(
        _merge_kernel,
        grid=(t // tm,),
        in_specs=[row(D_MODEL), row(FOURIER_DIM), row(HEAD_PAD), row(2 * D_MODEL)] + [_full(w.shape) for w in weights],
        out_specs=[row(D_MODEL), row(D_MODEL), row(ROUTER_LANES)],
        out_shape=[jax.ShapeDtypeStruct((t, D_MODEL), F32), jax.ShapeDtypeStruct((t, D_MODEL), BF16),
                   jax.ShapeDtypeStruct((t, ROUTER_LANES), F32)],
        compiler_params=_params("parallel"),
        name="merge",
    )(x2d, f2d, o2d, gate2d, *weights)


def _moe_kernel(x1_ref, xn_ref, comb_ref, wg_ref, wu_ref, wd_ref, fin_ref, out_ref, acc_ref, *, ec, final_norm):
    e = pl.program_id(1)

    @pl.when(e == 0)
    def _():
        acc_ref[...] = jnp.zeros_like(acc_ref)

    xb = xn_ref[...]
    comb = comb_ref[...]
    lane = lax.broadcasted_iota(jnp.int32, comb.shape, 1)
    y = None
    for j in range(ec):
        c = jnp.sum(jnp.where(lane == EXPERT_LANE0 + e * ec + j, comb, 0.0), axis=1, keepdims=True)
        hg = _dot(xb, wg_ref[j])
        hu = _dot(xb, wu_ref[j])
        hs = (hg * jax.nn.sigmoid(hg) * hu * c).astype(BF16)
        yj = _dot(hs, wd_ref[j])
        y = yj if y is None else y + yj
    acc_ref[...] += y

    @pl.when(e == pl.num_programs(1) - 1)
    def _():
        x2 = x1_ref[...] + acc_ref[...]
        out_ref[...] = _rms(x2, fin_ref[...]) if final_norm else x2


def _moe(x1, xn, comb, lw, final_g, final_norm):
    t = x1.shape[0]
    tm = min(1024, t)
    ec = 4
    row = lambda w: pl.BlockSpec((tm, w), lambda i, e: (i, 0))
    wspec = lambda a, b: pl.BlockSpec((ec, a, b), lambda i, e: (e, 0, 0))
    return pl.pallas_call(
        functools.partial(_moe_kernel, ec=ec, final_norm=final_norm),
        grid=(t // tm, N_EXPERTS // ec),
        in_specs=[row(D_MODEL), row(D_MODEL), row(ROUTER_LANES), wspec(D_MODEL, D_EXPERT), wspec(D_MODEL, D_EXPERT),
                  wspec(D_EXPERT, D_MODEL), pl.BlockSpec((1, D_MODEL), lambda i, e: (0, 0))],
        out_specs=row(D_MODEL),
        out_shape=jax.ShapeDtypeStruct((t, D_MODEL), F32),
        scratch_shapes=[pltpu.VMEM((tm, D_MODEL), F32)],
        compiler_params=_params("parallel", "arbitrary"),
        name="moe",
    )(x1, xn, comb, lw["wg"], lw["wu"], lw["wd"], final_g)


def _channel_dft_matrix():
    n = FOURIER_GROUP_DIM
    ang = 2.0 * np.pi * ((np.arange(n)[:, None] * np.arange(n)[None, :]) % n) / n
    wc = np.zeros((FOURIER_DIM, 2 * FOURIER_DIM), np.float32)
    for g in range(FOURIER_GROUPS):
        sl = slice(g * n, (g + 1) * n)
        wc[sl, g * n:(g + 1) * n] = np.cos(ang)
        wc[sl, FOURIER_DIM + g * n:FOURIER_DIM + (g + 1) * n] = np.sin(ang)
    return jnp.asarray(wc, BF16)


def _rope_key_placement():
    ekr = np.zeros((LANES, HEAD_PAD), np.float32)
    for h in range(N_HEADS):
        for r in range(QK_ROPE_DIM):
            ekr[r, h * LANES + QK_NOPE_DIM + r] = 1.0
    return jnp.asarray(ekr, BF16)


def _seq_tables(seq):
    idx = jnp.arange(seq, dtype=jnp.int32)
    ang = ((idx[:, None] * idx[None, :]) % seq).astype(F32) * (2.0 * math.pi / seq)
    cs = jnp.cos(ang).astype(BF16)
    nss = (-jnp.sin(ang)).astype(BF16)
    inv = 1.0 / (ROPE_BASE ** (jnp.arange(0, QK_ROPE_DIM, 2, dtype=F32) / QK_ROPE_DIM))
    rang = jnp.arange(seq, dtype=F32)[:, None] * inv[None, :]
    c, s = jnp.cos(rang), jnp.sin(rang)
    z = lambda w: jnp.zeros((seq, w), F32)
    tail = LANES - QK_DIM
    cos = jnp.concatenate([jnp.ones((seq, QK_NOPE_DIM), F32), c, c, z(tail)], axis=1)
    sin1 = jnp.concatenate([z(QK_NOPE_DIM), -s, z(ROPE_HALF), z(tail)], axis=1)
    sin2 = jnp.concatenate([z(QK_NOPE_DIM), z(ROPE_HALF), s, z(tail)], axis=1)
    return cs, nss, (cos, sin1, sin2)


def _pad_heads(w, per_head, lo, hi):
    r = w.shape[0]
    w3 = w.reshape(r, N_HEADS, per_head)[:, :, lo:hi]
    return jnp.pad(w3, ((0, 0), (0, 0), (0, LANES - (hi - lo)))).reshape(r, HEAD_PAD)


def _layer_weights(l, attn_norm_g, w_in, q_norm_g, kv_norm_g, w_uq, w_ukv, w_fourier_out, w_attn_out, w_out,
                   ffn_norm_g, w_grp, b_grp, w_exp, b_exp, w_gate, w_up, w_down):
    s0, s1, s2, s3 = (FOURIER_DIM, FOURIER_DIM + Q_LORA_RANK, FOURIER_DIM + Q_LORA_RANK + KV_LORA_RANK,
                      FOURIER_DIM + Q_LORA_RANK + KV_LORA_RANK + QK_ROPE_DIM)
    wi = w_in[l]
    wr = jnp.concatenate([w_grp[l], w_exp[l]], axis=1)
    wr = jnp.pad(wr, ((0, 0), (0, ROUTER_LANES - wr.shape[1])))
    wr_hi = wr.astype(BF16)
    br = jnp.concatenate([b_grp[l], b_exp[l]])
    wao = jnp.pad(w_attn_out[l].reshape(N_HEADS, V_HEAD_DIM, D_MODEL), ((0, 0), (0, LANES - V_HEAD_DIM), (0, 0)))
    return {
        "attn_g": attn_norm_g[l][None, :],
        "wf": wi[:, :s0].astype(BF16),
        "wc": _channel_dft_matrix(),
        "wql": wi[:, s0:s1].astype(BF16),
        "wkvl": wi[:, s1:s2].astype(BF16),
        "wkr": jnp.pad(wi[:, s2:s3], ((0, 0), (0, LANES - QK_ROPE_DIM))).astype(BF16),
        "wgate": wi[:, s3:].astype(BF16),
        "qg": q_norm_g[l][None, :],
        "kvg": kv_norm_g[l][None, :],
        "wuq": _pad_heads(w_uq[l], QK_DIM, 0, QK_DIM).astype(BF16),
        "wuk": _pad_heads(w_ukv[l], QK_NOPE_DIM + V_HEAD_DIM, 0, QK_NOPE_DIM).astype(BF16),
        "wuv": _pad_heads(w_ukv[l], QK_NOPE_DIM + V_HEAD_DIM, QK_NOPE_DIM, QK_NOPE_DIM + V_HEAD_DIM).astype(BF16),
        "ekr": _rope_key_placement(),
        "wfo": w_fourier_out[l].astype(BF16),
        "wao": wao.reshape(HEAD_PAD, D_MODEL).astype(BF16),
        "wout": w_out[l].astype(BF16),
        "ffn_g": ffn_norm_g[l][None, :],
        "wr_hi": wr_hi,
        "wr_lo": (wr - wr_hi.astype(F32)).astype(BF16),
        "br": jnp.pad(br, (0, ROUTER_LANES - br.shape[0]))[None, :],
        "wg": w_gate[l].astype(BF16),
        "wu": w_up[l].astype(BF16),
        "wd": w_down[l].astype(BF16),
    }


def _trunk(x, layers, final_g, tables):
    bsz, seq, d = x.shape
    t = bsz * seq
    cs, nss, rope = tables
    x2d = x.reshape(t, d)
    for l, lw in enumerate(layers):
        a, b, q, k, v, gate = _in_proj(x2d, seq, lw, rope)
        shp = lambda z: z.reshape(bsz, seq, z.shape[-1])
        f = _seq_dft(shp(a), shp(b), cs, nss).reshape(t, FOURIER_DIM)
        o = _attention(shp(q), shp(k), shp(v)).reshape(t, HEAD_PAD)
        x1, xn, comb = _merge(x2d, f, o, gate, lw)
        x2d = _moe(x1, xn, comb, lw, final_g, final_norm=(l == len(layers) - 1))
    return x2d.reshape(bsz, seq, d)


def kernel(x_prompt, x_sample, attn_norm_g, w_in, q_norm_g, kv_norm_g, w_uq, w_ukv, w_fourier_out, w_attn_out, w_out, ffn_norm_g, w_grp, b_grp, w_exp, b_exp, w_gate, w_up, w_down, final_norm_g):
    depth = w_in.shape[0]
    layers = [_layer_weights(l, attn_norm_g, w_in, q_norm_g, kv_norm_g, w_uq, w_ukv, w_fourier_out, w_attn_out,
                             w_out, ffn_norm_g, w_grp, b_grp, w_exp, b_exp, w_gate, w_up, w_down)
              for l in range(depth)]
    final_g = final_norm_g[None, :]
    outs = []
    for x in (x_prompt, x_sample):
        outs.append(_trunk(x, layers, final_g, _seq_tables(x.shape[1])))
    return tuple(outs)
```

```python
import functools
import math

import numpy as np
import jax
import jax.numpy as jnp
from jax import lax
from jax.experimental import pallas as pl
from jax.experimental.pallas import tpu as pltpu

D_MODEL = 1024
FOURIER_GROUPS = 4
FOURIER_GROUP_DIM = 128
FOURIER_DIM = FOURIER_GROUPS * FOURIER_GROUP_DIM
N_HEADS = 8
QK_NOPE_DIM = 64
QK_ROPE_DIM = 32
V_HEAD_DIM = 64
Q_LORA_RANK = 384
KV_LORA_RANK = 256
QK_DIM = QK_NOPE_DIM + QK_ROPE_DIM
ROPE_BASE = 10000.0
N_GROUPS = 4
EXPERTS_PER_GROUP = 8
N_EXPERTS = N_GROUPS * EXPERTS_PER_GROUP
D_EXPERT = 256
EPS = 1e-6

LANES = 128
HEAD_PAD = N_HEADS * LANES
ROPE_HALF = QK_ROPE_DIM // 2
ONES_ROW = V_HEAD_DIM
ROUTER_LANES = LANES
EXPERT_LANE0 = N_GROUPS
VMEM_LIMIT = 56 * 1024 * 1024

BF16 = jnp.bfloat16
F32 = jnp.float32


def _dot(a, b):
    return jnp.dot(a, b, preferred_element_type=F32)


def _rms(x, g):
    return x * lax.rsqrt(jnp.mean(x * x, axis=-1, keepdims=True) + EPS) * g


def _params(*sem):
    return pltpu.CompilerParams(dimension_semantics=sem, vmem_limit_bytes=VMEM_LIMIT)


def _full(shape):
    return pl.BlockSpec(shape, lambda *_: (0,) * len(shape))


def _in_proj_kernel(x_ref, g_ref, wf_ref, wc_ref, wql_ref, wkvl_ref, wkr_ref, wgate_ref, qg_ref, kvg_ref,
                    wuq_ref, wuk_ref, wuvt_ref, ekr_ref, cos_ref, sin1_ref, sin2_ref,
                    a_ref, b_ref, q_ref, k_ref, gate_ref, vt_ref, *, q_scale):
    xb = _rms(x_ref[...], g_ref[...]).astype(BF16)
    uf = _dot(xb, wf_ref[...]).astype(BF16)
    ab = _dot(uf, wc_ref[...])
    a_ref[...] = ab[:, :FOURIER_DIM].astype(BF16)
    b_ref[...] = ab[:, FOURIER_DIM:].astype(BF16)
    qn = _rms(_dot(xb, wql_ref[...]), qg_ref[...]).astype(BF16)
    kvn = _rms(_dot(xb, wkvl_ref[...]), kvg_ref[...]).astype(BF16)
    ukr = _dot(xb, wkr_ref[...]).astype(BF16)
    q = _dot(qn, wuq_ref[...])
    k = _dot(kvn, wuk_ref[...]) + _dot(ukr, ekr_ref[...])
    vt = lax.dot_general(wuvt_ref[...], kvn, (((1,), (1,)), ((), ())), preferred_element_type=F32)
    head_row = lax.broadcasted_iota(jnp.int32, vt.shape, 0) % LANES
    vt_ref[0, 0] = jnp.where(head_row == ONES_ROW, 1.0, vt).astype(BF16)
    cos, sin1, sin2 = cos_ref[...], sin1_ref[...], sin2_ref[...]
    q_up = pltpu.roll(q, HEAD_PAD - ROPE_HALF, 1)
    q_dn = pltpu.roll(q, ROPE_HALF, 1)
    k_up = pltpu.roll(k, HEAD_PAD - ROPE_HALF, 1)
    k_dn = pltpu.roll(k, ROPE_HALF, 1)
    for h in range(N_HEADS):
        sl = slice(h * LANES, (h + 1) * LANES)
        q_ref[:, sl] = ((q[:, sl] * cos + q_up[:, sl] * sin1 + q_dn[:, sl] * sin2) * q_scale).astype(BF16)
        k_ref[:, sl] = (k[:, sl] * cos + k_up[:, sl] * sin1 + k_dn[:, sl] * sin2).astype(BF16)
    gate_ref[...] = jax.nn.sigmoid(_dot(xb, wgate_ref[...])).astype(BF16)


def _in_proj(x2d, seq, lw, rope):
    t = x2d.shape[0]
    tm = _key_chunk(seq)
    n_seq_tiles = seq // tm
    row = lambda w: pl.BlockSpec((tm, w), lambda i: (i, 0))
    pos = pl.BlockSpec((tm, LANES), lambda i: (i % n_seq_tiles, 0))
    weights = [lw["attn_g"], lw["wf"], lw["wc"], lw["wql"], lw["wkvl"], lw["wkr"], lw["wgate"], lw["qg"], lw["kvg"],
               lw["wuq"], lw["wuk"], lw["wuvt"], lw["ekr"]]
    out_w = [FOURIER_DIM, FOURIER_DIM, HEAD_PAD, HEAD_PAD, 2 * D_MODEL]
    vt_spec = pl.BlockSpec((1, 1, HEAD_PAD, tm), lambda i: (i // n_seq_tiles, i % n_seq_tiles, 0, 0))
    vt_shape = jax.ShapeDtypeStruct((t // seq, n_seq_tiles, HEAD_PAD, tm), BF16)
    return pl.pallas_call(
        functools.partial(_in_proj_kernel, q_scale=(QK_DIM ** -0.5) * math.log2(math.e)),
        grid=(t // tm,),
        in_specs=[row(D_MODEL)] + [_full(w.shape) for w in weights] + [pos, pos, pos],
        out_specs=[row(w) for w in out_w] + [vt_spec],
        out_shape=[jax.ShapeDtypeStruct((t, w), BF16) for w in out_w] + [vt_shape],
        compiler_params=_params("parallel"),
        name="in_proj",
    )(x2d, *weights, *rope)


def _seq_dft_kernel(cs_ref, nss_ref, a_ref, b_ref, y_ref, acc_ref, *, scale):
    kk = pl.program_id(2)

    @pl.when(kk == 0)
    def _():
        acc_ref[...] = jnp.zeros_like(acc_ref)

    acc_ref[...] += _dot(cs_ref[...], a_ref[0]) + _dot(nss_ref[...], b_ref[0])

    @pl.when(kk == pl.num_programs(2) - 1)
    def _():
        y_ref[0] = (acc_ref[...] * scale).astype(BF16)


def _seq_dft(a3, b3, cs, nss):
    bsz, seq, _ = a3.shape
    tm = min(1024, seq)
    tk = min(1024, seq)
    mat = pl.BlockSpec((tm, tk), lambda b, i, k: (i, k))
    rhs = pl.BlockSpec((1, tk, FOURIER_DIM), lambda b, i, k: (b, k, 0))
    return pl.pallas_call(
        functools.partial(_seq_dft_kernel, scale=(seq * FOURIER_GROUP_DIM) ** -0.5),
        grid=(bsz, seq // tm, seq // tk),
        in_specs=[mat, mat, rhs, rhs],
        out_specs=pl.BlockSpec((1, tm, FOURIER_DIM), lambda b, i, k: (b, i, 0)),
        out_shape=jax.ShapeDtypeStruct((bsz, seq, FOURIER_DIM), BF16),
        scratch_shapes=[pltpu.VMEM((tm, FOURIER_DIM), F32)],
        compiler_params=_params("parallel", "parallel", "arbitrary"),
        name="seq_dft",
    )(cs, nss, a3, b3)


def _key_chunk(seq):
    return min(512, seq // 2)


def _attn_kernel(q_ref, k_ref, vt_ref, o_ref, m_ref, acc_ref, s_ref, *, tk, nk):
    q = q_ref[0]
    m_ref[...] = jnp.full_like(m_ref, -jnp.inf)
    acc_ref[...] = jnp.zeros_like(acc_ref)

    def scores(kc):
        ks = pl.multiple_of(kc * tk, tk)
        return lax.dot_general(k_ref[0, pl.ds(ks, tk), :], q, (((1,), (1,)), ((), ())),
                               preferred_element_type=F32)

    def softmax_pv(kc, st):
        m_prev = m_ref[...]
        m_new = jnp.maximum(m_prev, jnp.max(st, axis=0, keepdims=True))
        alpha = jnp.exp2(m_prev - m_new)
        p = jnp.exp2(st - m_new).astype(BF16)
        acc_ref[...] = alpha * acc_ref[...] + _dot(vt_ref[0, kc], p)
        m_ref[...] = m_new

    s_ref[0] = scores(0)

    def body(j, carry):
        c = 2 * j
        s_ref[1] = scores(c + 1)
        softmax_pv(c, s_ref[0])
        s_ref[0] = scores(jnp.minimum(c + 2, nk - 1))
        softmax_pv(c + 1, s_ref[1])
        return carry

    lax.fori_loop(0, nk // 2, body, 0)
    acc = acc_ref[...]
    o_ref[0] = (acc / acc[ONES_ROW:ONES_ROW + 1, :]).T.astype(BF16)


def _attention(q3, k3, vt4):
    bsz, seq, _ = q3.shape
    tq = min(1024, seq)
    tk = _key_chunk(seq)
    nk = seq // tk
    qo = pl.BlockSpec((1, tq, LANES), lambda b, h, i: (b, i, h))
    kk = pl.BlockSpec((1, seq, LANES), lambda b, h, i: (b, 0, h))
    vt = pl.BlockSpec((1, nk, LANES, tk), lambda b, h, i: (b, 0, h, 0))
    return pl.pallas_call(
        functools.partial(_attn_kernel, tk=tk, nk=nk),
        grid=(bsz, N_HEADS, seq // tq),
        in_specs=[qo, kk, vt],
        out_specs=qo,
        out_shape=jax.ShapeDtypeStruct((bsz, seq, HEAD_PAD), BF16),
        scratch_shapes=[pltpu.VMEM((1, tq), F32), pltpu.VMEM((LANES, tq), F32), pltpu.VMEM((2, tk, tq), F32)],
        compiler_params=_params("parallel", "parallel", "arbitrary"),
        name="attn",
    )(q3, k3, vt4)


def _merge_kernel(x_ref, f_ref, o_ref, gate_ref, wfo_ref, wao_ref, wout_ref, fg_ref, wrh_ref, wrl_ref, br_ref,
                  x1_ref, xn_ref, comb_ref):
    y_f = _dot(f_ref[...], wfo_ref[...])
    y_a = _dot(o_ref[...], wao_ref[...])
    gate = gate_ref[...].astype(F32)
    merged = gate[:, :D_MODEL] * y_f + gate[:, D_MODEL:] * y_a
    x1 = x_ref[...] + _dot(merged.astype(BF16), wout_ref[...])
    x1_ref[...] = x1
    xn = _rms(x1, fg_ref[...])
    xh = xn.astype(BF16)
    xn_ref[...] = xh
    xl = (xn - xh.astype(F32)).astype(BF16)
    logit = _dot(xh, wrh_ref[...]) + _dot(xl, wrh_ref[...]) + _dot(xh, wrl_ref[...]) + br_ref[...]
    lane = lax.broadcasted_iota(jnp.int32, logit.shape, 1)
    neg = jnp.float32(-jnp.inf)

    def top(mask):
        val = jnp.max(jnp.where(mask, logit, neg), axis=1, keepdims=True)
        idx = jnp.min(jnp.where(mask & (logit == val), lane, ROUTER_LANES), axis=1, keepdims=True)
        return val, idx

    is_grp = lane < N_GROUPS
    g_max, g_idx = top(is_grp)
    g_sum = jnp.sum(jnp.where(is_grp, jnp.exp(logit - g_max), 0.0), axis=1, keepdims=True)
    grp_p = 1.0 / g_sum
    lo = EXPERT_LANE0 + g_idx * EXPERTS_PER_GROUP
    in_grp = (lane >= lo) & (lane < lo + EXPERTS_PER_GROUP)
    l1, i1 = top(in_grp)
    l2, i2 = top(in_grp & (lane != i1))
    e2 = jnp.exp(l2 - l1)
    w1 = 1.0 / (1.0 + e2)
    w2 = e2 / (1.0 + e2)
    comb_ref[...] = jnp.where(lane == i1, grp_p * w1, jnp.where(lane == i2, grp_p * w2, 0.0))


def _merge(x2d, f2d, o2d, gate2d, lw):
    t = x2d.shape[0]
    tm = min(512, t)
    row = lambda w: pl.BlockSpec((tm, w), lambda i: (i, 0))
    weights = [lw["wfo"], lw["wao"], lw["wout"], lw["ffn_g"], lw["wr_hi"], lw["wr_lo"], lw["br"]]
    return pl.pallas_call(
        _merge_kernel,
        grid=(t // tm,),
        in_specs=[row(D_MODEL), row(FOURIER_DIM), row(HEAD_PAD), row(2 * D_MODEL)] + [_full(w.shape) for w in weights],
        out_specs=[row(D_MODEL), row(D_MODEL), row(ROUTER_LANES)],
        out_shape=[jax.ShapeDtypeStruct((t, D_MODEL), F32), jax.ShapeDtypeStruct((t, D_MODEL), BF16),
                   jax.ShapeDtypeStruct((t, ROUTER_LANES), F32)],
        compiler_params=_params("parallel"),
        name="merge",
    )(x2d, f2d, o2d, gate2d, *weights)


def _moe_kernel(x1_ref, xn_ref, comb_ref, wg_ref, wu_ref, wd_ref, fin_ref, out_ref, acc_ref, *, ec, final_norm):
    e = pl.program_id(1)

    @pl.when(e == 0)
    def _():
        acc_ref[...] = jnp.zeros_like(acc_ref)

    xb = xn_ref[...]
    comb = comb_ref[...]
    lane = lax.broadcasted_iota(jnp.int32, comb.shape, 1)
    y = None
    for j in range(ec):
        c = jnp.sum(jnp.where(lane == EXPERT_LANE0 + e * ec + j, comb, 0.0), axis=1, keepdims=True)
        hg = _dot(xb, wg_ref[j])
        hu = _dot(xb, wu_ref[j])
        hs = (hg * jax.nn.sigmoid(hg) * hu * c).astype(BF16)
        yj = _dot(hs, wd_ref[j])
        y = yj if y is None else y + yj
    acc_ref[...] += y

    @pl.when(e == pl.num_programs(1) - 1)
    def _():
        x2 = x1_ref[...] + acc_ref[...]
        out_ref[...] = _rms(x2, fin_ref[...]) if final_norm else x2


def _moe(x1, xn, comb, lw, final_g, final_norm):
    t = x1.shape[0]
    tm = min(1024, t)
    ec = 4
    row = lambda w: pl.BlockSpec((tm, w), lambda i, e: (i, 0))
    wspec = lambda a, b: pl.BlockSpec((ec, a, b), lambda i, e: (e, 0, 0))
    return pl.pallas_call(
        functools.partial(_moe_kernel, ec=ec, final_norm=final_norm),
        grid=(t // tm, N_EXPERTS // ec),
        in_specs=[row(D_MODEL), row(D_MODEL), row(ROUTER_LANES), wspec(D_MODEL, D_EXPERT), wspec(D_MODEL, D_EXPERT),
                  wspec(D_EXPERT, D_MODEL), pl.BlockSpec((1, D_MODEL), lambda i, e: (0, 0))],
        out_specs=row(D_MODEL),
        out_shape=jax.ShapeDtypeStruct((t, D_MODEL), F32),
        scratch_shapes=[pltpu.VMEM((tm, D_MODEL), F32)],
        compiler_params=_params("parallel", "arbitrary"),
        name="moe",
    )(x1, xn, comb, lw["wg"], lw["wu"], lw["wd"], final_g)


def _channel_dft_matrix():
    n = FOURIER_GROUP_DIM
    ang = 2.0 * np.pi * ((np.arange(n)[:, None] * np.arange(n)[None, :]) % n) / n
    wc = np.zeros((FOURIER_DIM, 2 * FOURIER_DIM), np.float32)
    for g in range(FOURIER_GROUPS):
        sl = slice(g * n, (g + 1) * n)
        wc[sl, g * n:(g + 1) * n] = np.cos(ang)
        wc[sl, FOURIER_DIM + g * n:FOURIER_DIM + (g + 1) * n] = np.sin(ang)
    return jnp.asarray(wc, BF16)


def _rope_key_placement():
    ekr = np.zeros((LANES, HEAD_PAD), np.float32)
    for h in range(N_HEADS):
        for r in range(QK_ROPE_DIM):
            ekr[r, h * LANES + QK_NOPE_DIM + r] = 1.0
    return jnp.asarray(ekr, BF16)


def _seq_tables(seq):
    idx = jnp.arange(seq, dtype=jnp.int32)
    ang = ((idx[:, None] * idx[None, :]) % seq).astype(F32) * (2.0 * math.pi / seq)
    cs = jnp.cos(ang).astype(BF16)
    nss = (-jnp.sin(ang)).astype(BF16)
    inv = 1.0 / (ROPE_BASE ** (jnp.arange(0, QK_ROPE_DIM, 2, dtype=F32) / QK_ROPE_DIM))
    rang = jnp.arange(seq, dtype=F32)[:, None] * inv[None, :]
    c, s = jnp.cos(rang), jnp.sin(rang)
    z = lambda w: jnp.zeros((seq, w), F32)
    tail = LANES - QK_DIM
    cos = jnp.concatenate([jnp.ones((seq, QK_NOPE_DIM), F32), c, c, z(tail)], axis=1)
    sin1 = jnp.concatenate([z(QK_NOPE_DIM), -s, z(ROPE_HALF), z(tail)], axis=1)
    sin2 = jnp.concatenate([z(QK_NOPE_DIM), z(ROPE_HALF), s, z(tail)], axis=1)
    return cs, nss, (cos, sin1, sin2)


def _pad_heads(w, per_head, lo, hi):
    r = w.shape[0]
    w3 = w.reshape(r, N_HEADS, per_head)[:, :, lo:hi]
    return jnp.pad(w3, ((0, 0), (0, 0), (0, LANES - (hi - lo)))).reshape(r, HEAD_PAD)


def _layer_weights(l, attn_norm_g, w_in, q_norm_g, kv_norm_g, w_uq, w_ukv, w_fourier_out, w_attn_out, w_out,
                   ffn_norm_g, w_grp, b_grp, w_exp, b_exp, w_gate, w_up, w_down):
    s0, s1, s2, s3 = (FOURIER_DIM, FOURIER_DIM + Q_LORA_RANK, FOURIER_DIM + Q_LORA_RANK + KV_LORA_RANK,
                      FOURIER_DIM + Q_LORA_RANK + KV_LORA_RANK + QK_ROPE_DIM)
    wi = w_in[l]
    wr = jnp.concatenate([w_grp[l], w_exp[l]], axis=1)
    wr = jnp.pad(wr, ((0, 0), (0, ROUTER_LANES - wr.shape[1])))
    wr_hi = wr.astype(BF16)
    br = jnp.concatenate([b_grp[l], b_exp[l]])
    wao = jnp.pad(w_attn_out[l].reshape(N_HEADS, V_HEAD_DIM, D_MODEL), ((0, 0), (0, LANES - V_HEAD_DIM), (0, 0)))
    return {
        "attn_g": attn_norm_g[l][None, :],
        "wf": wi[:, :s0].astype(BF16),
        "wc": _channel_dft_matrix(),
        "wql": wi[:, s0:s1].astype(BF16),
        "wkvl": wi[:, s1:s2].astype(BF16),
        "wkr": jnp.pad(wi[:, s2:s3], ((0, 0), (0, LANES - QK_ROPE_DIM))).astype(BF16),
        "wgate": wi[:, s3:].astype(BF16),
        "qg": q_norm_g[l][None, :],
        "kvg": kv_norm_g[l][None, :],
        "wuq": _pad_heads(w_uq[l], QK_DIM, 0, QK_DIM).astype(BF16),
        "wuk": _pad_heads(w_ukv[l], QK_NOPE_DIM + V_HEAD_DIM, 0, QK_NOPE_DIM).astype(BF16),
        "wuvt": _pad_heads(w_ukv[l], QK_NOPE_DIM + V_HEAD_DIM, QK_NOPE_DIM, QK_NOPE_DIM + V_HEAD_DIM).T.astype(BF16),
        "ekr": _rope_key_placement(),
        "wfo": w_fourier_out[l].astype(BF16),
        "wao": wao.reshape(HEAD_PAD, D_MODEL).astype(BF16),
        "wout": w_out[l].astype(BF16),
        "ffn_g": ffn_norm_g[l][None, :],
        "wr_hi": wr_hi,
        "wr_lo": (wr - wr_hi.astype(F32)).astype(BF16),
        "br": jnp.pad(br, (0, ROUTER_LANES - br.shape[0]))[None, :],
        "wg": w_gate[l].astype(BF16),
        "wu": w_up[l].astype(BF16),
        "wd": w_down[l].astype(BF16),
    }


def _trunk(x, layers, final_g, tables):
    bsz, seq, d = x.shape
    t = bsz * seq
    cs, nss, rope = tables
    x2d = x.reshape(t, d)
    for l, lw in enumerate(layers):
        a, b, q, k, gate, vt = _in_proj(x2d, seq, lw, rope)
        shp = lambda z: z.reshape(bsz, seq, z.shape[-1])
        f = _seq_dft(shp(a), shp(b), cs, nss).reshape(t, FOURIER_DIM)
        o = _attention(shp(q), shp(k), vt).reshape(t, HEAD_PAD)
        x1, xn, comb = _merge(x2d, f, o, gate, lw)
        x2d = _moe(x1, xn, comb, lw, final_g, final_norm=(l == len(layers) - 1))
    return x2d.reshape(bsz, seq, d)


def kernel(x_prompt, x_sample, attn_norm_g, w_in, q_norm_g, kv_norm_g, w_uq, w_ukv, w_fourier_out, w_attn_out, w_out, ffn_norm_g, w_grp, b_grp, w_exp, b_exp, w_gate, w_up, w_down, final_norm_g):
    depth = w_in.shape[0]
    layers = [_layer_weights(l, attn_norm_g, w_in, q_norm_g, kv_norm_g, w_uq, w_ukv, w_fourier_out, w_attn_out,
                             w_out, ffn_norm_g, w_grp, b_grp, w_exp, b_exp, w_gate, w_up, w_down)
              for l in range(depth)]
    final_g = final_norm_g[None, :]
    outs = []
    for x in (x_prompt, x_sample):
        outs.append(_trunk(x, layers, final_g, _seq_tables(x.shape[1])))
    return tuple(outs)
```

```python
import functools
import math

import numpy as np
import jax
import jax.numpy as jnp
from jax import lax
from jax.experimental import pallas as pl
from jax.experimental.pallas import tpu as pltpu

D_MODEL = 1024
FOURIER_GROUPS = 4
FOURIER_GROUP_DIM = 128
FOURIER_DIM = FOURIER_GROUPS * FOURIER_GROUP_DIM
N_HEADS = 8
QK_NOPE_DIM = 64
QK_ROPE_DIM = 32
V_HEAD_DIM = 64
Q_LORA_RANK = 384
KV_LORA_RANK = 256
QK_DIM = QK_NOPE_DIM + QK_ROPE_DIM
ROPE_BASE = 10000.0
N_GROUPS = 4
EXPERTS_PER_GROUP = 8
N_EXPERTS = N_GROUPS * EXPERTS_PER_GROUP
D_EXPERT = 256
EPS = 1e-6

LANES = 128
HEAD_PAD = N_HEADS * LANES
ROPE_HALF = QK_ROPE_DIM // 2
ONES_ROW = V_HEAD_DIM
FFT_N2 = 128
ROUTER_LANES = LANES
EXPERT_LANE0 = N_GROUPS
VMEM_LIMIT = 56 * 1024 * 1024

BF16 = jnp.bfloat16
F32 = jnp.float32


def _dot(a, b):
    return jnp.dot(a, b, preferred_element_type=F32)


def _rms(x, g):
    return x * lax.rsqrt(jnp.mean(x * x, axis=-1, keepdims=True) + EPS) * g


def _params(*sem):
    return pltpu.CompilerParams(dimension_semantics=sem, vmem_limit_bytes=VMEM_LIMIT)


def _full(shape):
    return pl.BlockSpec(shape, lambda *_: (0,) * len(shape))


def _in_proj_kernel(x_ref, g_ref, wf_ref, wql_ref, wkvl_ref, wkr_ref, wgate_ref, qg_ref, kvg_ref,
                    wuq_ref, wuk_ref, wuvt_ref, ekr_ref, cos_ref, sin1_ref, sin2_ref,
                    u_ref, q_ref, k_ref, gate_ref, vt_ref, *, q_scale):
    xb = _rms(x_ref[...], g_ref[...]).astype(BF16)
    u_ref[...] = _dot(xb, wf_ref[...]).astype(BF16)
    qn = _rms(_dot(xb, wql_ref[...]), qg_ref[...]).astype(BF16)
    kvn = _rms(_dot(xb, wkvl_ref[...]), kvg_ref[...]).astype(BF16)
    ukr = _dot(xb, wkr_ref[...]).astype(BF16)
    q = _dot(qn, wuq_ref[...])
    k = _dot(kvn, wuk_ref[...]) + _dot(ukr, ekr_ref[...])
    vt = lax.dot_general(wuvt_ref[...], kvn, (((1,), (1,)), ((), ())), preferred_element_type=F32)
    head_row = lax.broadcasted_iota(jnp.int32, vt.shape, 0) % LANES
    vt_ref[0, 0] = jnp.where(head_row == ONES_ROW, 1.0, vt).astype(BF16)
    cos, sin1, sin2 = cos_ref[...], sin1_ref[...], sin2_ref[...]
    q_up = pltpu.roll(q, HEAD_PAD - ROPE_HALF, 1)
    q_dn = pltpu.roll(q, ROPE_HALF, 1)
    k_up = pltpu.roll(k, HEAD_PAD - ROPE_HALF, 1)
    k_dn = pltpu.roll(k, ROPE_HALF, 1)
    for h in range(N_HEADS):
        sl = slice(h * LANES, (h + 1) * LANES)
        q_ref[:, sl] = ((q[:, sl] * cos + q_up[:, sl] * sin1 + q_dn[:, sl] * sin2) * q_scale).astype(BF16)
        k_ref[:, sl] = (k[:, sl] * cos + k_up[:, sl] * sin1 + k_dn[:, sl] * sin2).astype(BF16)
    gate_ref[...] = jax.nn.sigmoid(_dot(xb, wgate_ref[...])).astype(BF16)


def _in_proj(x2d, seq, lw, rope):
    t = x2d.shape[0]
    tm = _key_chunk(seq)
    n_seq_tiles = seq // tm
    row = lambda w: pl.BlockSpec((tm, w), lambda i: (i, 0))
    pos = pl.BlockSpec((tm, LANES), lambda i: (i % n_seq_tiles, 0))
    weights = [lw["attn_g"], lw["wf"], lw["wql"], lw["wkvl"], lw["wkr"], lw["wgate"], lw["qg"], lw["kvg"],
               lw["wuq"], lw["wuk"], lw["wuvt"], lw["ekr"]]
    out_w = [FOURIER_DIM, HEAD_PAD, HEAD_PAD, 2 * D_MODEL]
    vt_spec = pl.BlockSpec((1, 1, HEAD_PAD, tm), lambda i: (i // n_seq_tiles, i % n_seq_tiles, 0, 0))
    vt_shape = jax.ShapeDtypeStruct((t // seq, n_seq_tiles, HEAD_PAD, tm), BF16)
    return pl.pallas_call(
        functools.partial(_in_proj_kernel, q_scale=(QK_DIM ** -0.5) * math.log2(math.e)),
        grid=(t // tm,),
        in_specs=[row(D_MODEL)] + [_full(w.shape) for w in weights] + [pos, pos, pos],
        out_specs=[row(w) for w in out_w] + [vt_spec],
        out_shape=[jax.ShapeDtypeStruct((t, w), BF16) for w in out_w] + [vt_shape],
        compiler_params=_params("parallel"),
        name="in_proj",
    )(x2d, *weights, *rope)


def _dft_a_kernel(u_ref, f1_ref, twc_ref, tws_ref, y_ref, *, tj, n1):
    f1 = f1_ref[...]
    for j in range(tj):
        y = _dot(f1, u_ref[0, j])
        yr, yi = y[:n1], y[n1:]
        c = jnp.concatenate([twc_ref[j]] * FOURIER_GROUPS, axis=1)
        s = jnp.concatenate([tws_ref[j]] * FOURIER_GROUPS, axis=1)
        y_ref[0, j, 0] = (yr * c + yi * s).astype(BF16)
        y_ref[0, j, 1] = (yi * c - yr * s).astype(BF16)


def _dft_a(u4, f1, twc, tws):
    bsz, n2, n1, _ = u4.shape
    tj = 16
    return pl.pallas_call(
        functools.partial(_dft_a_kernel, tj=tj, n1=n1),
        grid=(bsz, n2 // tj),
        in_specs=[pl.BlockSpec((1, tj, n1, FOURIER_DIM), lambda b, j: (b, j, 0, 0)), _full(f1.shape),
                  pl.BlockSpec((tj, n1, LANES), lambda b, j: (j, 0, 0)),
                  pl.BlockSpec((tj, n1, LANES), lambda b, j: (j, 0, 0))],
        out_specs=pl.BlockSpec((1, tj, 2, n1, FOURIER_DIM), lambda b, j: (b, j, 0, 0, 0)),
        out_shape=jax.ShapeDtypeStruct((bsz, n2, 2, n1, FOURIER_DIM), BF16),
        compiler_params=_params("parallel", "parallel"),
        name="dft_a",
    )(u4, f1, twc, tws)


def _dft_b_kernel(y_ref, m2_ref, wcs_ref, f_ref, x_scr, *, tk1, scale):
    m2 = m2_ref[...]
    for j in range(tk1):
        x = _dot(m2, y_ref[0, j].reshape(2 * FFT_N2, FOURIER_DIM))
        rows = slice(j * FFT_N2, (j + 1) * FFT_N2)
        x_scr[rows, :FOURIER_DIM] = x[:FFT_N2].astype(BF16)
        x_scr[rows, FOURIER_DIM:] = x[FFT_N2:].astype(BF16)
    f = _dot(x_scr[...], wcs_ref[...]) * scale
    f_ref[0] = f.reshape(tk1, FFT_N2, FOURIER_DIM).astype(BF16)


def _dft_b(y5, m2, wcs, seq):
    bsz, n1 = y5.shape[:2]
    tk1 = 8
    return pl.pallas_call(
        functools.partial(_dft_b_kernel, tk1=tk1, scale=(seq * FOURIER_GROUP_DIM) ** -0.5),
        grid=(bsz, n1 // tk1),
        in_specs=[pl.BlockSpec((1, tk1, 2, FFT_N2, FOURIER_DIM), lambda b, j: (b, j, 0, 0, 0)),
                  _full(m2.shape), _full(wcs.shape)],
        out_specs=pl.BlockSpec((1, tk1, FFT_N2, FOURIER_DIM), lambda b, j: (b, j, 0, 0)),
        out_shape=jax.ShapeDtypeStruct((bsz, n1, FFT_N2, FOURIER_DIM), BF16),
        scratch_shapes=[pltpu.VMEM((tk1 * FFT_N2, 2 * FOURIER_DIM), BF16)],
        compiler_params=_params("parallel", "parallel"),
        name="dft_b",
    )(y5, m2, wcs)


def _fourier_mix(u2d, bsz, seq, dft):
    n1 = seq // FFT_N2
    u4 = u2d.reshape(bsz, n1, FFT_N2, FOURIER_DIM).transpose(0, 2, 1, 3)
    y5 = _dft_a(u4, dft["f1"], dft["twc"], dft["tws"])
    f4 = _dft_b(y5.transpose(0, 3, 2, 1, 4), dft["m2"], dft["wcs"], seq)
    return f4.transpose(0, 2, 1, 3).reshape(bsz * seq, FOURIER_DIM)


def _key_chunk(seq):
    return min(512, seq // 2)


def _attn_kernel(q_ref, k_ref, vt_ref, o_ref, m_ref, acc_ref, s_ref, *, tk, nk):
    q = q_ref[0]
    m_ref[...] = jnp.full_like(m_ref, -jnp.inf)
    acc_ref[...] = jnp.zeros_like(acc_ref)

    def scores(kc):
        ks = pl.multiple_of(kc * tk, tk)
        return lax.dot_general(k_ref[0, pl.ds(ks, tk), :], q, (((1,), (1,)), ((), ())),
                               preferred_element_type=F32)

    def softmax_pv(kc, st):
        m_prev = m_ref[...]
        m_new = jnp.maximum(m_prev, jnp.max(st, axis=0, keepdims=True))
        alpha = jnp.exp2(m_prev - m_new)
        p = jnp.exp2(st - m_new).astype(BF16)
        acc_ref[...] = alpha * acc_ref[...] + _dot(vt_ref[0, kc], p)
        m_ref[...] = m_new

    s_ref[0] = scores(0)

    def body(j, carry):
        c = 2 * j
        s_ref[1] = scores(c + 1)
        softmax_pv(c, s_ref[0])
        s_ref[0] = scores(jnp.minimum(c + 2, nk - 1))
        softmax_pv(c + 1, s_ref[1])
        return carry

    lax.fori_loop(0, nk // 2, body, 0)
    acc = acc_ref[...]
    o_ref[0] = (acc / acc[ONES_ROW:ONES_ROW + 1, :]).T.astype(BF16)


def _attention(q3, k3, vt4):
    bsz, seq, _ = q3.shape
    tq = min(1024, seq)
    tk = _key_chunk(seq)
    nk = seq // tk
    qo = pl.BlockSpec((1, tq, LANES), lambda b, h, i: (b, i, h))
    kk = pl.BlockSpec((1, seq, LANES), lambda b, h, i: (b, 0, h))
    vt = pl.BlockSpec((1, nk, LANES, tk), lambda b, h, i: (b, 0, h, 0))
    return pl.pallas_call(
        functools.partial(_attn_kernel, tk=tk, nk=nk),
        grid=(bsz, N_HEADS, seq // tq),
        in_specs=[qo, kk, vt],
        out_specs=qo,
        out_shape=jax.ShapeDtypeStruct((bsz, seq, HEAD_PAD), BF16),
        scratch_shapes=[pltpu.VMEM((1, tq), F32), pltpu.VMEM((LANES, tq), F32), pltpu.VMEM((2, tk, tq), F32)],
        compiler_params=_params("parallel", "parallel", "arbitrary"),
        name="attn",
    )(q3, k3, vt4)


def _merge_kernel(x_ref, f_ref, o_ref, gate_ref, wfo_ref, wao_ref, wout_ref, fg_ref, wrh_ref, wrl_ref, br_ref,
                  x1_ref, xn_ref, comb_ref):
    y_f = _dot(f_ref[...], wfo_ref[...])
    y_a = _dot(o_ref[...], wao_ref[...])
    gate = gate_ref[...].astype(F32)
    merged = gate[:, :D_MODEL] * y_f + gate[:, D_MODEL:] * y_a
    x1 = x_ref[...] + _dot(merged.astype(BF16), wout_ref[...])
    x1_ref[...] = x1
    xn = _rms(x1, fg_ref[...])
    xh = xn.astype(BF16)
    xn_ref[...] = xh
    xl = (xn - xh.astype(F32)).astype(BF16)
    logit = _dot(xh, wrh_ref[...]) + _dot(xl, wrh_ref[...]) + _dot(xh, wrl_ref[...]) + br_ref[...]
    lane = lax.broadcasted_iota(jnp.int32, logit.shape, 1)
    neg = jnp.float32(-jnp.inf)

    def top(mask):
        val = jnp.max(jnp.where(mask, logit, neg), axis=1, keepdims=True)
        idx = jnp.min(jnp.where(mask & (logit == val), lane, ROUTER_LANES), axis=1, keepdims=True)
        return val, idx

    is_grp = lane < N_GROUPS
    g_max, g_idx = top(is_grp)
    g_sum = jnp.sum(jnp.where(is_grp, jnp.exp(logit - g_max), 0.0), axis=1, keepdims=True)
    grp_p = 1.0 / g_sum
    lo = EXPERT_LANE0 + g_idx * EXPERTS_PER_GROUP
    in_grp = (lane >= lo) & (lane < lo + EXPERTS_PER_GROUP)
    l1, i1 = top(in_grp)
    l2, i2 = top(in_grp & (lane != i1))
    e2 = jnp.exp(l2 - l1)
    w1 = 1.0 / (1.0 + e2)
    w2 = e2 / (1.0 + e2)
    comb_ref[...] = jnp.where(lane == i1, grp_p * w1, jnp.where(lane == i2, grp_p * w2, 0.0))


def _merge(x2d, f2d, o2d, gate2d, lw):
    t = x2d.shape[0]
    tm = min(512, t)
    row = lambda w: pl.BlockSpec((tm, w), lambda i: (i, 0))
    weights = [lw["wfo"], lw["wao"], lw["wout"], lw["ffn_g"], lw["wr_hi"], lw["wr_lo"], lw["br"]]
    return pl.pallas_call(
        _merge_kernel,
        grid=(t // tm,),
        in_specs=[row(D_MODEL), row(FOURIER_DIM), row(HEAD_PAD), row(2 * D_MODEL)] + [_full(w.shape) for w in weights],
        out_specs=[row(D_MODEL), row(D_MODEL), row(ROUTER_LANES)],
        out_shape=[jax.ShapeDtypeStruct((t, D_MODEL), F32), jax.ShapeDtypeStruct((t, D_MODEL), BF16),
                   jax.ShapeDtypeStruct((t, ROUTER_LANES), F32)],
        compiler_params=_params("parallel"),
        name="merge",
    )(x2d, f2d, o2d, gate2d, *weights)


def _moe_kernel(x1_ref, xn_ref, comb_ref, wg_ref, wu_ref, wd_ref, fin_ref, out_ref, acc_ref, *, ec, final_norm):
    e = pl.program_id(1)

    @pl.when(e == 0)
    def _():
        acc_ref[...] = jnp.zeros_like(acc_ref)

    xb = xn_ref[...]
    comb = comb_ref[...]
    lane = lax.broadcasted_iota(jnp.int32, comb.shape, 1)
    y = None
    for j in range(ec):
        c = jnp.sum(jnp.where(lane == EXPERT_LANE0 + e * ec + j, comb, 0.0), axis=1, keepdims=True)
        hg = _dot(xb, wg_ref[j])
        hu = _dot(xb, wu_ref[j])
        hs = (hg * jax.nn.sigmoid(hg) * hu * c).astype(BF16)
        yj = _dot(hs, wd_ref[j])
        y = yj if y is None else y + yj
    acc_ref[...] += y

    @pl.when(e == pl.num_programs(1) - 1)
    def _():
        x2 = x1_ref[...] + acc_ref[...]
        out_ref[...] = _rms(x2, fin_ref[...]) if final_norm else x2


def _moe(x1, xn, comb, lw, final_g, final_norm):
    t = x1.shape[0]
    tm = min(1024, t)
    ec = 4
    row = lambda w: pl.BlockSpec((tm, w), lambda i, e: (i, 0))
    wspec = lambda a, b: pl.BlockSpec((ec, a, b), lambda i, e: (e, 0, 0))
    return pl.pallas_call(
        functools.partial(_moe_kernel, ec=ec, final_norm=final_norm),
        grid=(t // tm, N_EXPERTS // ec),
        in_specs=[row(D_MODEL), row(D_MODEL), row(ROUTER_LANES), wspec(D_MODEL, D_EXPERT), wspec(D_MODEL, D_EXPERT),
                  wspec(D_EXPERT, D_MODEL), pl.BlockSpec((1, D_MODEL), lambda i, e: (0, 0))],
        out_specs=row(D_MODEL),
        out_shape=jax.ShapeDtypeStruct((t, D_MODEL), F32),
        scratch_shapes=[pltpu.VMEM((tm, D_MODEL), F32)],
        compiler_params=_params("parallel", "arbitrary"),
        name="moe",
    )(x1, xn, comb, lw["wg"], lw["wu"], lw["wd"], final_g)


def _dft_angles(n):
    return 2.0 * np.pi * ((np.arange(n)[:, None] * np.arange(n)[None, :]) % n) / n


def _dft_tables(seq):
    n1, n2, n = seq // FFT_N2, FFT_N2, FOURIER_GROUP_DIM
    a1, a2, ac = _dft_angles(n1), _dft_angles(n2), _dft_angles(n)
    f1 = np.concatenate([np.cos(a1), -np.sin(a1)], axis=0)
    m2 = np.block([[np.cos(a2), np.sin(a2)], [-np.sin(a2), np.cos(a2)]])
    wcs = np.zeros((2 * FOURIER_DIM, FOURIER_DIM), np.float64)
    for g in range(FOURIER_GROUPS):
        sl = slice(g * n, (g + 1) * n)
        wcs[sl, sl] = np.cos(ac)
        wcs[FOURIER_DIM + g * n:FOURIER_DIM + (g + 1) * n, sl] = np.sin(ac)
    idx = (jnp.arange(n2, dtype=jnp.int32)[:, None] * jnp.arange(n1, dtype=jnp.int32)[None, :]) % seq
    ang = jnp.broadcast_to((idx.astype(F32) * (2.0 * math.pi / seq))[:, :, None], (n2, n1, LANES))
    return {"f1": jnp.asarray(f1, BF16), "m2": jnp.asarray(m2, BF16), "wcs": jnp.asarray(wcs, BF16),
            "twc": jnp.cos(ang), "tws": jnp.sin(ang)}


def _rope_key_placement():
    ekr = np.zeros((LANES, HEAD_PAD), np.float32)
    for h in range(N_HEADS):
        for r in range(QK_ROPE_DIM):
            ekr[r, h * LANES + QK_NOPE_DIM + r] = 1.0
    return jnp.asarray(ekr, BF16)


def _rope_tables(seq):
    inv = 1.0 / (ROPE_BASE ** (jnp.arange(0, QK_ROPE_DIM, 2, dtype=F32) / QK_ROPE_DIM))
    rang = jnp.arange(seq, dtype=F32)[:, None] * inv[None, :]
    c, s = jnp.cos(rang), jnp.sin(rang)
    z = lambda w: jnp.zeros((seq, w), F32)
    tail = LANES - QK_DIM
    cos = jnp.concatenate([jnp.ones((seq, QK_NOPE_DIM), F32), c, c, z(tail)], axis=1)
    sin1 = jnp.concatenate([z(QK_NOPE_DIM), -s, z(ROPE_HALF), z(tail)], axis=1)
    sin2 = jnp.concatenate([z(QK_NOPE_DIM), z(ROPE_HALF), s, z(tail)], axis=1)
    return cos, sin1, sin2


def _pad_heads(w, per_head, lo, hi):
    r = w.shape[0]
    w3 = w.reshape(r, N_HEADS, per_head)[:, :, lo:hi]
    return jnp.pad(w3, ((0, 0), (0, 0), (0, LANES - (hi - lo)))).reshape(r, HEAD_PAD)


def _layer_weights(l, attn_norm_g, w_in, q_norm_g, kv_norm_g, w_uq, w_ukv, w_fourier_out, w_attn_out, w_out,
                   ffn_norm_g, w_grp, b_grp, w_exp, b_exp, w_gate, w_up, w_down):
    s0, s1, s2, s3 = (FOURIER_DIM, FOURIER_DIM + Q_LORA_RANK, FOURIER_DIM + Q_LORA_RANK + KV_LORA_RANK,
                      FOURIER_DIM + Q_LORA_RANK + KV_LORA_RANK + QK_ROPE_DIM)
    wi = w_in[l]
    wr = jnp.concatenate([w_grp[l], w_exp[l]], axis=1)
    wr = jnp.pad(wr, ((0, 0), (0, ROUTER_LANES - wr.shape[1])))
    wr_hi = wr.astype(BF16)
    br = jnp.concatenate([b_grp[l], b_exp[l]])
    wao = jnp.pad(w_attn_out[l].reshape(N_HEADS, V_HEAD_DIM, D_MODEL), ((0, 0), (0, LANES - V_HEAD_DIM), (0, 0)))
    return {
        "attn_g": attn_norm_g[l][None, :],
        "wf": wi[:, :s0].astype(BF16),
        "wql": wi[:, s0:s1].astype(BF16),
        "wkvl": wi[:, s1:s2].astype(BF16),
        "wkr": jnp.pad(wi[:, s2:s3], ((0, 0), (0, LANES - QK_ROPE_DIM))).astype(BF16),
        "wgate": wi[:, s3:].astype(BF16),
        "qg": q_norm_g[l][None, :],
        "kvg": kv_norm_g[l][None, :],
        "wuq": _pad_heads(w_uq[l], QK_DIM, 0, QK_DIM).astype(BF16),
        "wuk": _pad_heads(w_ukv[l], QK_NOPE_DIM + V_HEAD_DIM, 0, QK_NOPE_DIM).astype(BF16),
        "wuvt": _pad_heads(w_ukv[l], QK_NOPE_DIM + V_HEAD_DIM, QK_NOPE_DIM, QK_NOPE_DIM + V_HEAD_DIM).T.astype(BF16),
        "ekr": _rope_key_placement(),
        "wfo": w_fourier_out[l].astype(BF16),
        "wao": wao.reshape(HEAD_PAD, D_MODEL).astype(BF16),
        "wout": w_out[l].astype(BF16),
        "ffn_g": ffn_norm_g[l][None, :],
        "wr_hi": wr_hi,
        "wr_lo": (wr - wr_hi.astype(F32)).astype(BF16),
        "br": jnp.pad(br, (0, ROUTER_LANES - br.shape[0]))[None, :],
        "wg": w_gate[l].astype(BF16),
        "wu": w_up[l].astype(BF16),
        "wd": w_down[l].astype(BF16),
    }


def _trunk(x, layers, final_g, tables):
    bsz, seq, d = x.shape
    t = bsz * seq
    dft, rope = tables
    x2d = x.reshape(t, d)
    for l, lw in enumerate(layers):
        u, q, k, gate, vt = _in_proj(x2d, seq, lw, rope)
        shp = lambda z: z.reshape(bsz, seq, z.shape[-1])
        f = _fourier_mix(u, bsz, seq, dft)
        o = _attention(shp(q), shp(k), vt).reshape(t, HEAD_PAD)
        x1, xn, comb = _merge(x2d, f, o, gate, lw)
        x2d = _moe(x1, xn, comb, lw, final_g, final_norm=(l == len(layers) - 1))
    return x2d.reshape(bsz, seq, d)


def kernel(x_prompt, x_sample, attn_norm_g, w_in, q_norm_g, kv_norm_g, w_uq, w_ukv, w_fourier_out, w_attn_out, w_out, ffn_norm_g, w_grp, b_grp, w_exp, b_exp, w_gate, w_up, w_down, final_norm_g):
    depth = w_in.shape[0]
    layers = [_layer_weights(l, attn_norm_g, w_in, q_norm_g, kv_norm_g, w_uq, w_ukv, w_fourier_out, w_attn_out,
                             w_out, ffn_norm_g, w_grp, b_grp, w_exp, b_exp, w_gate, w_up, w_down)
              for l in range(depth)]
    final_g = final_norm_g[None, :]
    outs = []
    for x in (x_prompt, x_sample):
        seq = x.shape[1]
        outs.append(_trunk(x, layers, final_g, (_dft_tables(seq), _rope_tables(seq))))
    return tuple(outs)
```

```python
import functools
import math

import numpy as np
import jax
import jax.numpy as jnp
from jax import lax
from jax.experimental import pallas as pl
from jax.experimental.pallas import tpu as pltpu

D_MODEL = 1024
FOURIER_GROUPS = 4
FOURIER_GROUP_DIM = 128
FOURIER_DIM = FOURIER_GROUPS * FOURIER_GROUP_DIM
N_HEADS = 8
QK_NOPE_DIM = 64
QK_ROPE_DIM = 32
V_HEAD_DIM = 64
Q_LORA_RANK = 384
KV_LORA_RANK = 256
QK_DIM = QK_NOPE_DIM + QK_ROPE_DIM
ROPE_BASE = 10000.0
N_GROUPS = 4
EXPERTS_PER_GROUP = 8
N_EXPERTS = N_GROUPS * EXPERTS_PER_GROUP
D_EXPERT = 256
EPS = 1e-6

LANES = 128
HEAD_PAD = N_HEADS * LANES
ROPE_HALF = QK_ROPE_DIM // 2
ONES_ROW = V_HEAD_DIM
FFT_N2 = 128
ROUTER_LANES = LANES
EXPERT_LANE0 = N_GROUPS
GROUP_LANE = 0
SUBLANES = 8
MERGE_TILE = 512
MOE_TILE = 1024
MOE_CHUNK = 256
VMEM_LIMIT = 56 * 1024 * 1024

BF16 = jnp.bfloat16
F32 = jnp.float32


def _dot(a, b):
    return jnp.dot(a, b, preferred_element_type=F32)


def _rms(x, g):
    return x * lax.rsqrt(jnp.mean(x * x, axis=-1, keepdims=True) + EPS) * g


def _params(*sem):
    return pltpu.CompilerParams(dimension_semantics=sem, vmem_limit_bytes=VMEM_LIMIT)


def _full(shape):
    return pl.BlockSpec(shape, lambda *_: (0,) * len(shape))


def _in_proj_kernel(x_ref, g_ref, wf_ref, wql_ref, wkvl_ref, wkr_ref, wgate_ref, qg_ref, kvg_ref,
                    wuq_ref, wuk_ref, wuvt_ref, ekr_ref, cos_ref, sin1_ref, sin2_ref,
                    u_ref, q_ref, k_ref, gate_ref, vt_ref, *, q_scale):
    xb = _rms(x_ref[...], g_ref[...]).astype(BF16)
    u_ref[...] = _dot(xb, wf_ref[...]).astype(BF16)
    qn = _rms(_dot(xb, wql_ref[...]), qg_ref[...]).astype(BF16)
    kvn = _rms(_dot(xb, wkvl_ref[...]), kvg_ref[...]).astype(BF16)
    ukr = _dot(xb, wkr_ref[...]).astype(BF16)
    q = _dot(qn, wuq_ref[...])
    k = _dot(kvn, wuk_ref[...]) + _dot(ukr, ekr_ref[...])
    vt = lax.dot_general(wuvt_ref[...], kvn, (((1,), (1,)), ((), ())), preferred_element_type=F32)
    head_row = lax.broadcasted_iota(jnp.int32, vt.shape, 0) % LANES
    vt_ref[0, 0] = jnp.where(head_row == ONES_ROW, 1.0, vt).astype(BF16)
    cos, sin1, sin2 = cos_ref[...], sin1_ref[...], sin2_ref[...]
    q_up = pltpu.roll(q, HEAD_PAD - ROPE_HALF, 1)
    q_dn = pltpu.roll(q, ROPE_HALF, 1)
    k_up = pltpu.roll(k, HEAD_PAD - ROPE_HALF, 1)
    k_dn = pltpu.roll(k, ROPE_HALF, 1)
    for h in range(N_HEADS):
        sl = slice(h * LANES, (h + 1) * LANES)
        q_ref[:, sl] = ((q[:, sl] * cos + q_up[:, sl] * sin1 + q_dn[:, sl] * sin2) * q_scale).astype(BF16)
        k_ref[:, sl] = (k[:, sl] * cos + k_up[:, sl] * sin1 + k_dn[:, sl] * sin2).astype(BF16)
    gate_ref[...] = jax.nn.sigmoid(_dot(xb, wgate_ref[...])).astype(BF16)


def _in_proj(x2d, seq, lw, rope):
    t = x2d.shape[0]
    tm = _key_chunk(seq)
    n_seq_tiles = seq // tm
    row = lambda w: pl.BlockSpec((tm, w), lambda i: (i, 0))
    pos = pl.BlockSpec((tm, LANES), lambda i: (i % n_seq_tiles, 0))
    weights = [lw["attn_g"], lw["wf"], lw["wql"], lw["wkvl"], lw["wkr"], lw["wgate"], lw["qg"], lw["kvg"],
               lw["wuq"], lw["wuk"], lw["wuvt"], lw["ekr"]]
    out_w = [FOURIER_DIM, HEAD_PAD, HEAD_PAD, 2 * D_MODEL]
    vt_spec = pl.BlockSpec((1, 1, HEAD_PAD, tm), lambda i: (i // n_seq_tiles, i % n_seq_tiles, 0, 0))
    vt_shape = jax.ShapeDtypeStruct((t // seq, n_seq_tiles, HEAD_PAD, tm), BF16)
    return pl.pallas_call(
        functools.partial(_in_proj_kernel, q_scale=(QK_DIM ** -0.5) * math.log2(math.e)),
        grid=(t // tm,),
        in_specs=[row(D_MODEL)] + [_full(w.shape) for w in weights] + [pos, pos, pos],
        out_specs=[row(w) for w in out_w] + [vt_spec],
        out_shape=[jax.ShapeDtypeStruct((t, w), BF16) for w in out_w] + [vt_shape],
        compiler_params=_params("parallel"),
        name="in_proj",
    )(x2d, *weights, *rope)


def _dft_a_kernel(u_ref, f1_ref, twc_ref, tws_ref, y_ref, *, tj, n1):
    f1 = f1_ref[...]
    for j in range(tj):
        y = _dot(f1, u_ref[0, j])
        yr, yi = y[:n1], y[n1:]
        c = jnp.concatenate([twc_ref[j]] * FOURIER_GROUPS, axis=1)
        s = jnp.concatenate([tws_ref[j]] * FOURIER_GROUPS, axis=1)
        y_ref[0, j, 0] = (yr * c + yi * s).astype(BF16)
        y_ref[0, j, 1] = (yi * c - yr * s).astype(BF16)


def _dft_a(u4, f1, twc, tws):
    bsz, n2, n1, _ = u4.shape
    tj = 16
    return pl.pallas_call(
        functools.partial(_dft_a_kernel, tj=tj, n1=n1),
        grid=(bsz, n2 // tj),
        in_specs=[pl.BlockSpec((1, tj, n1, FOURIER_DIM), lambda b, j: (b, j, 0, 0)), _full(f1.shape),
                  pl.BlockSpec((tj, n1, LANES), lambda b, j: (j, 0, 0)),
                  pl.BlockSpec((tj, n1, LANES), lambda b, j: (j, 0, 0))],
        out_specs=pl.BlockSpec((1, tj, 2, n1, FOURIER_DIM), lambda b, j: (b, j, 0, 0, 0)),
        out_shape=jax.ShapeDtypeStruct((bsz, n2, 2, n1, FOURIER_DIM), BF16),
        compiler_params=_params("parallel", "parallel"),
        name="dft_a",
    )(u4, f1, twc, tws)


def _dft_b_kernel(y_ref, m2_ref, wcs_ref, f_ref, x_scr, *, tk1, scale):
    m2 = m2_ref[...]
    for j in range(tk1):
        x = _dot(m2, y_ref[0, j].reshape(2 * FFT_N2, FOURIER_DIM))
        rows = slice(j * FFT_N2, (j + 1) * FFT_N2)
        x_scr[rows, :FOURIER_DIM] = x[:FFT_N2].astype(BF16)
        x_scr[rows, FOURIER_DIM:] = x[FFT_N2:].astype(BF16)
    f = _dot(x_scr[...], wcs_ref[...]) * scale
    f_ref[0] = f.reshape(tk1, FFT_N2, FOURIER_DIM).astype(BF16)


def _dft_b(y5, m2, wcs, seq):
    bsz, n1 = y5.shape[:2]
    tk1 = 8
    return pl.pallas_call(
        functools.partial(_dft_b_kernel, tk1=tk1, scale=(seq * FOURIER_GROUP_DIM) ** -0.5),
        grid=(bsz, n1 // tk1),
        in_specs=[pl.BlockSpec((1, tk1, 2, FFT_N2, FOURIER_DIM), lambda b, j: (b, j, 0, 0, 0)),
                  _full(m2.shape), _full(wcs.shape)],
        out_specs=pl.BlockSpec((1, tk1, FFT_N2, FOURIER_DIM), lambda b, j: (b, j, 0, 0)),
        out_shape=jax.ShapeDtypeStruct((bsz, n1, FFT_N2, FOURIER_DIM), BF16),
        scratch_shapes=[pltpu.VMEM((tk1 * FFT_N2, 2 * FOURIER_DIM), BF16)],
        compiler_params=_params("parallel", "parallel"),
        name="dft_b",
    )(y5, m2, wcs)


def _fourier_mix(u2d, bsz, seq, dft):
    n1 = seq // FFT_N2
    u4 = u2d.reshape(bsz, n1, FFT_N2, FOURIER_DIM).transpose(0, 2, 1, 3)
    y5 = _dft_a(u4, dft["f1"], dft["twc"], dft["tws"])
    f4 = _dft_b(y5.transpose(0, 3, 2, 1, 4), dft["m2"], dft["wcs"], seq)
    return f4.transpose(0, 2, 1, 3).reshape(bsz * seq, FOURIER_DIM)


def _key_chunk(seq):
    return min(512, seq // 2)


def _attn_kernel(q_ref, k_ref, vt_ref, o_ref, m_ref, acc_ref, s_ref, *, tk, nk):
    q = q_ref[0]
    m_ref[...] = jnp.full_like(m_ref, -jnp.inf)
    acc_ref[...] = jnp.zeros_like(acc_ref)

    def scores(kc):
        ks = pl.multiple_of(kc * tk, tk)
        return lax.dot_general(k_ref[0, pl.ds(ks, tk), :], q, (((1,), (1,)), ((), ())),
                               preferred_element_type=F32)

    def softmax_pv(kc, st):
        m_prev = m_ref[...]
        m_new = jnp.maximum(m_prev, jnp.max(st, axis=0, keepdims=True))
        alpha = jnp.exp2(m_prev - m_new)
        p = jnp.exp2(st - m_new).astype(BF16)
        acc_ref[...] = alpha * acc_ref[...] + _dot(vt_ref[0, kc], p)
        m_ref[...] = m_new

    s_ref[0] = scores(0)

    def body(j, carry):
        c = 2 * j
        s_ref[1] = scores(c + 1)
        softmax_pv(c, s_ref[0])
        s_ref[0] = scores(jnp.minimum(c + 2, nk - 1))
        softmax_pv(c + 1, s_ref[1])
        return carry

    lax.fori_loop(0, nk // 2, body, 0)
    acc = acc_ref[...]
    o_ref[0] = (acc / acc[ONES_ROW:ONES_ROW + 1, :]).T.astype(BF16)


def _attention(q3, k3, vt4):
    bsz, seq, _ = q3.shape
    tq = min(1024, seq)
    tk = _key_chunk(seq)
    nk = seq // tk
    qo = pl.BlockSpec((1, tq, LANES), lambda b, h, i: (b, i, h))
    kk = pl.BlockSpec((1, seq, LANES), lambda b, h, i: (b, 0, h))
    vt = pl.BlockSpec((1, nk, LANES, tk), lambda b, h, i: (b, 0, h, 0))
    return pl.pallas_call(
        functools.partial(_attn_kernel, tk=tk, nk=nk),
        grid=(bsz, N_HEADS, seq // tq),
        in_specs=[qo, kk, vt],
        out_specs=qo,
        out_shape=jax.ShapeDtypeStruct((bsz, seq, HEAD_PAD), BF16),
        scratch_shapes=[pltpu.VMEM((1, tq), F32), pltpu.VMEM((LANES, tq), F32), pltpu.VMEM((2, tk, tq), F32)],
        compiler_params=_params("parallel", "parallel", "arbitrary"),
        name="attn",
    )(q3, k3, vt4)


def _merge_kernel(x_ref, f_ref, o_ref, gate_ref, wfo_ref, wao_ref, wout_ref, fg_ref, wrh_ref, wrl_ref, br_ref,
                  x1_ref, xn_ref, comb_ref, cnt_ref):
    y_f = _dot(f_ref[...], wfo_ref[...])
    y_a = _dot(o_ref[...], wao_ref[...])
    gate = gate_ref[...].astype(F32)
    merged = gate[:, :D_MODEL] * y_f + gate[:, D_MODEL:] * y_a
    x1 = x_ref[...] + _dot(merged.astype(BF16), wout_ref[...])
    x1_ref[...] = x1
    xn = _rms(x1, fg_ref[...])
    xh = xn.astype(BF16)
    xn_ref[...] = xh
    xl = (xn - xh.astype(F32)).astype(BF16)
    logit = _dot(xh, wrh_ref[...]) + _dot(xl, wrh_ref[...]) + _dot(xh, wrl_ref[...]) + br_ref[...]
    lane = lax.broadcasted_iota(jnp.int32, logit.shape, 1)
    neg = jnp.float32(-jnp.inf)

    def top(mask):
        val = jnp.max(jnp.where(mask, logit, neg), axis=1, keepdims=True)
        idx = jnp.min(jnp.where(mask & (logit == val), lane, ROUTER_LANES), axis=1, keepdims=True)
        return val, idx

    is_grp = lane < N_GROUPS
    g_max, g_idx = top(is_grp)
    g_sum = jnp.sum(jnp.where(is_grp, jnp.exp(logit - g_max), 0.0), axis=1, keepdims=True)
    grp_p = 1.0 / g_sum
    lo = EXPERT_LANE0 + g_idx * EXPERTS_PER_GROUP
    in_grp = (lane >= lo) & (lane < lo + EXPERTS_PER_GROUP)
    l1, i1 = top(in_grp)
    l2, i2 = top(in_grp & (lane != i1))
    e2 = jnp.exp(l2 - l1)
    w1 = 1.0 / (1.0 + e2)
    w2 = e2 / (1.0 + e2)
    comb = jnp.where(lane == i1, grp_p * w1, jnp.where(lane == i2, grp_p * w2, 0.0))
    comb_ref[...] = jnp.where(lane == GROUP_LANE, g_idx.astype(F32), comb)
    cnt = jnp.sum(jnp.where(lane == g_idx, 1.0, 0.0), axis=0, keepdims=True)
    cnt_ref[0] = jnp.broadcast_to(cnt, cnt_ref.shape[1:])


def _merge(x2d, f2d, o2d, gate2d, lw):
    t = x2d.shape[0]
    tm = min(MERGE_TILE, t)
    row = lambda w: pl.BlockSpec((tm, w), lambda i: (i, 0))
    weights = [lw["wfo"], lw["wao"], lw["wout"], lw["ffn_g"], lw["wr_hi"], lw["wr_lo"], lw["br"]]
    return pl.pallas_call(
        _merge_kernel,
        grid=(t // tm,),
        in_specs=[row(D_MODEL), row(FOURIER_DIM), row(HEAD_PAD), row(2 * D_MODEL)] + [_full(w.shape) for w in weights],
        out_specs=[row(D_MODEL), row(D_MODEL), row(ROUTER_LANES),
                   pl.BlockSpec((1, SUBLANES, ROUTER_LANES), lambda i: (i, 0, 0))],
        out_shape=[jax.ShapeDtypeStruct((t, D_MODEL), F32), jax.ShapeDtypeStruct((t, D_MODEL), BF16),
                   jax.ShapeDtypeStruct((t, ROUTER_LANES), F32),
                   jax.ShapeDtypeStruct((t // tm, SUBLANES, ROUTER_LANES), F32)],
        compiler_params=_params("parallel"),
        name="merge",
    )(x2d, f2d, o2d, gate2d, *weights)


def _moe_kernel(cnt_ref, x1_ref, xn_ref, comb_ref, wg_ref, wu_ref, wd_ref, fin_ref, out_ref,
                acc_ref, dcol_ref, drow_ref, chi_ref, clo_ref, *, final_norm):
    i, g = pl.program_id(0), pl.program_id(1)
    tm = xn_ref.shape[0]

    def n_chunks(gg):
        return (cnt_ref[i * N_GROUPS + gg] + (MOE_CHUNK - 1)) // MOE_CHUNK

    @pl.when(g == 0)
    def _():
        comb = comb_ref[...]
        lane = lax.broadcasted_iota(jnp.int32, comb.shape, 1)
        grp = comb[:, GROUP_LANE:GROUP_LANE + 1].astype(jnp.int32)
        onehot = jnp.where((lane == grp) & (lane < N_GROUPS), 1.0, 0.0)
        earlier = lax.broadcasted_iota(jnp.int32, (tm, tm), 0) > lax.broadcasted_iota(jnp.int32, (tm, tm), 1)
        before = _dot(jnp.where(earlier, 1.0, 0.0).astype(BF16), onehot.astype(BF16))
        dest = jnp.sum(onehot * before, axis=1, keepdims=True)
        first_chunk = jnp.int32(0)
        for gg in range(N_GROUPS):
            dest = dest + jnp.where(grp == gg, (first_chunk * MOE_CHUNK).astype(F32), 0.0)
            first_chunk = first_chunk + n_chunks(gg)
        dcol = jnp.broadcast_to(dest, comb.shape)
        dcol_ref[...] = dcol
        drow_ref[...] = dcol.T[:SUBLANES, :]
        w = jnp.where(lane == GROUP_LANE, 0.0, comb)
        chi = w.astype(BF16)
        chi_ref[...] = chi
        clo_ref[...] = (w - chi.astype(F32)).astype(BF16)
        acc_ref[...] = jnp.zeros_like(acc_ref)

    first_chunk = jnp.int32(0)
    for gg in range(N_GROUPS):
        first_chunk = first_chunk + jnp.where(gg < g, n_chunks(gg), 0)

    def chunk(c, carry):
        row0 = ((first_chunk + c) * MOE_CHUNK).astype(F32)
        slot = lax.broadcasted_iota(jnp.int32, (MOE_CHUNK, tm), 0).astype(F32) + row0
        gather = jnp.where(drow_ref[0:1, :] == slot, 1.0, 0.0).astype(BF16)
        xs = _dot(gather, xn_ref[...]).astype(BF16)
        cs = _dot(gather, chi_ref[...]) + _dot(gather, clo_ref[...])
        lane = lax.broadcasted_iota(jnp.int32, cs.shape, 1)
        y = None
        for e in range(EXPERTS_PER_GROUP):
            c_e = jnp.sum(jnp.where(lane == EXPERT_LANE0 + g * EXPERTS_PER_GROUP + e, cs, 0.0), axis=1, keepdims=True)
            hg = _dot(xs, wg_ref[e])
            hu = _dot(xs, wu_ref[e])
            hs = (hg * jax.nn.sigmoid(hg) * hu * c_e).astype(BF16)
            ye = _dot(hs, wd_ref[e])
            y = ye if y is None else y + ye
        slot_t = lax.broadcasted_iota(jnp.int32, (tm, MOE_CHUNK), 1).astype(F32) + row0
        scatter = jnp.where(dcol_ref[:, 0:1] == slot_t, 1.0, 0.0).astype(BF16)
        acc_ref[...] += _dot(scatter, y.astype(BF16))
        return carry

    lax.fori_loop(0, n_chunks(g), chunk, 0)

    @pl.when(g == pl.num_programs(1) - 1)
    def _():
        x2 = x1_ref[...] + acc_ref[...]
        out_ref[...] = _rms(x2, fin_ref[...]) if final_norm else x2


def _moe(x1, xn, comb, cnt, lw, final_g, final_norm):
    t = x1.shape[0]
    tm = min(MOE_TILE, t)
    per = tm // min(MERGE_TILE, t)
    counts = cnt[:, 0, :N_GROUPS].reshape(t // tm, per, N_GROUPS).sum(axis=1).astype(jnp.int32).reshape(-1)
    row = lambda w: pl.BlockSpec((tm, w), lambda i, g, c: (i, 0))
    wspec = lambda a, b: pl.BlockSpec((EXPERTS_PER_GROUP, a, b), lambda i, g, c: (g, 0, 0))
    grid_spec = pltpu.PrefetchScalarGridSpec(
        num_scalar_prefetch=1,
        grid=(t // tm, N_GROUPS),
        in_specs=[row(D_MODEL), row(D_MODEL), row(ROUTER_LANES), wspec(D_MODEL, D_EXPERT), wspec(D_MODEL, D_EXPERT),
                  wspec(D_EXPERT, D_MODEL), pl.BlockSpec((1, D_MODEL), lambda i, g, c: (0, 0))],
        out_specs=row(D_MODEL),
        scratch_shapes=[pltpu.VMEM((tm, D_MODEL), F32), pltpu.VMEM((tm, ROUTER_LANES), F32),
                        pltpu.VMEM((SUBLANES, tm), F32), pltpu.VMEM((tm, ROUTER_LANES), BF16),
                        pltpu.VMEM((tm, ROUTER_LANES), BF16)],
    )
    return pl.pallas_call(
        functools.partial(_moe_kernel, final_norm=final_norm),
        grid_spec=grid_spec,
        out_shape=jax.ShapeDtypeStruct((t, D_MODEL), F32),
        compiler_params=_params("parallel", "arbitrary"),
        name="moe",
    )(counts, x1, xn, comb, lw["wg"], lw["wu"], lw["wd"], final_g)


def _dft_angles(n):
    return 2.0 * np.pi * ((np.arange(n)[:, None] * np.arange(n)[None, :]) % n) / n


def _dft_tables(seq):
    n1, n2, n = seq // FFT_N2, FFT_N2, FOURIER_GROUP_DIM
    a1, a2, ac = _dft_angles(n1), _dft_angles(n2), _dft_angles(n)
    f1 = np.concatenate([np.cos(a1), -np.sin(a1)], axis=0)
    m2 = np.block([[np.cos(a2), np.sin(a2)], [-np.sin(a2), np.cos(a2)]])
    wcs = np.zeros((2 * FOURIER_DIM, FOURIER_DIM), np.float64)
    for g in range(FOURIER_GROUPS):
        sl = slice(g * n, (g + 1) * n)
        wcs[sl, sl] = np.cos(ac)
        wcs[FOURIER_DIM + g * n:FOURIER_DIM + (g + 1) * n, sl] = np.sin(ac)
    idx = (jnp.arange(n2, dtype=jnp.int32)[:, None] * jnp.arange(n1, dtype=jnp.int32)[None, :]) % seq
    ang = jnp.broadcast_to((idx.astype(F32) * (2.0 * math.pi / seq))[:, :, None], (n2, n1, LANES))
    return {"f1": jnp.asarray(f1, BF16), "m2": jnp.asarray(m2, BF16), "wcs": jnp.asarray(wcs, BF16),
            "twc": jnp.cos(ang), "tws": jnp.sin(ang)}


def _rope_key_placement():
    ekr = np.zeros((LANES, HEAD_PAD), np.float32)
    for h in range(N_HEADS):
        for r in range(QK_ROPE_DIM):
            ekr[r, h * LANES + QK_NOPE_DIM + r] = 1.0
    return jnp.asarray(ekr, BF16)


def _rope_tables(seq):
    inv = 1.0 / (ROPE_BASE ** (jnp.arange(0, QK_ROPE_DIM, 2, dtype=F32) / QK_ROPE_DIM))
    rang = jnp.arange(seq, dtype=F32)[:, None] * inv[None, :]
    c, s = jnp.cos(rang), jnp.sin(rang)
    z = lambda w: jnp.zeros((seq, w), F32)
    tail = LANES - QK_DIM
    cos = jnp.concatenate([jnp.ones((seq, QK_NOPE_DIM), F32), c, c, z(tail)], axis=1)
    sin1 = jnp.concatenate([z(QK_NOPE_DIM), -s, z(ROPE_HALF), z(tail)], axis=1)
    sin2 = jnp.concatenate([z(QK_NOPE_DIM), z(ROPE_HALF), s, z(tail)], axis=1)
    return cos, sin1, sin2


def _pad_heads(w, per_head, lo, hi):
    r = w.shape[0]
    w3 = w.reshape(r, N_HEADS, per_head)[:, :, lo:hi]
    return jnp.pad(w3, ((0, 0), (0, 0), (0, LANES - (hi - lo)))).reshape(r, HEAD_PAD)


def _layer_weights(l, attn_norm_g, w_in, q_norm_g, kv_norm_g, w_uq, w_ukv, w_fourier_out, w_attn_out, w_out,
                   ffn_norm_g, w_grp, b_grp, w_exp, b_exp, w_gate, w_up, w_down):
    s0, s1, s2, s3 = (FOURIER_DIM, FOURIER_DIM + Q_LORA_RANK, FOURIER_DIM + Q_LORA_RANK + KV_LORA_RANK,
                      FOURIER_DIM + Q_LORA_RANK + KV_LORA_RANK + QK_ROPE_DIM)
    wi = w_in[l]
    wr = jnp.concatenate([w_grp[l], w_exp[l]], axis=1)
    wr = jnp.pad(wr, ((0, 0), (0, ROUTER_LANES - wr.shape[1])))
    wr_hi = wr.astype(BF16)
    br = jnp.concatenate([b_grp[l], b_exp[l]])
    wao = jnp.pad(w_attn_out[l].reshape(N_HEADS, V_HEAD_DIM, D_MODEL), ((0, 0), (0, LANES - V_HEAD_DIM), (0, 0)))
    return {
        "attn_g": attn_norm_g[l][None, :],
        "wf": wi[:, :s0].astype(BF16),
        "wql": wi[:, s0:s1].astype(BF16),
        "wkvl": wi[:, s1:s2].astype(BF16),
        "wkr": jnp.pad(wi[:, s2:s3], ((0, 0), (0, LANES - QK_ROPE_DIM))).astype(BF16),
        "wgate": wi[:, s3:].astype(BF16),
        "qg": q_norm_g[l][None, :],
        "kvg": kv_norm_g[l][None, :],
        "wuq": _pad_heads(w_uq[l], QK_DIM, 0, QK_DIM).astype(BF16),
        "wuk": _pad_heads(w_ukv[l], QK_NOPE_DIM + V_HEAD_DIM, 0, QK_NOPE_DIM).astype(BF16),
        "wuvt": _pad_heads(w_ukv[l], QK_NOPE_DIM + V_HEAD_DIM, QK_NOPE_DIM, QK_NOPE_DIM + V_HEAD_DIM).T.astype(BF16),
        "ekr": _rope_key_placement(),
        "wfo": w_fourier_out[l].astype(BF16),
        "wao": wao.reshape(HEAD_PAD, D_MODEL).astype(BF16),
        "wout": w_out[l].astype(BF16),
        "ffn_g": ffn_norm_g[l][None, :],
        "wr_hi": wr_hi,
        "wr_lo": (wr - wr_hi.astype(F32)).astype(BF16),
        "br": jnp.pad(br, (0, ROUTER_LANES - br.shape[0]))[None, :],
        "wg": w_gate[l].astype(BF16),
        "wu": w_up[l].astype(BF16),
        "wd": w_down[l].astype(BF16),
    }


def _trunk(x, layers, final_g, tables):
    bsz, seq, d = x.shape
    t = bsz * seq
    dft, rope = tables
    x2d = x.reshape(t, d)
    for l, lw in enumerate(layers):
        u, q, k, gate, vt = _in_proj(x2d, seq, lw, rope)
        shp = lambda z: z.reshape(bsz, seq, z.shape[-1])
        f = _fourier_mix(u, bsz, seq, dft)
        o = _attention(shp(q), shp(k), vt).reshape(t, HEAD_PAD)
        x1, xn, comb, cnt = _merge(x2d, f, o, gate, lw)
        x2d = _moe(x1, xn, comb, cnt, lw, final_g, final_norm=(l == len(layers) - 1))
    return x2d.reshape(bsz, seq, d)


def kernel(x_prompt, x_sample, attn_norm_g, w_in, q_norm_g, kv_norm_g, w_uq, w_ukv, w_fourier_out, w_attn_out, w_out, ffn_norm_g, w_grp, b_grp, w_exp, b_exp, w_gate, w_up, w_down, final_norm_g):
    depth = w_in.shape[0]
    layers = [_layer_weights(l, attn_norm_g, w_in, q_norm_g, kv_norm_g, w_uq, w_ukv, w_fourier_out, w_attn_out,
                             w_out, ffn_norm_g, w_grp, b_grp, w_exp, b_exp, w_gate, w_up, w_down)
              for l in range(depth)]
    final_g = final_norm_g[None, :]
    outs = []
    for x in (x_prompt, x_sample):
        seq = x.shape[1]
        outs.append(_trunk(x, layers, final_g, (_dft_tables(seq), _rope_tables(seq))))
    return tuple(outs)
```

```python
import functools
import math

import numpy as np
import jax
import jax.numpy as jnp
from jax import lax
from jax.experimental import pallas as pl
from jax.experimental.pallas import tpu as pltpu

D_MODEL = 1024
FOURIER_GROUPS = 4
FOURIER_GROUP_DIM = 128
FOURIER_DIM = FOURIER_GROUPS * FOURIER_GROUP_DIM
N_HEADS = 8
QK_NOPE_DIM = 64
QK_ROPE_DIM = 32
V_HEAD_DIM = 64
Q_LORA_RANK = 384
KV_LORA_RANK = 256
QK_DIM = QK_NOPE_DIM + QK_ROPE_DIM
ROPE_BASE = 10000.0
N_GROUPS = 4
EXPERTS_PER_GROUP = 8
N_EXPERTS = N_GROUPS * EXPERTS_PER_GROUP
D_EXPERT = 256
EPS = 1e-6

LANES = 128
HEAD_PAD = N_HEADS * LANES
ROPE_HALF = QK_ROPE_DIM // 2
ONES_ROW = V_HEAD_DIM
V_ROWS = 80
FFT_N2 = 128
ROUTER_LANES = LANES
EXPERT_LANE0 = N_GROUPS
GROUP_LANE = 0
SUBLANES = 8
MERGE_TILE = 512
MOE_TILE = 1024
MOE_CHUNK = 256
VMEM_LIMIT = 56 * 1024 * 1024

BF16 = jnp.bfloat16
F32 = jnp.float32


def _dot(a, b):
    return jnp.dot(a, b, preferred_element_type=F32)


def _rms(x, g):
    return x * lax.rsqrt(jnp.mean(x * x, axis=-1, keepdims=True) + EPS) * g


def _params(*sem):
    return pltpu.CompilerParams(dimension_semantics=sem, vmem_limit_bytes=VMEM_LIMIT)


def _full(shape):
    return pl.BlockSpec(shape, lambda *_: (0,) * len(shape))


def _in_proj_kernel(x_ref, g_ref, wf_ref, wql_ref, wkvl_ref, wkr_ref, wgate_ref, qg_ref, kvg_ref,
                    wuq_ref, wuk_ref, wuvt_ref, ekr_ref, cos_ref, sin1_ref, sin2_ref,
                    u_ref, q_ref, k_ref, gate_ref, vt_ref, *, q_scale):
    xb = _rms(x_ref[...], g_ref[...]).astype(BF16)
    u_ref[...] = _dot(xb, wf_ref[...]).astype(BF16)
    qn = _rms(_dot(xb, wql_ref[...]), qg_ref[...]).astype(BF16)
    kvn = _rms(_dot(xb, wkvl_ref[...]), kvg_ref[...]).astype(BF16)
    ukr = _dot(xb, wkr_ref[...]).astype(BF16)
    q = _dot(qn, wuq_ref[...])
    k = _dot(kvn, wuk_ref[...]) + _dot(ukr, ekr_ref[...])
    vt = lax.dot_general(wuvt_ref[...], kvn, (((1,), (1,)), ((), ())), preferred_element_type=F32)
    head_row = lax.broadcasted_iota(jnp.int32, vt.shape, 0) % V_ROWS
    vt_ref[0, 0] = jnp.where(head_row == ONES_ROW, 1.0, vt).astype(BF16)
    cos, sin1, sin2 = cos_ref[...], sin1_ref[...], sin2_ref[...]
    q_up = pltpu.roll(q, HEAD_PAD - ROPE_HALF, 1)
    q_dn = pltpu.roll(q, ROPE_HALF, 1)
    k_up = pltpu.roll(k, HEAD_PAD - ROPE_HALF, 1)
    k_dn = pltpu.roll(k, ROPE_HALF, 1)
    for h in range(N_HEADS):
        sl = slice(h * LANES, (h + 1) * LANES)
        q_ref[:, sl] = ((q[:, sl] * cos + q_up[:, sl] * sin1 + q_dn[:, sl] * sin2) * q_scale).astype(BF16)
        k_ref[:, sl] = (k[:, sl] * cos + k_up[:, sl] * sin1 + k_dn[:, sl] * sin2).astype(BF16)
    gate_ref[...] = jax.nn.sigmoid(_dot(xb, wgate_ref[...])).astype(BF16)


def _in_proj(x2d, seq, lw, rope):
    t = x2d.shape[0]
    tm = _key_chunk(seq)
    n_seq_tiles = seq // tm
    row = lambda w: pl.BlockSpec((tm, w), lambda i: (i, 0))
    pos = pl.BlockSpec((tm, LANES), lambda i: (i % n_seq_tiles, 0))
    weights = [lw["attn_g"], lw["wf"], lw["wql"], lw["wkvl"], lw["wkr"], lw["wgate"], lw["qg"], lw["kvg"],
               lw["wuq"], lw["wuk"], lw["wuvt"], lw["ekr"]]
    out_w = [FOURIER_DIM, HEAD_PAD, HEAD_PAD, 2 * D_MODEL]
    vt_spec = pl.BlockSpec((1, 1, N_HEADS * V_ROWS, tm), lambda i: (i // n_seq_tiles, i % n_seq_tiles, 0, 0))
    vt_shape = jax.ShapeDtypeStruct((t // seq, n_seq_tiles, N_HEADS * V_ROWS, tm), BF16)
    return pl.pallas_call(
        functools.partial(_in_proj_kernel, q_scale=(QK_DIM ** -0.5) * math.log2(math.e)),
        grid=(t // tm,),
        in_specs=[row(D_MODEL)] + [_full(w.shape) for w in weights] + [pos, pos, pos],
        out_specs=[row(w) for w in out_w] + [vt_spec],
        out_shape=[jax.ShapeDtypeStruct((t, w), BF16) for w in out_w] + [vt_shape],
        compiler_params=_params("parallel"),
        name="in_proj",
    )(x2d, *weights, *rope)


def _dft_a_kernel(u_ref, f1_ref, twc_ref, tws_ref, y_ref, *, tj, n1):
    f1 = f1_ref[...]
    for j in range(tj):
        y = _dot(f1, u_ref[0, j])
        yr, yi = y[:n1], y[n1:]
        c = jnp.concatenate([twc_ref[j]] * FOURIER_GROUPS, axis=1)
        s = jnp.concatenate([tws_ref[j]] * FOURIER_GROUPS, axis=1)
        y_ref[0, j, 0] = (yr * c + yi * s).astype(BF16)
        y_ref[0, j, 1] = (yi * c - yr * s).astype(BF16)


def _dft_a(u4, f1, twc, tws):
    bsz, n2, n1, _ = u4.shape
    tj = 16
    return pl.pallas_call(
        functools.partial(_dft_a_kernel, tj=tj, n1=n1),
        grid=(bsz, n2 // tj),
        in_specs=[pl.BlockSpec((1, tj, n1, FOURIER_DIM), lambda b, j: (b, j, 0, 0)), _full(f1.shape),
                  pl.BlockSpec((tj, n1, LANES), lambda b, j: (j, 0, 0)),
                  pl.BlockSpec((tj, n1, LANES), lambda b, j: (j, 0, 0))],
        out_specs=pl.BlockSpec((1, tj, 2, n1, FOURIER_DIM), lambda b, j: (b, j, 0, 0, 0)),
        out_shape=jax.ShapeDtypeStruct((bsz, n2, 2, n1, FOURIER_DIM), BF16),
        compiler_params=_params("parallel", "parallel"),
        name="dft_a",
    )(u4, f1, twc, tws)


def _dft_b_kernel(y_ref, m2_ref, wcs_ref, f_ref, x_scr, *, tk1, scale):
    m2 = m2_ref[...]
    for j in range(tk1):
        x = _dot(m2, y_ref[0, j].reshape(2 * FFT_N2, FOURIER_DIM))
        rows = slice(j * FFT_N2, (j + 1) * FFT_N2)
        x_scr[rows, :FOURIER_DIM] = x[:FFT_N2].astype(BF16)
        x_scr[rows, FOURIER_DIM:] = x[FFT_N2:].astype(BF16)
    f = _dot(x_scr[...], wcs_ref[...]) * scale
    f_ref[0] = f.reshape(tk1, FFT_N2, FOURIER_DIM).astype(BF16)


def _dft_b(y5, m2, wcs, seq):
    bsz, n1 = y5.shape[:2]
    tk1 = 8
    return pl.pallas_call(
        functools.partial(_dft_b_kernel, tk1=tk1, scale=(seq * FOURIER_GROUP_DIM) ** -0.5),
        grid=(bsz, n1 // tk1),
        in_specs=[pl.BlockSpec((1, tk1, 2, FFT_N2, FOURIER_DIM), lambda b, j: (b, j, 0, 0, 0)),
                  _full(m2.shape), _full(wcs.shape)],
        out_specs=pl.BlockSpec((1, tk1, FFT_N2, FOURIER_DIM), lambda b, j: (b, j, 0, 0)),
        out_shape=jax.ShapeDtypeStruct((bsz, n1, FFT_N2, FOURIER_DIM), BF16),
        scratch_shapes=[pltpu.VMEM((tk1 * FFT_N2, 2 * FOURIER_DIM), BF16)],
        compiler_params=_params("parallel", "parallel"),
        name="dft_b",
    )(y5, m2, wcs)


def _fourier_mix(u2d, bsz, seq, dft):
    n1 = seq // FFT_N2
    u4 = u2d.reshape(bsz, n1, FFT_N2, FOURIER_DIM).transpose(0, 2, 1, 3)
    y5 = _dft_a(u4, dft["f1"], dft["twc"], dft["tws"])
    f4 = _dft_b(y5.transpose(0, 3, 2, 1, 4), dft["m2"], dft["wcs"], seq)
    return f4.transpose(0, 2, 1, 3).reshape(bsz * seq, FOURIER_DIM)


def _key_chunk(seq):
    return min(512, seq // 2)


def _attn_kernel(q_ref, k_ref, vt_ref, o_ref, m_ref, acc_ref, s_ref, smax_ref, *, tk, nk):
    q = q_ref[0]
    m_ref[...] = jnp.full_like(m_ref, -jnp.inf)
    acc_ref[...] = jnp.zeros_like(acc_ref)

    def scores(kc, slot):
        ks = pl.multiple_of(kc * tk, tk)
        st = lax.dot_general(k_ref[0, pl.ds(ks, tk), :], q, (((1,), (1,)), ((), ())),
                             preferred_element_type=F32)
        s_ref[slot] = st
        smax_ref[slot] = jnp.max(st, axis=0, keepdims=True)

    def softmax_pv(kc, slot):
        m_prev = m_ref[...]
        m_new = jnp.maximum(m_prev, smax_ref[slot])
        alpha = jnp.exp2(m_prev - m_new)
        p = jnp.exp2(s_ref[slot] - m_new).astype(BF16)
        acc_ref[...] = alpha * acc_ref[...] + _dot(vt_ref[0, kc], p)
        m_ref[...] = m_new

    scores(0, 0)

    def body(j, carry):
        c = 2 * j
        scores(c + 1, 1)
        softmax_pv(c, 0)
        scores(jnp.minimum(c + 2, nk - 1), 0)
        softmax_pv(c + 1, 1)
        return carry

    lax.fori_loop(0, nk // 2, body, 0)
    acc = acc_ref[...]
    o = acc / acc[ONES_ROW:ONES_ROW + 1, :]
    o = jnp.concatenate([o, jnp.zeros((LANES - V_ROWS, o.shape[1]), F32)], axis=0)
    o_ref[0] = o.T.astype(BF16)


def _attention(q3, k3, vt4):
    bsz, seq, _ = q3.shape
    tq = min(1024, seq)
    tk = _key_chunk(seq)
    nk = seq // tk
    qo = pl.BlockSpec((1, tq, LANES), lambda b, h, i: (b, i, h))
    kk = pl.BlockSpec((1, seq, LANES), lambda b, h, i: (b, 0, h))
    vt = pl.BlockSpec((1, nk, V_ROWS, tk), lambda b, h, i: (b, 0, h, 0))
    return pl.pallas_call(
        functools.partial(_attn_kernel, tk=tk, nk=nk),
        grid=(bsz, N_HEADS, seq // tq),
        in_specs=[qo, kk, vt],
        out_specs=qo,
        out_shape=jax.ShapeDtypeStruct((bsz, seq, HEAD_PAD), BF16),
        scratch_shapes=[pltpu.VMEM((1, tq), F32), pltpu.VMEM((V_ROWS, tq), F32), pltpu.VMEM((2, tk, tq), F32),
                        pltpu.VMEM((2, 1, tq), F32)],
        compiler_params=_params("parallel", "parallel", "arbitrary"),
        name="attn",
    )(q3, k3, vt4)


def _merge_kernel(x_ref, f_ref, o_ref, gate_ref, wfo_ref, wao_ref, wout_ref, fg_ref, wrh_ref, wrl_ref, br_ref,
                  x1_ref, xn_ref, comb_ref, cnt_ref):
    y_f = _dot(f_ref[...], wfo_ref[...])
    y_a = _dot(o_ref[...], wao_ref[...])
    gate = gate_ref[...].astype(F32)
    merged = gate[:, :D_MODEL] * y_f + gate[:, D_MODEL:] * y_a
    x1 = x_ref[...] + _dot(merged.astype(BF16), wout_ref[...])
    x1_ref[...] = x1
    xn = _rms(x1, fg_ref[...])
    xh = xn.astype(BF16)
    xn_ref[...] = xh
    xl = (xn - xh.astype(F32)).astype(BF16)
    logit = _dot(xh, wrh_ref[...]) + _dot(xl, wrh_ref[...]) + _dot(xh, wrl_ref[...]) + br_ref[...]
    lane = lax.broadcasted_iota(jnp.int32, logit.shape, 1)
    neg = jnp.float32(-jnp.inf)

    def top(mask):
        val = jnp.max(jnp.where(mask, logit, neg), axis=1, keepdims=True)
        idx = jnp.min(jnp.where(mask & (logit == val), lane, ROUTER_LANES), axis=1, keepdims=True)
        return val, idx

    is_grp = lane < N_GROUPS
    g_max, g_idx = top(is_grp)
    g_sum = jnp.sum(jnp.where(is_grp, jnp.exp(logit - g_max), 0.0), axis=1, keepdims=True)
    grp_p = 1.0 / g_sum
    lo = EXPERT_LANE0 + g_idx * EXPERTS_PER_GROUP
    in_grp = (lane >= lo) & (lane < lo + EXPERTS_PER_GROUP)
    l1, i1 = top(in_grp)
    l2, i2 = top(in_grp & (lane != i1))
    e2 = jnp.exp(l2 - l1)
    w1 = 1.0 / (1.0 + e2)
    w2 = e2 / (1.0 + e2)
    comb = jnp.where(lane == i1, grp_p * w1, jnp.where(lane == i2, grp_p * w2, 0.0))
    comb_ref[...] = jnp.where(lane == GROUP_LANE, g_idx.astype(F32), comb)
    cnt = jnp.sum(jnp.where(lane == g_idx, 1.0, 0.0), axis=0, keepdims=True)
    cnt_ref[0] = jnp.broadcast_to(cnt, cnt_ref.shape[1:])


def _merge(x2d, f2d, o2d, gate2d, lw):
    t = x2d.shape[0]
    tm = min(MERGE_TILE, t)
    row = lambda w: pl.BlockSpec((tm, w), lambda i: (i, 0))
    weights = [lw["wfo"], lw["wao"], lw["wout"], lw["ffn_g"], lw["wr_hi"], lw["wr_lo"], lw["br"]]
    return pl.pallas_call(
        _merge_kernel,
        grid=(t // tm,),
        in_specs=[row(D_MODEL), row(FOURIER_DIM), row(HEAD_PAD), row(2 * D_MODEL)] + [_full(w.shape) for w in weights],
        out_specs=[row(D_MODEL), row(D_MODEL), row(ROUTER_LANES),
                   pl.BlockSpec((1, SUBLANES, ROUTER_LANES), lambda i: (i, 0, 0))],
        out_shape=[jax.ShapeDtypeStruct((t, D_MODEL), F32), jax.ShapeDtypeStruct((t, D_MODEL), BF16),
                   jax.ShapeDtypeStruct((t, ROUTER_LANES), F32),
                   jax.ShapeDtypeStruct((t // tm, SUBLANES, ROUTER_LANES), F32)],
        compiler_params=_params("parallel"),
        name="merge",
    )(x2d, f2d, o2d, gate2d, *weights)


def _moe_kernel(cnt_ref, x1_ref, xn_ref, comb_ref, wg_ref, wu_ref, wd_ref, fin_ref, out_ref,
                acc_ref, dcol_ref, drow_ref, chi_ref, clo_ref, *, final_norm):
    i, g = pl.program_id(0), pl.program_id(1)
    tm = xn_ref.shape[0]

    def n_chunks(gg):
        return (cnt_ref[i * N_GROUPS + gg] + (MOE_CHUNK - 1)) // MOE_CHUNK

    @pl.when(g == 0)
    def _():
        comb = comb_ref[...]
        lane = lax.broadcasted_iota(jnp.int32, comb.shape, 1)
        grp = comb[:, GROUP_LANE:GROUP_LANE + 1].astype(jnp.int32)
        onehot = jnp.where((lane == grp) & (lane < N_GROUPS), 1.0, 0.0)
        earlier = lax.broadcasted_iota(jnp.int32, (tm, tm), 0) > lax.broadcasted_iota(jnp.int32, (tm, tm), 1)
        before = _dot(jnp.where(earlier, 1.0, 0.0).astype(BF16), onehot.astype(BF16))
        dest = jnp.sum(onehot * before, axis=1, keepdims=True)
        first_chunk = jnp.int32(0)
        for gg in range(N_GROUPS):
            dest = dest + jnp.where(grp == gg, (first_chunk * MOE_CHUNK).astype(F32), 0.0)
            first_chunk = first_chunk + n_chunks(gg)
        dcol = jnp.broadcast_to(dest, comb.shape)
        dcol_ref[...] = dcol
        drow_ref[...] = dcol.T[:SUBLANES, :]
        w = jnp.where(lane == GROUP_LANE, 0.0, comb)
        chi = w.astype(BF16)
        chi_ref[...] = chi
        clo_ref[...] = (w - chi.astype(F32)).astype(BF16)
        acc_ref[...] = jnp.zeros_like(acc_ref)

    first_chunk = jnp.int32(0)
    for gg in range(N_GROUPS):
        first_chunk = first_chunk + jnp.where(gg < g, n_chunks(gg), 0)

    def chunk(c, carry):
        row0 = ((first_chunk + c) * MOE_CHUNK).astype(F32)
        slot = lax.broadcasted_iota(jnp.int32, (MOE_CHUNK, tm), 0).astype(F32) + row0
        gather = jnp.where(drow_ref[0:1, :] == slot, 1.0, 0.0).astype(BF16)
        xs = _dot(gather, xn_ref[...]).astype(BF16)
        cs = _dot(gather, chi_ref[...]) + _dot(gather, clo_ref[...])
        lane = lax.broadcasted_iota(jnp.int32, cs.shape, 1)
        y = None
        for e in range(EXPERTS_PER_GROUP):
            c_e = jnp.sum(jnp.where(lane == EXPERT_LANE0 + g * EXPERTS_PER_GROUP + e, cs, 0.0), axis=1, keepdims=True)
            hg = _dot(xs, wg_ref[e])
            hu = _dot(xs, wu_ref[e])
            hs = (hg * jax.nn.sigmoid(hg) * hu * c_e).astype(BF16)
            ye = _dot(hs, wd_ref[e])
            y = ye if y is None else y + ye
        slot_t = lax.broadcasted_iota(jnp.int32, (tm, MOE_CHUNK), 1).astype(F32) + row0
        scatter = jnp.where(dcol_ref[:, 0:1] == slot_t, 1.0, 0.0).astype(BF16)
        acc_ref[...] += _dot(scatter, y.astype(BF16))
        return carry

    lax.fori_loop(0, n_chunks(g), chunk, 0)

    @pl.when(g == pl.num_programs(1) - 1)
    def _():
        x2 = x1_ref[...] + acc_ref[...]
        out_ref[...] = _rms(x2, fin_ref[...]) if final_norm else x2


def _moe(x1, xn, comb, cnt, lw, final_g, final_norm):
    t = x1.shape[0]
    tm = min(MOE_TILE, t)
    per = tm // min(MERGE_TILE, t)
    counts = cnt[:, 0, :N_GROUPS].reshape(t // tm, per, N_GROUPS).sum(axis=1).astype(jnp.int32).reshape(-1)
    row = lambda w: pl.BlockSpec((tm, w), lambda i, g, c: (i, 0))
    wspec = lambda a, b: pl.BlockSpec((EXPERTS_PER_GROUP, a, b), lambda i, g, c: (g, 0, 0))
    grid_spec = pltpu.PrefetchScalarGridSpec(
        num_scalar_prefetch=1,
        grid=(t // tm, N_GROUPS),
        in_specs=[row(D_MODEL), row(D_MODEL), row(ROUTER_LANES), wspec(D_MODEL, D_EXPERT), wspec(D_MODEL, D_EXPERT),
                  wspec(D_EXPERT, D_MODEL), pl.BlockSpec((1, D_MODEL), lambda i, g, c: (0, 0))],
        out_specs=row(D_MODEL),
        scratch_shapes=[pltpu.VMEM((tm, D_MODEL), F32), pltpu.VMEM((tm, ROUTER_LANES), F32),
                        pltpu.VMEM((SUBLANES, tm), F32), pltpu.VMEM((tm, ROUTER_LANES), BF16),
                        pltpu.VMEM((tm, ROUTER_LANES), BF16)],
    )
    return pl.pallas_call(
        functools.partial(_moe_kernel, final_norm=final_norm),
        grid_spec=grid_spec,
        out_shape=jax.ShapeDtypeStruct((t, D_MODEL), F32),
        compiler_params=_params("parallel", "arbitrary"),
        name="moe",
    )(counts, x1, xn, comb, lw["wg"], lw["wu"], lw["wd"], final_g)


def _dft_angles(n):
    return 2.0 * np.pi * ((np.arange(n)[:, None] * np.arange(n)[None, :]) % n) / n


def _dft_tables(seq):
    n1, n2, n = seq // FFT_N2, FFT_N2, FOURIER_GROUP_DIM
    a1, a2, ac = _dft_angles(n1), _dft_angles(n2), _dft_angles(n)
    f1 = np.concatenate([np.cos(a1), -np.sin(a1)], axis=0)
    m2 = np.block([[np.cos(a2), np.sin(a2)], [-np.sin(a2), np.cos(a2)]])
    wcs = np.zeros((2 * FOURIER_DIM, FOURIER_DIM), np.float64)
    for g in range(FOURIER_GROUPS):
        sl = slice(g * n, (g + 1) * n)
        wcs[sl, sl] = np.cos(ac)
        wcs[FOURIER_DIM + g * n:FOURIER_DIM + (g + 1) * n, sl] = np.sin(ac)
    idx = (jnp.arange(n2, dtype=jnp.int32)[:, None] * jnp.arange(n1, dtype=jnp.int32)[None, :]) % seq
    ang = jnp.broadcast_to((idx.astype(F32) * (2.0 * math.pi / seq))[:, :, None], (n2, n1, LANES))
    return {"f1": jnp.asarray(f1, BF16), "m2": jnp.asarray(m2, BF16), "wcs": jnp.asarray(wcs, BF16),
            "twc": jnp.cos(ang), "tws": jnp.sin(ang)}


def _rope_key_placement():
    ekr = np.zeros((LANES, HEAD_PAD), np.float32)
    for h in range(N_HEADS):
        for r in range(QK_ROPE_DIM):
            ekr[r, h * LANES + QK_NOPE_DIM + r] = 1.0
    return jnp.asarray(ekr, BF16)


def _rope_tables(seq):
    inv = 1.0 / (ROPE_BASE ** (jnp.arange(0, QK_ROPE_DIM, 2, dtype=F32) / QK_ROPE_DIM))
    rang = jnp.arange(seq, dtype=F32)[:, None] * inv[None, :]
    c, s = jnp.cos(rang), jnp.sin(rang)
    z = lambda w: jnp.zeros((seq, w), F32)
    tail = LANES - QK_DIM
    cos = jnp.concatenate([jnp.ones((seq, QK_NOPE_DIM), F32), c, c, z(tail)], axis=1)
    sin1 = jnp.concatenate([z(QK_NOPE_DIM), -s, z(ROPE_HALF), z(tail)], axis=1)
    sin2 = jnp.concatenate([z(QK_NOPE_DIM), z(ROPE_HALF), s, z(tail)], axis=1)
    return cos, sin1, sin2


def _pad_heads(w, per_head, lo, hi, width=LANES):
    r = w.shape[0]
    w3 = w.reshape(r, N_HEADS, per_head)[:, :, lo:hi]
    return jnp.pad(w3, ((0, 0), (0, 0), (0, width - (hi - lo)))).reshape(r, N_HEADS * width)


def _layer_weights(l, attn_norm_g, w_in, q_norm_g, kv_norm_g, w_uq, w_ukv, w_fourier_out, w_attn_out, w_out,
                   ffn_norm_g, w_grp, b_grp, w_exp, b_exp, w_gate, w_up, w_down):
    s0, s1, s2, s3 = (FOURIER_DIM, FOURIER_DIM + Q_LORA_RANK, FOURIER_DIM + Q_LORA_RANK + KV_LORA_RANK,
                      FOURIER_DIM + Q_LORA_RANK + KV_LORA_RANK + QK_ROPE_DIM)
    wi = w_in[l]
    wr = jnp.concatenate([w_grp[l], w_exp[l]], axis=1)
    wr = jnp.pad(wr, ((0, 0), (0, ROUTER_LANES - wr.shape[1])))
    wr_hi = wr.astype(BF16)
    br = jnp.concatenate([b_grp[l], b_exp[l]])
    wao = jnp.pad(w_attn_out[l].reshape(N_HEADS, V_HEAD_DIM, D_MODEL), ((0, 0), (0, LANES - V_HEAD_DIM), (0, 0)))
    return {
        "attn_g": attn_norm_g[l][None, :],
        "wf": wi[:, :s0].astype(BF16),
        "wql": wi[:, s0:s1].astype(BF16),
        "wkvl": wi[:, s1:s2].astype(BF16),
        "wkr": jnp.pad(wi[:, s2:s3], ((0, 0), (0, LANES - QK_ROPE_DIM))).astype(BF16),
        "wgate": wi[:, s3:].astype(BF16),
        "qg": q_norm_g[l][None, :],
        "kvg": kv_norm_g[l][None, :],
        "wuq": _pad_heads(w_uq[l], QK_DIM, 0, QK_DIM).astype(BF16),
        "wuk": _pad_heads(w_ukv[l], QK_NOPE_DIM + V_HEAD_DIM, 0, QK_NOPE_DIM).astype(BF16),
        "wuvt": _pad_heads(w_ukv[l], QK_NOPE_DIM + V_HEAD_DIM, QK_NOPE_DIM, QK_NOPE_DIM + V_HEAD_DIM,
                           V_ROWS).T.astype(BF16),
        "ekr": _rope_key_placement(),
        "wfo": w_fourier_out[l].astype(BF16),
        "wao": wao.reshape(HEAD_PAD, D_MODEL).astype(BF16),
        "wout": w_out[l].astype(BF16),
        "ffn_g": ffn_norm_g[l][None, :],
        "wr_hi": wr_hi,
        "wr_lo": (wr - wr_hi.astype(F32)).astype(BF16),
        "br": jnp.pad(br, (0, ROUTER_LANES - br.shape[0]))[None, :],
        "wg": w_gate[l].astype(BF16),
        "wu": w_up[l].astype(BF16),
        "wd": w_down[l].astype(BF16),
    }


def _trunk(x, layers, final_g, tables):
    bsz, seq, d = x.shape
    t = bsz * seq
    dft, rope = tables
    x2d = x.reshape(t, d)
    for l, lw in enumerate(layers):
        u, q, k, gate, vt = _in_proj(x2d, seq, lw, rope)
        shp = lambda z: z.reshape(bsz, seq, z.shape[-1])
        f = _fourier_mix(u, bsz, seq, dft)
        o = _attention(shp(q), shp(k), vt).reshape(t, HEAD_PAD)
        x1, xn, comb, cnt = _merge(x2d, f, o, gate, lw)
        x2d = _moe(x1, xn, comb, cnt, lw, final_g, final_norm=(l == len(layers) - 1))
    return x2d.reshape(bsz, seq, d)


def kernel(x_prompt, x_sample, attn_norm_g, w_in, q_norm_g, kv_norm_g, w_uq, w_ukv, w_fourier_out, w_attn_out, w_out, ffn_norm_g, w_grp, b_grp, w_exp, b_exp, w_gate, w_up, w_down, final_norm_g):
    depth = w_in.shape[0]
    layers = [_layer_weights(l, attn_norm_g, w_in, q_norm_g, kv_norm_g, w_uq, w_ukv, w_fourier_out, w_attn_out,
                             w_out, ffn_norm_g, w_grp, b_grp, w_exp, b_exp, w_gate, w_up, w_down)
              for l in range(depth)]
    final_g = final_norm_g[None, :]
    outs = []
    for x in (x_prompt, x_sample):
        seq = x.shape[1]
        outs.append(_trunk(x, layers, final_g, (_dft_tables(seq), _rope_tables(seq))))
    return tuple(outs)
```

```python
import functools
import math

import numpy as np
import jax
import jax.numpy as jnp
from jax import lax
from jax.experimental import pallas as pl
from jax.experimental.pallas import tpu as pltpu

D_MODEL = 1024
FOURIER_GROUPS = 4
FOURIER_GROUP_DIM = 128
FOURIER_DIM = FOURIER_GROUPS * FOURIER_GROUP_DIM
N_HEADS = 8
QK_NOPE_DIM = 64
QK_ROPE_DIM = 32
V_HEAD_DIM = 64
Q_LORA_RANK = 384
KV_LORA_RANK = 256
QK_DIM = QK_NOPE_DIM + QK_ROPE_DIM
ROPE_BASE = 10000.0
N_GROUPS = 4
EXPERTS_PER_GROUP = 8
N_EXPERTS = N_GROUPS * EXPERTS_PER_GROUP
D_EXPERT = 256
EPS = 1e-6

LANES = 128
HEAD_PAD = N_HEADS * LANES
ROPE_HALF = QK_ROPE_DIM // 2
ONES_ROW = V_HEAD_DIM
V_ROWS = 80
FFT_N2 = 128
ROUTER_LANES = LANES
EXPERT_LANE0 = N_GROUPS
GROUP_LANE = 0
SUBLANES = 8
MERGE_TILE = 512
MOE_TILE = 1024
MOE_CHUNK = 256
MOE_TAILS = (256, 320, 512)
VMEM_LIMIT = 56 * 1024 * 1024

BF16 = jnp.bfloat16
F32 = jnp.float32


def _dot(a, b):
    return jnp.dot(a, b, preferred_element_type=F32)


def _rms(x, g):
    return x * lax.rsqrt(jnp.mean(x * x, axis=-1, keepdims=True) + EPS) * g


def _params(*sem):
    return pltpu.CompilerParams(dimension_semantics=sem, vmem_limit_bytes=VMEM_LIMIT)


def _full(shape):
    return pl.BlockSpec(shape, lambda *_: (0,) * len(shape))


def _in_proj_kernel(x_ref, g_ref, wf_ref, wql_ref, wkvl_ref, wkr_ref, wgate_ref, qg_ref, kvg_ref,
                    wuq_ref, wuk_ref, wuvt_ref, ekr_ref, cos_ref, sin1_ref, sin2_ref,
                    u_ref, q_ref, k_ref, gate_ref, vt_ref, *, q_scale):
    xb = _rms(x_ref[...], g_ref[...]).astype(BF16)
    u_ref[...] = _dot(xb, wf_ref[...]).astype(BF16)
    qn = _rms(_dot(xb, wql_ref[...]), qg_ref[...]).astype(BF16)
    kvn = _rms(_dot(xb, wkvl_ref[...]), kvg_ref[...]).astype(BF16)
    ukr = _dot(xb, wkr_ref[...]).astype(BF16)
    q = _dot(qn, wuq_ref[...])
    k = _dot(kvn, wuk_ref[...]) + _dot(ukr, ekr_ref[...])
    vt = lax.dot_general(wuvt_ref[...], kvn, (((1,), (1,)), ((), ())), preferred_element_type=F32)
    head_row = lax.broadcasted_iota(jnp.int32, vt.shape, 0) % V_ROWS
    vt_ref[0, 0] = jnp.where(head_row == ONES_ROW, 1.0, vt).astype(BF16)
    cos, sin1, sin2 = cos_ref[...], sin1_ref[...], sin2_ref[...]
    q_up = pltpu.roll(q, HEAD_PAD - ROPE_HALF, 1)
    q_dn = pltpu.roll(q, ROPE_HALF, 1)
    k_up = pltpu.roll(k, HEAD_PAD - ROPE_HALF, 1)
    k_dn = pltpu.roll(k, ROPE_HALF, 1)
    for h in range(N_HEADS):
        sl = slice(h * LANES, (h + 1) * LANES)
        q_ref[:, sl] = ((q[:, sl] * cos + q_up[:, sl] * sin1 + q_dn[:, sl] * sin2) * q_scale).astype(BF16)
        k_ref[:, sl] = (k[:, sl] * cos + k_up[:, sl] * sin1 + k_dn[:, sl] * sin2).astype(BF16)
    gate_ref[...] = jax.nn.sigmoid(_dot(xb, wgate_ref[...])).astype(BF16)


def _in_proj(x2d, seq, lw, rope):
    t = x2d.shape[0]
    tm = _key_chunk(seq)
    n_seq_tiles = seq // tm
    row = lambda w: pl.BlockSpec((tm, w), lambda i: (i, 0))
    pos = pl.BlockSpec((tm, LANES), lambda i: (i % n_seq_tiles, 0))
    weights = [lw["attn_g"], lw["wf"], lw["wql"], lw["wkvl"], lw["wkr"], lw["wgate"], lw["qg"], lw["kvg"],
               lw["wuq"], lw["wuk"], lw["wuvt"], lw["ekr"]]
    out_w = [FOURIER_DIM, HEAD_PAD, HEAD_PAD, 2 * D_MODEL]
    vt_spec = pl.BlockSpec((1, 1, N_HEADS * V_ROWS, tm), lambda i: (i // n_seq_tiles, i % n_seq_tiles, 0, 0))
    vt_shape = jax.ShapeDtypeStruct((t // seq, n_seq_tiles, N_HEADS * V_ROWS, tm), BF16)
    return pl.pallas_call(
        functools.partial(_in_proj_kernel, q_scale=(QK_DIM ** -0.5) * math.log2(math.e)),
        grid=(t // tm,),
        in_specs=[row(D_MODEL)] + [_full(w.shape) for w in weights] + [pos, pos, pos],
        out_specs=[row(w) for w in out_w] + [vt_spec],
        out_shape=[jax.ShapeDtypeStruct((t, w), BF16) for w in out_w] + [vt_shape],
        compiler_params=_params("parallel"),
        name="in_proj",
    )(x2d, *weights, *rope)


def _dft_a_kernel(u_ref, f1_ref, twc_ref, tws_ref, y_ref, *, tj, n1):
    f1 = f1_ref[...]
    for j in range(tj):
        y = _dot(f1, u_ref[0, j])
        yr, yi = y[:n1], y[n1:]
        c = jnp.concatenate([twc_ref[j]] * FOURIER_GROUPS, axis=1)
        s = jnp.concatenate([tws_ref[j]] * FOURIER_GROUPS, axis=1)
        y_ref[0, j, 0] = (yr * c + yi * s).astype(BF16)
        y_ref[0, j, 1] = (yi * c - yr * s).astype(BF16)


def _dft_a(u4, f1, twc, tws):
    bsz, n2, n1, _ = u4.shape
    tj = 16
    return pl.pallas_call(
        functools.partial(_dft_a_kernel, tj=tj, n1=n1),
        grid=(bsz, n2 // tj),
        in_specs=[pl.BlockSpec((1, tj, n1, FOURIER_DIM), lambda b, j: (b, j, 0, 0)), _full(f1.shape),
                  pl.BlockSpec((tj, n1, LANES), lambda b, j: (j, 0, 0)),
                  pl.BlockSpec((tj, n1, LANES), lambda b, j: (j, 0, 0))],
        out_specs=pl.BlockSpec((1, tj, 2, n1, FOURIER_DIM), lambda b, j: (b, j, 0, 0, 0)),
        out_shape=jax.ShapeDtypeStruct((bsz, n2, 2, n1, FOURIER_DIM), BF16),
        compiler_params=_params("parallel", "parallel"),
        name="dft_a",
    )(u4, f1, twc, tws)


def _dft_b_kernel(y_ref, m2_ref, wcs_ref, f_ref, x_scr, *, tk1, scale):
    m2 = m2_ref[...]
    for j in range(tk1):
        x = _dot(m2, y_ref[0, j].reshape(2 * FFT_N2, FOURIER_DIM))
        rows = slice(j * FFT_N2, (j + 1) * FFT_N2)
        x_scr[rows, :FOURIER_DIM] = x[:FFT_N2].astype(BF16)
        x_scr[rows, FOURIER_DIM:] = x[FFT_N2:].astype(BF16)
    f = _dot(x_scr[...], wcs_ref[...]) * scale
    f_ref[0] = f.reshape(tk1, FFT_N2, FOURIER_DIM).astype(BF16)


def _dft_b(y5, m2, wcs, seq):
    bsz, n1 = y5.shape[:2]
    tk1 = 8
    return pl.pallas_call(
        functools.partial(_dft_b_kernel, tk1=tk1, scale=(seq * FOURIER_GROUP_DIM) ** -0.5),
        grid=(bsz, n1 // tk1),
        in_specs=[pl.BlockSpec((1, tk1, 2, FFT_N2, FOURIER_DIM), lambda b, j: (b, j, 0, 0, 0)),
                  _full(m2.shape), _full(wcs.shape)],
        out_specs=pl.BlockSpec((1, tk1, FFT_N2, FOURIER_DIM), lambda b, j: (b, j, 0, 0)),
        out_shape=jax.ShapeDtypeStruct((bsz, n1, FFT_N2, FOURIER_DIM), BF16),
        scratch_shapes=[pltpu.VMEM((tk1 * FFT_N2, 2 * FOURIER_DIM), BF16)],
        compiler_params=_params("parallel", "parallel"),
        name="dft_b",
    )(y5, m2, wcs)


def _fourier_mix(u2d, bsz, seq, dft):
    n1 = seq // FFT_N2
    u4 = u2d.reshape(bsz, n1, FFT_N2, FOURIER_DIM).transpose(0, 2, 1, 3)
    y5 = _dft_a(u4, dft["f1"], dft["twc"], dft["tws"])
    f4 = _dft_b(y5.transpose(0, 3, 2, 1, 4), dft["m2"], dft["wcs"], seq)
    return f4.transpose(0, 2, 1, 3).reshape(bsz * seq, FOURIER_DIM)


def _key_chunk(seq):
    return min(512, seq // 2)


def _attn_kernel(q_ref, k_ref, vt_ref, o_ref, m_ref, acc_ref, s_ref, smax_ref, *, tk, nk):
    q = q_ref[0]
    m_ref[...] = jnp.full_like(m_ref, -jnp.inf)
    acc_ref[...] = jnp.zeros_like(acc_ref)

    def scores(kc, slot):
        ks = pl.multiple_of(kc * tk, tk)
        st = lax.dot_general(k_ref[0, pl.ds(ks, tk), :], q, (((1,), (1,)), ((), ())),
                             preferred_element_type=F32)
        s_ref[slot] = st
        smax_ref[slot] = jnp.max(st, axis=0, keepdims=True)

    def softmax_pv(kc, slot):
        m_prev = m_ref[...]
        m_new = jnp.maximum(m_prev, smax_ref[slot])
        alpha = jnp.exp2(m_prev - m_new)
        p = jnp.exp2(s_ref[slot] - m_new).astype(BF16)
        acc_ref[...] = alpha * acc_ref[...] + _dot(vt_ref[0, kc], p)
        m_ref[...] = m_new

    scores(0, 0)

    def body(j, carry):
        c = 2 * j
        scores(c + 1, 1)
        softmax_pv(c, 0)
        scores(jnp.minimum(c + 2, nk - 1), 0)
        softmax_pv(c + 1, 1)
        return carry

    lax.fori_loop(0, nk // 2, body, 0)
    acc = acc_ref[...]
    o = acc / acc[ONES_ROW:ONES_ROW + 1, :]
    o = jnp.concatenate([o, jnp.zeros((LANES - V_ROWS, o.shape[1]), F32)], axis=0)
    o_ref[0] = o.T.astype(BF16)


def _attention(q3, k3, vt4):
    bsz, seq, _ = q3.shape
    tq = min(1024, seq)
    tk = _key_chunk(seq)
    nk = seq // tk
    qo = pl.BlockSpec((1, tq, LANES), lambda b, h, i: (b, i, h))
    kk = pl.BlockSpec((1, seq, LANES), lambda b, h, i: (b, 0, h))
    vt = pl.BlockSpec((1, nk, V_ROWS, tk), lambda b, h, i: (b, 0, h, 0))
    return pl.pallas_call(
        functools.partial(_attn_kernel, tk=tk, nk=nk),
        grid=(bsz, N_HEADS, seq // tq),
        in_specs=[qo, kk, vt],
        out_specs=qo,
        out_shape=jax.ShapeDtypeStruct((bsz, seq, HEAD_PAD), BF16),
        scratch_shapes=[pltpu.VMEM((1, tq), F32), pltpu.VMEM((V_ROWS, tq), F32), pltpu.VMEM((2, tk, tq), F32),
                        pltpu.VMEM((2, 1, tq), F32)],
        compiler_params=_params("parallel", "parallel", "arbitrary"),
        name="attn",
    )(q3, k3, vt4)


def _merge_kernel(x_ref, f_ref, o_ref, gate_ref, wfo_ref, wao_ref, wout_ref, fg_ref, wrh_ref, wrl_ref, br_ref,
                  x1_ref, xn_ref, comb_ref, cnt_ref):
    y_f = _dot(f_ref[...], wfo_ref[...])
    y_a = _dot(o_ref[...], wao_ref[...])
    gate = gate_ref[...].astype(F32)
    merged = gate[:, :D_MODEL] * y_f + gate[:, D_MODEL:] * y_a
    x1 = x_ref[...] + _dot(merged.astype(BF16), wout_ref[...])
    x1_ref[...] = x1
    xn = _rms(x1, fg_ref[...])
    xh = xn.astype(BF16)
    xn_ref[...] = xh
    xl = (xn - xh.astype(F32)).astype(BF16)
    logit = _dot(xh, wrh_ref[...]) + _dot(xl, wrh_ref[...]) + _dot(xh, wrl_ref[...]) + br_ref[...]
    lane = lax.broadcasted_iota(jnp.int32, logit.shape, 1)
    neg = jnp.float32(-jnp.inf)

    def top(mask):
        val = jnp.max(jnp.where(mask, logit, neg), axis=1, keepdims=True)
        idx = jnp.min(jnp.where(mask & (logit == val), lane, ROUTER_LANES), axis=1, keepdims=True)
        return val, idx

    is_grp = lane < N_GROUPS
    g_max, g_idx = top(is_grp)
    g_sum = jnp.sum(jnp.where(is_grp, jnp.exp(logit - g_max), 0.0), axis=1, keepdims=True)
    grp_p = 1.0 / g_sum
    lo = EXPERT_LANE0 + g_idx * EXPERTS_PER_GROUP
    in_grp = (lane >= lo) & (lane < lo + EXPERTS_PER_GROUP)
    l1, i1 = top(in_grp)
    l2, i2 = top(in_grp & (lane != i1))
    e2 = jnp.exp(l2 - l1)
    w1 = 1.0 / (1.0 + e2)
    w2 = e2 / (1.0 + e2)
    comb = jnp.where(lane == i1, grp_p * w1, jnp.where(lane == i2, grp_p * w2, 0.0))
    comb_ref[...] = jnp.where(lane == GROUP_LANE, g_idx.astype(F32), comb)
    cnt = jnp.sum(jnp.where(lane == g_idx, 1.0, 0.0), axis=0, keepdims=True)
    cnt_ref[0] = jnp.broadcast_to(cnt, cnt_ref.shape[1:])


def _merge(x2d, f2d, o2d, gate2d, lw):
    t = x2d.shape[0]
    tm = min(MERGE_TILE, t)
    row = lambda w: pl.BlockSpec((tm, w), lambda i: (i, 0))
    weights = [lw["wfo"], lw["wao"], lw["wout"], lw["ffn_g"], lw["wr_hi"], lw["wr_lo"], lw["br"]]
    return pl.pallas_call(
        _merge_kernel,
        grid=(t // tm,),
        in_specs=[row(D_MODEL), row(FOURIER_DIM), row(HEAD_PAD), row(2 * D_MODEL)] + [_full(w.shape) for w in weights],
        out_specs=[row(D_MODEL), row(D_MODEL), row(ROUTER_LANES),
                   pl.BlockSpec((1, SUBLANES, ROUTER_LANES), lambda i: (i, 0, 0))],
        out_shape=[jax.ShapeDtypeStruct((t, D_MODEL), F32), jax.ShapeDtypeStruct((t, D_MODEL), BF16),
                   jax.ShapeDtypeStruct((t, ROUTER_LANES), F32),
                   jax.ShapeDtypeStruct((t // tm, SUBLANES, ROUTER_LANES), F32)],
        compiler_params=_params("parallel"),
        name="merge",
    )(x2d, f2d, o2d, gate2d, *weights)


def _moe_kernel(cnt_ref, x1_ref, xn_ref, comb_ref, wg_ref, wu_ref, wd_ref, fin_ref, out_ref,
                acc_ref, dcol_ref, drow_ref, chi_ref, clo_ref, *, final_norm):
    i, g = pl.program_id(0), pl.program_id(1)
    tm = xn_ref.shape[0]

    def schedule(gg):
        n = cnt_ref[i * N_GROUPS + gg]
        full, rem = n // MOE_CHUNK, n % MOE_CHUNK
        plain = jnp.where(rem > 0, jnp.maximum(full - 1, 0), full)
        need = jnp.where(rem > 0, jnp.where(full > 0, MOE_CHUNK + rem, rem), 0)
        tail = jnp.int32(0)
        for size in reversed(MOE_TAILS):
            tail = jnp.where(need <= size, size, tail)
        return plain, jnp.where(need > 0, tail, 0)

    def rows(gg):
        plain, tail = schedule(gg)
        return plain * MOE_CHUNK + tail

    @pl.when(g == 0)
    def _():
        comb = comb_ref[...]
        lane = lax.broadcasted_iota(jnp.int32, comb.shape, 1)
        grp = comb[:, GROUP_LANE:GROUP_LANE + 1].astype(jnp.int32)
        onehot = jnp.where((lane == grp) & (lane < N_GROUPS), 1.0, 0.0)
        earlier = lax.broadcasted_iota(jnp.int32, (tm, tm), 0) > lax.broadcasted_iota(jnp.int32, (tm, tm), 1)
        before = _dot(jnp.where(earlier, 1.0, 0.0).astype(BF16), onehot.astype(BF16))
        dest = jnp.sum(onehot * before, axis=1, keepdims=True)
        first_row = jnp.int32(0)
        for gg in range(N_GROUPS):
            dest = dest + jnp.where(grp == gg, first_row.astype(F32), 0.0)
            first_row = first_row + rows(gg)
        dcol = jnp.broadcast_to(dest, comb.shape)
        dcol_ref[...] = dcol
        drow_ref[...] = dcol.T[:SUBLANES, :]
        w = jnp.where(lane == GROUP_LANE, 0.0, comb)
        chi = w.astype(BF16)
        chi_ref[...] = chi
        clo_ref[...] = (w - chi.astype(F32)).astype(BF16)
        acc_ref[...] = jnp.zeros_like(acc_ref)

    first_row = jnp.int32(0)
    for gg in range(N_GROUPS):
        first_row = first_row + jnp.where(gg < g, rows(gg), 0)
    plain, tail = schedule(g)

    def chunk(start, size):
        row0 = start.astype(F32)
        slot = lax.broadcasted_iota(jnp.int32, (size, tm), 0).astype(F32) + row0
        gather = jnp.where(drow_ref[0:1, :] == slot, 1.0, 0.0).astype(BF16)
        xs = _dot(gather, xn_ref[...]).astype(BF16)
        cs = _dot(gather, chi_ref[...]) + _dot(gather, clo_ref[...])
        lane = lax.broadcasted_iota(jnp.int32, cs.shape, 1)
        y = None
        for e in range(EXPERTS_PER_GROUP):
            c_e = jnp.sum(jnp.where(lane == EXPERT_LANE0 + g * EXPERTS_PER_GROUP + e, cs, 0.0), axis=1, keepdims=True)
            hg = _dot(xs, wg_ref[e])
            hu = _dot(xs, wu_ref[e])
            hs = (hg * jax.nn.sigmoid(hg) * hu * c_e).astype(BF16)
            ye = _dot(hs, wd_ref[e])
            y = ye if y is None else y + ye
        slot_t = lax.broadcasted_iota(jnp.int32, (tm, size), 1).astype(F32) + row0
        scatter = jnp.where(dcol_ref[:, 0:1] == slot_t, 1.0, 0.0).astype(BF16)
        acc_ref[...] += _dot(scatter, y.astype(BF16))

    def plain_chunk(c, carry):
        chunk(first_row + c * MOE_CHUNK, MOE_CHUNK)
        return carry

    lax.fori_loop(0, plain, plain_chunk, 0)
    for size in MOE_TAILS:
        @pl.when(tail == size)
        def _(size=size):
            chunk(first_row + plain * MOE_CHUNK, size)

    @pl.when(g == pl.num_programs(1) - 1)
    def _():
        x2 = x1_ref[...] + acc_ref[...]
        out_ref[...] = _rms(x2, fin_ref[...]) if final_norm else x2


def _moe(x1, xn, comb, cnt, lw, final_g, final_norm):
    t = x1.shape[0]
    tm = min(MOE_TILE, t)
    per = tm // min(MERGE_TILE, t)
    counts = cnt[:, 0, :N_GROUPS].reshape(t // tm, per, N_GROUPS).sum(axis=1).astype(jnp.int32).reshape(-1)
    row = lambda w: pl.BlockSpec((tm, w), lambda i, g, c: (i, 0))
    wspec = lambda a, b: pl.BlockSpec((EXPERTS_PER_GROUP, a, b), lambda i, g, c: (g, 0, 0))
    grid_spec = pltpu.PrefetchScalarGridSpec(
        num_scalar_prefetch=1,
        grid=(t // tm, N_GROUPS),
        in_specs=[row(D_MODEL), row(D_MODEL), row(ROUTER_LANES), wspec(D_MODEL, D_EXPERT), wspec(D_MODEL, D_EXPERT),
                  wspec(D_EXPERT, D_MODEL), pl.BlockSpec((1, D_MODEL), lambda i, g, c: (0, 0))],
        out_specs=row(D_MODEL),
        scratch_shapes=[pltpu.VMEM((tm, D_MODEL), F32), pltpu.VMEM((tm, ROUTER_LANES), F32),
                        pltpu.VMEM((SUBLANES, tm), F32), pltpu.VMEM((tm, ROUTER_LANES), BF16),
                        pltpu.VMEM((tm, ROUTER_LANES), BF16)],
    )
    return pl.pallas_call(
        functools.partial(_moe_kernel, final_norm=final_norm),
        grid_spec=grid_spec,
        out_shape=jax.ShapeDtypeStruct((t, D_MODEL), F32),
        compiler_params=_params("parallel", "arbitrary"),
        name="moe",
    )(counts, x1, xn, comb, lw["wg"], lw["wu"], lw["wd"], final_g)


def _dft_angles(n):
    return 2.0 * np.pi * ((np.arange(n)[:, None] * np.arange(n)[None, :]) % n) / n


def _dft_tables(seq):
    n1, n2, n = seq // FFT_N2, FFT_N2, FOURIER_GROUP_DIM
    a1, a2, ac = _dft_angles(n1), _dft_angles(n2), _dft_angles(n)
    f1 = np.concatenate([np.cos(a1), -np.sin(a1)], axis=0)
    m2 = np.block([[np.cos(a2), np.sin(a2)], [-np.sin(a2), np.cos(a2)]])
    wcs = np.zeros((2 * FOURIER_DIM, FOURIER_DIM), np.float64)
    for g in range(FOURIER_GROUPS):
        sl = slice(g * n, (g + 1) * n)
        wcs[sl, sl] = np.cos(ac)
        wcs[FOURIER_DIM + g * n:FOURIER_DIM + (g + 1) * n, sl] = np.sin(ac)
    idx = (jnp.arange(n2, dtype=jnp.int32)[:, None] * jnp.arange(n1, dtype=jnp.int32)[None, :]) % seq
    ang = jnp.broadcast_to((idx.astype(F32) * (2.0 * math.pi / seq))[:, :, None], (n2, n1, LANES))
    return {"f1": jnp.asarray(f1, BF16), "m2": jnp.asarray(m2, BF16), "wcs": jnp.asarray(wcs, BF16),
            "twc": jnp.cos(ang), "tws": jnp.sin(ang)}


def _rope_key_placement():
    ekr = np.zeros((LANES, HEAD_PAD), np.float32)
    for h in range(N_HEADS):
        for r in range(QK_ROPE_DIM):
            ekr[r, h * LANES + QK_NOPE_DIM + r] = 1.0
    return jnp.asarray(ekr, BF16)


def _rope_tables(seq):
    inv = 1.0 / (ROPE_BASE ** (jnp.arange(0, QK_ROPE_DIM, 2, dtype=F32) / QK_ROPE_DIM))
    rang = jnp.arange(seq, dtype=F32)[:, None] * inv[None, :]
    c, s = jnp.cos(rang), jnp.sin(rang)
    z = lambda w: jnp.zeros((seq, w), F32)
    tail = LANES - QK_DIM
    cos = jnp.concatenate([jnp.ones((seq, QK_NOPE_DIM), F32), c, c, z(tail)], axis=1)
    sin1 = jnp.concatenate([z(QK_NOPE_DIM), -s, z(ROPE_HALF), z(tail)], axis=1)
    sin2 = jnp.concatenate([z(QK_NOPE_DIM), z(ROPE_HALF), s, z(tail)], axis=1)
    return cos, sin1, sin2


def _pad_heads(w, per_head, lo, hi, width=LANES):
    r = w.shape[0]
    w3 = w.reshape(r, N_HEADS, per_head)[:, :, lo:hi]
    return jnp.pad(w3, ((0, 0), (0, 0), (0, width - (hi - lo)))).reshape(r, N_HEADS * width)


def _layer_weights(l, attn_norm_g, w_in, q_norm_g, kv_norm_g, w_uq, w_ukv, w_fourier_out, w_attn_out, w_out,
                   ffn_norm_g, w_grp, b_grp, w_exp, b_exp, w_gate, w_up, w_down):
    s0, s1, s2, s3 = (FOURIER_DIM, FOURIER_DIM + Q_LORA_RANK, FOURIER_DIM + Q_LORA_RANK + KV_LORA_RANK,
                      FOURIER_DIM + Q_LORA_RANK + KV_LORA_RANK + QK_ROPE_DIM)
    wi = w_in[l]
    wr = jnp.concatenate([w_grp[l], w_exp[l]], axis=1)
    wr = jnp.pad(wr, ((0, 0), (0, ROUTER_LANES - wr.shape[1])))
    wr_hi = wr.astype(BF16)
    br = jnp.concatenate([b_grp[l], b_exp[l]])
    wao = jnp.pad(w_attn_out[l].reshape(N_HEADS, V_HEAD_DIM, D_MODEL), ((0, 0), (0, LANES - V_HEAD_DIM), (0, 0)))
    return {
        "attn_g": attn_norm_g[l][None, :],
        "wf": wi[:, :s0].astype(BF16),
        "wql": wi[:, s0:s1].astype(BF16),
        "wkvl": wi[:, s1:s2].astype(BF16),
        "wkr": jnp.pad(wi[:, s2:s3], ((0, 0), (0, LANES - QK_ROPE_DIM))).astype(BF16),
        "wgate": wi[:, s3:].astype(BF16),
        "qg": q_norm_g[l][None, :],
        "kvg": kv_norm_g[l][None, :],
        "wuq": _pad_heads(w_uq[l], QK_DIM, 0, QK_DIM).astype(BF16),
        "wuk": _pad_heads(w_ukv[l], QK_NOPE_DIM + V_HEAD_DIM, 0, QK_NOPE_DIM).astype(BF16),
        "wuvt": _pad_heads(w_ukv[l], QK_NOPE_DIM + V_HEAD_DIM, QK_NOPE_DIM, QK_NOPE_DIM + V_HEAD_DIM,
                           V_ROWS).T.astype(BF16),
        "ekr": _rope_key_placement(),
        "wfo": w_fourier_out[l].astype(BF16),
        "wao": wao.reshape(HEAD_PAD, D_MODEL).astype(BF16),
        "wout": w_out[l].astype(BF16),
        "ffn_g": ffn_norm_g[l][None, :],
        "wr_hi": wr_hi,
        "wr_lo": (wr - wr_hi.astype(F32)).astype(BF16),
        "br": jnp.pad(br, (0, ROUTER_LANES - br.shape[0]))[None, :],
        "wg": w_gate[l].astype(BF16),
        "wu": w_up[l].astype(BF16),
        "wd": w_down[l].astype(BF16),
    }


def _trunk(x, layers, final_g, tables):
    bsz, seq, d = x.shape
    t = bsz * seq
    dft, rope = tables
    x2d = x.reshape(t, d)
    for l, lw in enumerate(layers):
        u, q, k, gate, vt = _in_proj(x2d, seq, lw, rope)
        shp = lambda z: z.reshape(bsz, seq, z.shape[-1])
        f = _fourier_mix(u, bsz, seq, dft)
        o = _attention(shp(q), shp(k), vt).reshape(t, HEAD_PAD)
        x1, xn, comb, cnt = _merge(x2d, f, o, gate, lw)
        x2d = _moe(x1, xn, comb, cnt, lw, final_g, final_norm=(l == len(layers) - 1))
    return x2d.reshape(bsz, seq, d)


def kernel(x_prompt, x_sample, attn_norm_g, w_in, q_norm_g, kv_norm_g, w_uq, w_ukv, w_fourier_out, w_attn_out, w_out, ffn_norm_g, w_grp, b_grp, w_exp, b_exp, w_gate, w_up, w_down, final_norm_g):
    depth = w_in.shape[0]
    layers = [_layer_weights(l, attn_norm_g, w_in, q_norm_g, kv_norm_g, w_uq, w_ukv, w_fourier_out, w_attn_out,
                             w_out, ffn_norm_g, w_grp, b_grp, w_exp, b_exp, w_gate, w_up, w_down)
              for l in range(depth)]
    final_g = final_norm_g[None, :]
    outs = []
    for x in (x_prompt, x_sample):
        seq = x.shape[1]
        outs.append(_trunk(x, layers, final_g, (_dft_tables(seq), _rope_tables(seq))))
    return tuple(outs)
```

```python
import functools
import math

import numpy as np
import jax
import jax.numpy as jnp
from jax import lax
from jax.experimental import pallas as pl
from jax.experimental.pallas import tpu as pltpu

D_MODEL = 1024
FOURIER_GROUPS = 4
FOURIER_GROUP_DIM = 128
FOURIER_DIM = FOURIER_GROUPS * FOURIER_GROUP_DIM
N_HEADS = 8
QK_NOPE_DIM = 64
QK_ROPE_DIM = 32
V_HEAD_DIM = 64
Q_LORA_RANK = 384
KV_LORA_RANK = 256
QK_DIM = QK_NOPE_DIM + QK_ROPE_DIM
ROPE_BASE = 10000.0
N_GROUPS = 4
EXPERTS_PER_GROUP = 8
N_EXPERTS = N_GROUPS * EXPERTS_PER_GROUP
D_EXPERT = 256
EPS = 1e-6

LANES = 128
HEAD_PAD = N_HEADS * LANES
ROPE_HALF = QK_ROPE_DIM // 2
ONES_ROW = V_HEAD_DIM
V_ROWS = 80
FFT_N2 = 128
ROUTER_LANES = LANES
EXPERT_LANE0 = N_GROUPS
GROUP_LANE = 0
SUBLANES = 8
MERGE_TILE = 512
MOE_TILE = 1024
MOE_CHUNK = 256
MOE_TAILS = (256, 320, 512)
VMEM_LIMIT = 56 * 1024 * 1024

BF16 = jnp.bfloat16
F32 = jnp.float32


def _dot(a, b):
    return jnp.dot(a, b, preferred_element_type=F32)


def _rms(x, g):
    return x * lax.rsqrt(jnp.mean(x * x, axis=-1, keepdims=True) + EPS) * g


def _params(*sem):
    return pltpu.CompilerParams(dimension_semantics=sem, vmem_limit_bytes=VMEM_LIMIT)


def _full(shape):
    return pl.BlockSpec(shape, lambda *_: (0,) * len(shape))


def _in_proj_kernel(x_ref, g_ref, wf_ref, wql_ref, wkvl_ref, wkr_ref, wgate_ref, qg_ref, kvg_ref,
                    wuqt_ref, wuk_ref, wuvt_ref, ekr_ref, cos_ref, sin1_ref, sin2_ref, cost_ref, sin1t_ref, sin2t_ref,
                    u_ref, k_ref, gate_ref, qt_ref, vt_ref, *, q_scale):
    xb = _rms(x_ref[...], g_ref[...]).astype(BF16)
    u_ref[...] = _dot(xb, wf_ref[...]).astype(BF16)
    qn = _rms(_dot(xb, wql_ref[...]), qg_ref[...]).astype(BF16)
    kvn = _rms(_dot(xb, wkvl_ref[...]), kvg_ref[...]).astype(BF16)
    ukr = _dot(xb, wkr_ref[...]).astype(BF16)
    k = _dot(kvn, wuk_ref[...]) + _dot(ukr, ekr_ref[...])
    nt = (((1,), (1,)), ((), ()))
    qt = lax.dot_general(wuqt_ref[...], qn, nt, preferred_element_type=F32)
    vt = lax.dot_general(wuvt_ref[...], kvn, nt, preferred_element_type=F32)
    head_row = lax.broadcasted_iota(jnp.int32, vt.shape, 0) % V_ROWS
    vt_ref[0, 0] = jnp.where(head_row == ONES_ROW, 1.0, vt).astype(BF16)
    cos, sin1, sin2 = cos_ref[...], sin1_ref[...], sin2_ref[...]
    cost, sin1t, sin2t = cost_ref[...], sin1t_ref[...], sin2t_ref[...]
    k_up = pltpu.roll(k, HEAD_PAD - ROPE_HALF, 1)
    k_dn = pltpu.roll(k, ROPE_HALF, 1)
    qt_up = jnp.concatenate([qt[ROPE_HALF:], qt[:ROPE_HALF]], axis=0)
    qt_dn = jnp.concatenate([qt[-ROPE_HALF:], qt[:-ROPE_HALF]], axis=0)
    for h in range(N_HEADS):
        sl = slice(h * LANES, (h + 1) * LANES)
        k_ref[:, sl] = (k[:, sl] * cos + k_up[:, sl] * sin1 + k_dn[:, sl] * sin2).astype(BF16)
        qt_ref[0, sl, :] = ((qt[sl] * cost + qt_up[sl] * sin1t + qt_dn[sl] * sin2t) * q_scale).astype(BF16)
    gate_ref[...] = jax.nn.sigmoid(_dot(xb, wgate_ref[...])).astype(BF16)


def _in_proj(x2d, seq, lw, rope):
    t = x2d.shape[0]
    tm = _key_chunk(seq)
    n_seq_tiles = seq // tm
    row = lambda w: pl.BlockSpec((tm, w), lambda i: (i, 0))
    pos = pl.BlockSpec((tm, LANES), lambda i: (i % n_seq_tiles, 0))
    pos_t = pl.BlockSpec((LANES, tm), lambda i: (0, i % n_seq_tiles))
    weights = [lw["attn_g"], lw["wf"], lw["wql"], lw["wkvl"], lw["wkr"], lw["wgate"], lw["qg"], lw["kvg"],
               lw["wuqt"], lw["wuk"], lw["wuvt"], lw["ekr"]]
    out_w = [FOURIER_DIM, HEAD_PAD, 2 * D_MODEL]
    qt_spec = pl.BlockSpec((1, HEAD_PAD, tm), lambda i: (i // n_seq_tiles, 0, i % n_seq_tiles))
    qt_shape = jax.ShapeDtypeStruct((t // seq, HEAD_PAD, seq), BF16)
    vt_spec = pl.BlockSpec((1, 1, N_HEADS * V_ROWS, tm), lambda i: (i // n_seq_tiles, i % n_seq_tiles, 0, 0))
    vt_shape = jax.ShapeDtypeStruct((t // seq, n_seq_tiles, N_HEADS * V_ROWS, tm), BF16)
    return pl.pallas_call(
        functools.partial(_in_proj_kernel, q_scale=(QK_DIM ** -0.5) * math.log2(math.e)),
        grid=(t // tm,),
        in_specs=[row(D_MODEL)] + [_full(w.shape) for w in weights] + [pos] * 3 + [pos_t] * 3,
        out_specs=[row(w) for w in out_w] + [qt_spec, vt_spec],
        out_shape=[jax.ShapeDtypeStruct((t, w), BF16) for w in out_w] + [qt_shape, vt_shape],
        compiler_params=_params("parallel"),
        name="in_proj",
    )(x2d, *weights, *rope, *[r.T for r in rope])


def _dft_a_kernel(u_ref, f1_ref, twc_ref, tws_ref, y_ref, *, tj, n1):
    f1 = f1_ref[...]
    for j in range(tj):
        y = _dot(f1, u_ref[0, j])
        yr, yi = y[:n1], y[n1:]
        c = jnp.concatenate([twc_ref[j]] * FOURIER_GROUPS, axis=1)
        s = jnp.concatenate([tws_ref[j]] * FOURIER_GROUPS, axis=1)
        y_ref[0, j, 0] = (yr * c + yi * s).astype(BF16)
        y_ref[0, j, 1] = (yi * c - yr * s).astype(BF16)


def _dft_a(u4, f1, twc, tws):
    bsz, n2, n1, _ = u4.shape
    tj = 16
    return pl.pallas_call(
        functools.partial(_dft_a_kernel, tj=tj, n1=n1),
        grid=(bsz, n2 // tj),
        in_specs=[pl.BlockSpec((1, tj, n1, FOURIER_DIM), lambda b, j: (b, j, 0, 0)), _full(f1.shape),
                  pl.BlockSpec((tj, n1, LANES), lambda b, j: (j, 0, 0)),
                  pl.BlockSpec((tj, n1, LANES), lambda b, j: (j, 0, 0))],
        out_specs=pl.BlockSpec((1, tj, 2, n1, FOURIER_DIM), lambda b, j: (b, j, 0, 0, 0)),
        out_shape=jax.ShapeDtypeStruct((bsz, n2, 2, n1, FOURIER_DIM), BF16),
        compiler_params=_params("parallel", "parallel"),
        name="dft_a",
    )(u4, f1, twc, tws)


def _dft_b_kernel(y_ref, m2_ref, wcs_ref, f_ref, x_scr, *, tk1, scale):
    m2 = m2_ref[...]
    for j in range(tk1):
        x = _dot(m2, y_ref[0, j].reshape(2 * FFT_N2, FOURIER_DIM))
        rows = slice(j * FFT_N2, (j + 1) * FFT_N2)
        x_scr[rows, :FOURIER_DIM] = x[:FFT_N2].astype(BF16)
        x_scr[rows, FOURIER_DIM:] = x[FFT_N2:].astype(BF16)
    f = _dot(x_scr[...], wcs_ref[...]) * scale
    f_ref[0] = f.reshape(tk1, FFT_N2, FOURIER_DIM).astype(BF16)


def _dft_b(y5, m2, wcs, seq):
    bsz, n1 = y5.shape[:2]
    tk1 = 8
    return pl.pallas_call(
        functools.partial(_dft_b_kernel, tk1=tk1, scale=(seq * FOURIER_GROUP_DIM) ** -0.5),
        grid=(bsz, n1 // tk1),
        in_specs=[pl.BlockSpec((1, tk1, 2, FFT_N2, FOURIER_DIM), lambda b, j: (b, j, 0, 0, 0)),
                  _full(m2.shape), _full(wcs.shape)],
        out_specs=pl.BlockSpec((1, tk1, FFT_N2, FOURIER_DIM), lambda b, j: (b, j, 0, 0)),
        out_shape=jax.ShapeDtypeStruct((bsz, n1, FFT_N2, FOURIER_DIM), BF16),
        scratch_shapes=[pltpu.VMEM((tk1 * FFT_N2, 2 * FOURIER_DIM), BF16)],
        compiler_params=_params("parallel", "parallel"),
        name="dft_b",
    )(y5, m2, wcs)


def _fourier_mix(u2d, bsz, seq, dft):
    n1 = seq // FFT_N2
    u4 = u2d.reshape(bsz, n1, FFT_N2, FOURIER_DIM).transpose(0, 2, 1, 3)
    y5 = _dft_a(u4, dft["f1"], dft["twc"], dft["tws"])
    f4 = _dft_b(y5.transpose(0, 3, 2, 1, 4), dft["m2"], dft["wcs"], seq)
    return f4.transpose(0, 2, 1, 3).reshape(bsz * seq, FOURIER_DIM)


def _key_chunk(seq):
    return min(512, seq // 2)


def _attn_kernel(q_ref, k_ref, vt_ref, o_ref, m_ref, acc_ref, s_ref, smax_ref, *, tk, nk):
    q = q_ref[0]
    m_ref[...] = jnp.full_like(m_ref, -jnp.inf)
    acc_ref[...] = jnp.zeros_like(acc_ref)

    def scores(kc, slot):
        ks = pl.multiple_of(kc * tk, tk)
        st = _dot(k_ref[0, pl.ds(ks, tk), :], q)
        s_ref[slot] = st
        smax_ref[slot] = jnp.max(st, axis=0, keepdims=True)

    def softmax_pv(kc, slot):
        m_prev = m_ref[...]
        m_new = jnp.maximum(m_prev, smax_ref[slot])
        alpha = jnp.exp2(m_prev - m_new)
        p = jnp.exp2(s_ref[slot] - m_new).astype(BF16)
        acc_ref[...] = alpha * acc_ref[...] + _dot(vt_ref[0, kc], p)
        m_ref[...] = m_new

    scores(0, 0)

    def body(j, carry):
        c = 2 * j
        scores(c + 1, 1)
        softmax_pv(c, 0)
        scores(jnp.minimum(c + 2, nk - 1), 0)
        softmax_pv(c + 1, 1)
        return carry

    lax.fori_loop(0, nk // 2, body, 0)
    acc = acc_ref[...]
    o = acc / acc[ONES_ROW:ONES_ROW + 1, :]
    o = jnp.concatenate([o, jnp.zeros((LANES - V_ROWS, o.shape[1]), F32)], axis=0)
    o_ref[0] = o.T.astype(BF16)


def _attention(qt3, k3, vt4):
    bsz, seq, _ = k3.shape
    tq = min(1024, seq)
    tk = _key_chunk(seq)
    nk = seq // tk
    qt = pl.BlockSpec((1, LANES, tq), lambda b, h, i: (b, h, i))
    qo = pl.BlockSpec((1, tq, LANES), lambda b, h, i: (b, i, h))
    kk = pl.BlockSpec((1, seq, LANES), lambda b, h, i: (b, 0, h))
    vt = pl.BlockSpec((1, nk, V_ROWS, tk), lambda b, h, i: (b, 0, h, 0))
    return pl.pallas_call(
        functools.partial(_attn_kernel, tk=tk, nk=nk),
        grid=(bsz, N_HEADS, seq // tq),
        in_specs=[qt, kk, vt],
        out_specs=qo,
        out_shape=jax.ShapeDtypeStruct((bsz, seq, HEAD_PAD), BF16),
        scratch_shapes=[pltpu.VMEM((1, tq), F32), pltpu.VMEM((V_ROWS, tq), F32), pltpu.VMEM((2, tk, tq), F32),
                        pltpu.VMEM((2, 1, tq), F32)],
        compiler_params=_params("parallel", "parallel", "arbitrary"),
        name="attn",
    )(qt3, k3, vt4)


def _merge_kernel(x_ref, f_ref, o_ref, gate_ref, wfo_ref, wao_ref, wout_ref, fg_ref, wrh_ref, wrl_ref, br_ref,
                  x1_ref, xn_ref, comb_ref, cnt_ref):
    y_f = _dot(f_ref[...], wfo_ref[...])
    y_a = _dot(o_ref[...], wao_ref[...])
    gate = gate_ref[...].astype(F32)
    merged = gate[:, :D_MODEL] * y_f + gate[:, D_MODEL:] * y_a
    x1 = x_ref[...] + _dot(merged.astype(BF16), wout_ref[...])
    x1_ref[...] = x1
    xn = _rms(x1, fg_ref[...])
    xh = xn.astype(BF16)
    xn_ref[...] = xh
    xl = (xn - xh.astype(F32)).astype(BF16)
    logit = _dot(xh, wrh_ref[...]) + _dot(xl, wrh_ref[...]) + _dot(xh, wrl_ref[...]) + br_ref[...]
    lane = lax.broadcasted_iota(jnp.int32, logit.shape, 1)
    neg = jnp.float32(-jnp.inf)

    def top(mask):
        val = jnp.max(jnp.where(mask, logit, neg), axis=1, keepdims=True)
        idx = jnp.min(jnp.where(mask & (logit == val), lane, ROUTER_LANES), axis=1, keepdims=True)
        return val, idx

    is_grp = lane < N_GROUPS
    g_max, g_idx = top(is_grp)
    g_sum = jnp.sum(jnp.where(is_grp, jnp.exp(logit - g_max), 0.0), axis=1, keepdims=True)
    grp_p = 1.0 / g_sum
    lo = EXPERT_LANE0 + g_idx * EXPERTS_PER_GROUP
    in_grp = (lane >= lo) & (lane < lo + EXPERTS_PER_GROUP)
    l1, i1 = top(in_grp)
    l2, i2 = top(in_grp & (lane != i1))
    e2 = jnp.exp(l2 - l1)
    w1 = 1.0 / (1.0 + e2)
    w2 = e2 / (1.0 + e2)
    comb = jnp.where(lane == i1, grp_p * w1, jnp.where(lane == i2, grp_p * w2, 0.0))
    comb_ref[...] = jnp.where(lane == GROUP_LANE, g_idx.astype(F32), comb)
    cnt = jnp.sum(jnp.where(lane == g_idx, 1.0, 0.0), axis=0, keepdims=True)
    cnt_ref[0] = jnp.broadcast_to(cnt, cnt_ref.shape[1:])


def _merge(x2d, f2d, o2d, gate2d, lw):
    t = x2d.shape[0]
    tm = min(MERGE_TILE, t)
    row = lambda w: pl.BlockSpec((tm, w), lambda i: (i, 0))
    weights = [lw["wfo"], lw["wao"], lw["wout"], lw["ffn_g"], lw["wr_hi"], lw["wr_lo"], lw["br"]]
    return pl.pallas_call(
        _merge_kernel,
        grid=(t // tm,),
        in_specs=[row(D_MODEL), row(FOURIER_DIM), row(HEAD_PAD), row(2 * D_MODEL)] + [_full(w.shape) for w in weights],
        out_specs=[row(D_MODEL), row(D_MODEL), row(ROUTER_LANES),
                   pl.BlockSpec((1, SUBLANES, ROUTER_LANES), lambda i: (i, 0, 0))],
        out_shape=[jax.ShapeDtypeStruct((t, D_MODEL), F32), jax.ShapeDtypeStruct((t, D_MODEL), BF16),
                   jax.ShapeDtypeStruct((t, ROUTER_LANES), F32),
                   jax.ShapeDtypeStruct((t // tm, SUBLANES, ROUTER_LANES), F32)],
        compiler_params=_params("parallel"),
        name="merge",
    )(x2d, f2d, o2d, gate2d, *weights)


def _moe_kernel(cnt_ref, x1_ref, xn_ref, comb_ref, wg_ref, wu_ref, wd_ref, fin_ref, out_ref,
                acc_ref, dcol_ref, drow_ref, chi_ref, clo_ref, *, final_norm):
    i, g = pl.program_id(0), pl.program_id(1)
    tm = xn_ref.shape[0]

    def schedule(gg):
        n = cnt_ref[i * N_GROUPS + gg]
        full, rem = n // MOE_CHUNK, n % MOE_CHUNK
        plain = jnp.where(rem > 0, jnp.maximum(full - 1, 0), full)
        need = jnp.where(rem > 0, jnp.where(full > 0, MOE_CHUNK + rem, rem), 0)
        tail = jnp.int32(0)
        for size in reversed(MOE_TAILS):
            tail = jnp.where(need <= size, size, tail)
        return plain, jnp.where(need > 0, tail, 0)

    def rows(gg):
        plain, tail = schedule(gg)
        return plain * MOE_CHUNK + tail

    @pl.when(g == 0)
    def _():
        comb = comb_ref[...]
        lane = lax.broadcasted_iota(jnp.int32, comb.shape, 1)
        grp = comb[:, GROUP_LANE:GROUP_LANE + 1].astype(jnp.int32)
        onehot = jnp.where((lane == grp) & (lane < N_GROUPS), 1.0, 0.0)
        earlier = lax.broadcasted_iota(jnp.int32, (tm, tm), 0) > lax.broadcasted_iota(jnp.int32, (tm, tm), 1)
        before = _dot(jnp.where(earlier, 1.0, 0.0).astype(BF16), onehot.astype(BF16))
        dest = jnp.sum(onehot * before, axis=1, keepdims=True)
        first_row = jnp.int32(0)
        for gg in range(N_GROUPS):
            dest = dest + jnp.where(grp == gg, first_row.astype(F32), 0.0)
            first_row = first_row + rows(gg)
        dcol = jnp.broadcast_to(dest, comb.shape)
        dcol_ref[...] = dcol
        drow_ref[...] = dcol.T[:SUBLANES, :]
        w = jnp.where(lane == GROUP_LANE, 0.0, comb)
        chi = w.astype(BF16)
        chi_ref[...] = chi
        clo_ref[...] = (w - chi.astype(F32)).astype(BF16)
        acc_ref[...] = jnp.zeros_like(acc_ref)

    first_row = jnp.int32(0)
    for gg in range(N_GROUPS):
        first_row = first_row + jnp.where(gg < g, rows(gg), 0)
    plain, tail = schedule(g)

    def chunk(start, size):
        row0 = start.astype(F32)
        slot = lax.broadcasted_iota(jnp.int32, (size, tm), 0).astype(F32) + row0
        gather = jnp.where(drow_ref[0:1, :] == slot, 1.0, 0.0).astype(BF16)
        xs = _dot(gather, xn_ref[...]).astype(BF16)
        cs = _dot(gather, chi_ref[...]) + _dot(gather, clo_ref[...])
        lane = lax.broadcasted_iota(jnp.int32, cs.shape, 1)
        y = None
        for e in range(EXPERTS_PER_GROUP):
            c_e = jnp.sum(jnp.where(lane == EXPERT_LANE0 + g * EXPERTS_PER_GROUP + e, cs, 0.0), axis=1, keepdims=True)
            hg = _dot(xs, wg_ref[e])
            hu = _dot(xs, wu_ref[e])
            hs = (hg * jax.nn.sigmoid(hg) * hu * c_e).astype(BF16)
            ye = _dot(hs, wd_ref[e])
            y = ye if y is None else y + ye
        slot_t = lax.broadcasted_iota(jnp.int32, (tm, size), 1).astype(F32) + row0
        scatter = jnp.where(dcol_ref[:, 0:1] == slot_t, 1.0, 0.0).astype(BF16)
        acc_ref[...] += _dot(scatter, y.astype(BF16))

    def plain_chunk(c, carry):
        chunk(first_row + c * MOE_CHUNK, MOE_CHUNK)
        return carry

    lax.fori_loop(0, plain, plain_chunk, 0)
    for size in MOE_TAILS:
        @pl.when(tail == size)
        def _(size=size):
            chunk(first_row + plain * MOE_CHUNK, size)

    @pl.when(g == pl.num_programs(1) - 1)
    def _():
        x2 = x1_ref[...] + acc_ref[...]
        out_ref[...] = _rms(x2, fin_ref[...]) if final_norm else x2


def _moe(x1, xn, comb, cnt, lw, final_g, final_norm):
    t = x1.shape[0]
    tm = min(MOE_TILE, t)
    per = tm // min(MERGE_TILE, t)
    counts = cnt[:, 0, :N_GROUPS].reshape(t // tm, per, N_GROUPS).sum(axis=1).astype(jnp.int32).reshape(-1)
    row = lambda w: pl.BlockSpec((tm, w), lambda i, g, c: (i, 0))
    wspec = lambda a, b: pl.BlockSpec((EXPERTS_PER_GROUP, a, b), lambda i, g, c: (g, 0, 0))
    grid_spec = pltpu.PrefetchScalarGridSpec(
        num_scalar_prefetch=1,
        grid=(t // tm, N_GROUPS),
        in_specs=[row(D_MODEL), row(D_MODEL), row(ROUTER_LANES), wspec(D_MODEL, D_EXPERT), wspec(D_MODEL, D_EXPERT),
                  wspec(D_EXPERT, D_MODEL), pl.BlockSpec((1, D_MODEL), lambda i, g, c: (0, 0))],
        out_specs=row(D_MODEL),
        scratch_shapes=[pltpu.VMEM((tm, D_MODEL), F32), pltpu.VMEM((tm, ROUTER_LANES), F32),
                        pltpu.VMEM((SUBLANES, tm), F32), pltpu.VMEM((tm, ROUTER_LANES), BF16),
                        pltpu.VMEM((tm, ROUTER_LANES), BF16)],
    )
    return pl.pallas_call(
        functools.partial(_moe_kernel, final_norm=final_norm),
        grid_spec=grid_spec,
        out_shape=jax.ShapeDtypeStruct((t, D_MODEL), F32),
        compiler_params=_params("parallel", "arbitrary"),
        name="moe",
    )(counts, x1, xn, comb, lw["wg"], lw["wu"], lw["wd"], final_g)


def _dft_angles(n):
    return 2.0 * np.pi * ((np.arange(n)[:, None] * np.arange(n)[None, :]) % n) / n


def _dft_tables(seq):
    n1, n2, n = seq // FFT_N2, FFT_N2, FOURIER_GROUP_DIM
    a1, a2, ac = _dft_angles(n1), _dft_angles(n2), _dft_angles(n)
    f1 = np.concatenate([np.cos(a1), -np.sin(a1)], axis=0)
    m2 = np.block([[np.cos(a2), np.sin(a2)], [-np.sin(a2), np.cos(a2)]])
    wcs = np.zeros((2 * FOURIER_DIM, FOURIER_DIM), np.float64)
    for g in range(FOURIER_GROUPS):
        sl = slice(g * n, (g + 1) * n)
        wcs[sl, sl] = np.cos(ac)
        wcs[FOURIER_DIM + g * n:FOURIER_DIM + (g + 1) * n, sl] = np.sin(ac)
    idx = (jnp.arange(n2, dtype=jnp.int32)[:, None] * jnp.arange(n1, dtype=jnp.int32)[None, :]) % seq
    ang = jnp.broadcast_to((idx.astype(F32) * (2.0 * math.pi / seq))[:, :, None], (n2, n1, LANES))
    return {"f1": jnp.asarray(f1, BF16), "m2": jnp.asarray(m2, BF16), "wcs": jnp.asarray(wcs, BF16),
            "twc": jnp.cos(ang), "tws": jnp.sin(ang)}


def _rope_key_placement():
    ekr = np.zeros((LANES, HEAD_PAD), np.float32)
    for h in range(N_HEADS):
        for r in range(QK_ROPE_DIM):
            ekr[r, h * LANES + QK_NOPE_DIM + r] = 1.0
    return jnp.asarray(ekr, BF16)


def _rope_tables(seq):
    inv = 1.0 / (ROPE_BASE ** (jnp.arange(0, QK_ROPE_DIM, 2, dtype=F32) / QK_ROPE_DIM))
    rang = jnp.arange(seq, dtype=F32)[:, None] * inv[None, :]
    c, s = jnp.cos(rang), jnp.sin(rang)
    z = lambda w: jnp.zeros((seq, w), F32)
    tail = LANES - QK_DIM
    cos = jnp.concatenate([jnp.ones((seq, QK_NOPE_DIM), F32), c, c, z(tail)], axis=1)
    sin1 = jnp.concatenate([z(QK_NOPE_DIM), -s, z(ROPE_HALF), z(tail)], axis=1)
    sin2 = jnp.concatenate([z(QK_NOPE_DIM), z(ROPE_HALF), s, z(tail)], axis=1)
    return cos, sin1, sin2


def _pad_heads(w, per_head, lo, hi, width=LANES):
    r = w.shape[0]
    w3 = w.reshape(r, N_HEADS, per_head)[:, :, lo:hi]
    return jnp.pad(w3, ((0, 0), (0, 0), (0, width - (hi - lo)))).reshape(r, N_HEADS * width)


def _layer_weights(l, attn_norm_g, w_in, q_norm_g, kv_norm_g, w_uq, w_ukv, w_fourier_out, w_attn_out, w_out,
                   ffn_norm_g, w_grp, b_grp, w_exp, b_exp, w_gate, w_up, w_down):
    s0, s1, s2, s3 = (FOURIER_DIM, FOURIER_DIM + Q_LORA_RANK, FOURIER_DIM + Q_LORA_RANK + KV_LORA_RANK,
                      FOURIER_DIM + Q_LORA_RANK + KV_LORA_RANK + QK_ROPE_DIM)
    wi = w_in[l]
    wr = jnp.concatenate([w_grp[l], w_exp[l]], axis=1)
    wr = jnp.pad(wr, ((0, 0), (0, ROUTER_LANES - wr.shape[1])))
    wr_hi = wr.astype(BF16)
    br = jnp.concatenate([b_grp[l], b_exp[l]])
    wao = jnp.pad(w_attn_out[l].reshape(N_HEADS, V_HEAD_DIM, D_MODEL), ((0, 0), (0, LANES - V_HEAD_DIM), (0, 0)))
    return {
        "attn_g": attn_norm_g[l][None, :],
        "wf": wi[:, :s0].astype(BF16),
        "wql": wi[:, s0:s1].astype(BF16),
        "wkvl": wi[:, s1:s2].astype(BF16),
        "wkr": jnp.pad(wi[:, s2:s3], ((0, 0), (0, LANES - QK_ROPE_DIM))).astype(BF16),
        "wgate": wi[:, s3:].astype(BF16),
        "qg": q_norm_g[l][None, :],
        "kvg": kv_norm_g[l][None, :],
        "wuqt": _pad_heads(w_uq[l], QK_DIM, 0, QK_DIM).T.astype(BF16),
        "wuk": _pad_heads(w_ukv[l], QK_NOPE_DIM + V_HEAD_DIM, 0, QK_NOPE_DIM).astype(BF16),
        "wuvt": _pad_heads(w_ukv[l], QK_NOPE_DIM + V_HEAD_DIM, QK_NOPE_DIM, QK_NOPE_DIM + V_HEAD_DIM,
                           V_ROWS).T.astype(BF16),
        "ekr": _rope_key_placement(),
        "wfo": w_fourier_out[l].astype(BF16),
        "wao": wao.reshape(HEAD_PAD, D_MODEL).astype(BF16),
        "wout": w_out[l].astype(BF16),
        "ffn_g": ffn_norm_g[l][None, :],
        "wr_hi": wr_hi,
        "wr_lo": (wr - wr_hi.astype(F32)).astype(BF16),
        "br": jnp.pad(br, (0, ROUTER_LANES - br.shape[0]))[None, :],
        "wg": w_gate[l].astype(BF16),
        "wu": w_up[l].astype(BF16),
        "wd": w_down[l].astype(BF16),
    }


def _trunk(x, layers, final_g, tables):
    bsz, seq, d = x.shape
    t = bsz * seq
    dft, rope = tables
    x2d = x.reshape(t, d)
    for l, lw in enumerate(layers):
        u, k, gate, qt, vt = _in_proj(x2d, seq, lw, rope)
        shp = lambda z: z.reshape(bsz, seq, z.shape[-1])
        f = _fourier_mix(u, bsz, seq, dft)
        o = _attention(qt, shp(k), vt).reshape(t, HEAD_PAD)
        x1, xn, comb, cnt = _merge(x2d, f, o, gate, lw)
        x2d = _moe(x1, xn, comb, cnt, lw, final_g, final_norm=(l == len(layers) - 1))
    return x2d.reshape(bsz, seq, d)


def kernel(x_prompt, x_sample, attn_norm_g, w_in, q_norm_g, kv_norm_g, w_uq, w_ukv, w_fourier_out, w_attn_out, w_out, ffn_norm_g, w_grp, b_grp, w_exp, b_exp, w_gate, w_up, w_down, final_norm_g):
    depth = w_in.shape[0]
    layers = [_layer_weights(l, attn_norm_g, w_in, q_norm_g, kv_norm_g, w_uq, w_ukv, w_fourier_out, w_attn_out,
                             w_out, ffn_norm_g, w_grp, b_grp, w_exp, b_exp, w_gate, w_up, w_down)
              for l in range(depth)]
    final_g = final_norm_g[None, :]
    outs = []
    for x in (x_prompt, x_sample):
        seq = x.shape[1]
        outs.append(_trunk(x, layers, final_g, (_dft_tables(seq), _rope_tables(seq))))
    return tuple(outs)
```

```python
import functools
import math

import numpy as np
import jax
import jax.numpy as jnp
from jax import lax
from jax.experimental import pallas as pl
from jax.experimental.pallas import tpu as pltpu

D_MODEL = 1024
FOURIER_GROUPS = 4
FOURIER_GROUP_DIM = 128
FOURIER_DIM = FOURIER_GROUPS * FOURIER_GROUP_DIM
N_HEADS = 8
QK_NOPE_DIM = 64
QK_ROPE_DIM = 32
V_HEAD_DIM = 64
Q_LORA_RANK = 384
KV_LORA_RANK = 256
QK_DIM = QK_NOPE_DIM + QK_ROPE_DIM
ROPE_BASE = 10000.0
N_GROUPS = 4
EXPERTS_PER_GROUP = 8
N_EXPERTS = N_GROUPS * EXPERTS_PER_GROUP
D_EXPERT = 256
EPS = 1e-6

LANES = 128
HEAD_PAD = N_HEADS * LANES
ROPE_HALF = QK_ROPE_DIM // 2
ONES_ROW = V_HEAD_DIM
V_ROWS = 80
FFT_N2 = 128
ROUTER_LANES = LANES
EXPERT_LANE0 = N_GROUPS
GROUP_LANE = 0
SUBLANES = 8
ATTN_Q_TILE = 4096
IN_PROJ_TILE = 1024
MERGE_TILE = 1024
MOE_TILE = 1024
MOE_CHUNK = 256
MOE_TAILS = (256, 320, 512)
VMEM_LIMIT = 56 * 1024 * 1024

BF16 = jnp.bfloat16
F32 = jnp.float32


def _dot(a, b):
    return jnp.dot(a, b, preferred_element_type=F32)


def _rms(x, g):
    return x * lax.rsqrt(jnp.mean(x * x, axis=-1, keepdims=True) + EPS) * g


def _params(*sem):
    return pltpu.CompilerParams(dimension_semantics=sem, vmem_limit_bytes=VMEM_LIMIT)


def _full(shape):
    return pl.BlockSpec(shape, lambda *_: (0,) * len(shape))


def _in_proj_kernel(x_ref, g_ref, wf_ref, wql_ref, wkvl_ref, wkr_ref, wgate_ref, qg_ref, kvg_ref,
                    wuqt_ref, wuk_ref, wuvt_ref, ekr_ref, cos_ref, sin1_ref, sin2_ref, cost_ref, sin1t_ref, sin2t_ref,
                    u_ref, k_ref, gate_ref, qt_ref, vt_ref, *, q_scale):
    xb = _rms(x_ref[...], g_ref[...]).astype(BF16)
    u_ref[...] = _dot(xb, wf_ref[...]).astype(BF16)
    qn = _rms(_dot(xb, wql_ref[...]), qg_ref[...]).astype(BF16)
    kvn = _rms(_dot(xb, wkvl_ref[...]), kvg_ref[...]).astype(BF16)
    ukr = _dot(xb, wkr_ref[...]).astype(BF16)
    k = _dot(kvn, wuk_ref[...]) + _dot(ukr, ekr_ref[...])
    nt = (((1,), (1,)), ((), ()))
    qt = lax.dot_general(wuqt_ref[...], qn, nt, preferred_element_type=F32)
    vt = lax.dot_general(wuvt_ref[...], kvn, nt, preferred_element_type=F32)
    head_row = lax.broadcasted_iota(jnp.int32, vt.shape, 0) % V_ROWS
    vt = jnp.where(head_row == ONES_ROW, 1.0, vt).astype(BF16)
    tk = vt_ref.shape[-1]
    for c in range(vt_ref.shape[1]):
        vt_ref[0, c] = vt[:, c * tk:(c + 1) * tk]
    cos, sin1, sin2 = cos_ref[...], sin1_ref[...], sin2_ref[...]
    cost, sin1t, sin2t = cost_ref[...], sin1t_ref[...], sin2t_ref[...]
    k_up = pltpu.roll(k, HEAD_PAD - ROPE_HALF, 1)
    k_dn = pltpu.roll(k, ROPE_HALF, 1)
    qt_up = jnp.concatenate([qt[ROPE_HALF:], qt[:ROPE_HALF]], axis=0)
    qt_dn = jnp.concatenate([qt[-ROPE_HALF:], qt[:-ROPE_HALF]], axis=0)
    for h in range(N_HEADS):
        sl = slice(h * LANES, (h + 1) * LANES)
        k_ref[:, sl] = (k[:, sl] * cos + k_up[:, sl] * sin1 + k_dn[:, sl] * sin2).astype(BF16)
        qt_ref[0, sl, :] = ((qt[sl] * cost + qt_up[sl] * sin1t + qt_dn[sl] * sin2t) * q_scale).astype(BF16)
    gate_ref[...] = jax.nn.sigmoid(_dot(xb, wgate_ref[...])).astype(BF16)


def _in_proj(x2d, seq, lw, rope):
    t = x2d.shape[0]
    tk = _key_chunk(seq)
    tm = max(tk, min(IN_PROJ_TILE, seq))
    n_seq_tiles = seq // tm
    row = lambda w: pl.BlockSpec((tm, w), lambda i: (i, 0))
    pos = pl.BlockSpec((tm, LANES), lambda i: (i % n_seq_tiles, 0))
    pos_t = pl.BlockSpec((LANES, tm), lambda i: (0, i % n_seq_tiles))
    weights = [lw["attn_g"], lw["wf"], lw["wql"], lw["wkvl"], lw["wkr"], lw["wgate"], lw["qg"], lw["kvg"],
               lw["wuqt"], lw["wuk"], lw["wuvt"], lw["ekr"]]
    out_w = [FOURIER_DIM, HEAD_PAD, 2 * D_MODEL]
    qt_spec = pl.BlockSpec((1, HEAD_PAD, tm), lambda i: (i // n_seq_tiles, 0, i % n_seq_tiles))
    qt_shape = jax.ShapeDtypeStruct((t // seq, HEAD_PAD, seq), BF16)
    vt_spec = pl.BlockSpec((1, tm // tk, N_HEADS * V_ROWS, tk), lambda i: (i // n_seq_tiles, i % n_seq_tiles, 0, 0))
    vt_shape = jax.ShapeDtypeStruct((t // seq, seq // tk, N_HEADS * V_ROWS, tk), BF16)
    return pl.pallas_call(
        functools.partial(_in_proj_kernel, q_scale=(QK_DIM ** -0.5) * math.log2(math.e)),
        grid=(t // tm,),
        in_specs=[row(D_MODEL)] + [_full(w.shape) for w in weights] + [pos] * 3 + [pos_t] * 3,
        out_specs=[row(w) for w in out_w] + [qt_spec, vt_spec],
        out_shape=[jax.ShapeDtypeStruct((t, w), BF16) for w in out_w] + [qt_shape, vt_shape],
        compiler_params=_params("parallel"),
        name="in_proj",
    )(x2d, *weights, *rope, *[r.T for r in rope])


def _dft_a_kernel(u_ref, f1_ref, twc_ref, tws_ref, y_ref, *, tj, n1):
    f1 = f1_ref[...]
    for j in range(tj):
        y = _dot(f1, u_ref[0, j])
        yr, yi = y[:n1], y[n1:]
        c = jnp.concatenate([twc_ref[j]] * FOURIER_GROUPS, axis=1)
        s = jnp.concatenate([tws_ref[j]] * FOURIER_GROUPS, axis=1)
        y_ref[0, j, 0] = (yr * c + yi * s).astype(BF16)
        y_ref[0, j, 1] = (yi * c - yr * s).astype(BF16)


def _dft_a(u4, f1, twc, tws):
    bsz, n2, n1, _ = u4.shape
    tj = 16
    return pl.pallas_call(
        functools.partial(_dft_a_kernel, tj=tj, n1=n1),
        grid=(bsz, n2 // tj),
        in_specs=[pl.BlockSpec((1, tj, n1, FOURIER_DIM), lambda b, j: (b, j, 0, 0)), _full(f1.shape),
                  pl.BlockSpec((tj, n1, LANES), lambda b, j: (j, 0, 0)),
                  pl.BlockSpec((tj, n1, LANES), lambda b, j: (j, 0, 0))],
        out_specs=pl.BlockSpec((1, tj, 2, n1, FOURIER_DIM), lambda b, j: (b, j, 0, 0, 0)),
        out_shape=jax.ShapeDtypeStruct((bsz, n2, 2, n1, FOURIER_DIM), BF16),
        compiler_params=_params("parallel", "parallel"),
        name="dft_a",
    )(u4, f1, twc, tws)


def _dft_b_kernel(y_ref, m2_ref, wcs_ref, f_ref, x_scr, *, tk1, scale):
    m2 = m2_ref[...]
    for j in range(tk1):
        x = _dot(m2, y_ref[0, j].reshape(2 * FFT_N2, FOURIER_DIM))
        rows = slice(j * FFT_N2, (j + 1) * FFT_N2)
        x_scr[rows, :FOURIER_DIM] = x[:FFT_N2].astype(BF16)
        x_scr[rows, FOURIER_DIM:] = x[FFT_N2:].astype(BF16)
    f = _dot(x_scr[...], wcs_ref[...]) * scale
    f_ref[0] = f.reshape(tk1, FFT_N2, FOURIER_DIM).astype(BF16)


def _dft_b(y5, m2, wcs, seq):
    bsz, n1 = y5.shape[:2]
    tk1 = 8
    return pl.pallas_call(
        functools.partial(_dft_b_kernel, tk1=tk1, scale=(seq * FOURIER_GROUP_DIM) ** -0.5),
        grid=(bsz, n1 // tk1),
        in_specs=[pl.BlockSpec((1, tk1, 2, FFT_N2, FOURIER_DIM), lambda b, j: (b, j, 0, 0, 0)),
                  _full(m2.shape), _full(wcs.shape)],
        out_specs=pl.BlockSpec((1, tk1, FFT_N2, FOURIER_DIM), lambda b, j: (b, j, 0, 0)),
        out_shape=jax.ShapeDtypeStruct((bsz, n1, FFT_N2, FOURIER_DIM), BF16),
        scratch_shapes=[pltpu.VMEM((tk1 * FFT_N2, 2 * FOURIER_DIM), BF16)],
        compiler_params=_params("parallel", "parallel"),
        name="dft_b",
    )(y5, m2, wcs)


def _fourier_mix(u2d, bsz, seq, dft):
    n1 = seq // FFT_N2
    u4 = u2d.reshape(bsz, n1, FFT_N2, FOURIER_DIM).transpose(0, 2, 1, 3)
    y5 = _dft_a(u4, dft["f1"], dft["twc"], dft["tws"])
    f4 = _dft_b(y5.transpose(0, 3, 2, 1, 4), dft["m2"], dft["wcs"], seq)
    return f4.transpose(0, 2, 1, 3).reshape(bsz * seq, FOURIER_DIM)


def _key_chunk(seq):
    return min(512, seq // 2)


def _attn_kernel(q_ref, k_ref, vt_ref, o_ref, m_ref, acc_ref, s_ref, smax_ref, *, tk, nk):
    q = q_ref[0]
    m_ref[...] = jnp.full_like(m_ref, -jnp.inf)
    acc_ref[...] = jnp.zeros_like(acc_ref)

    def scores(kc, slot):
        ks = pl.multiple_of(kc * tk, tk)
        st = _dot(k_ref[0, pl.ds(ks, tk), :], q)
        s_ref[slot] = st
        smax_ref[slot] = jnp.max(st, axis=0, keepdims=True)

    def softmax_pv(kc, slot):
        m_prev = m_ref[...]
        m_new = jnp.maximum(m_prev, smax_ref[slot])
        alpha = jnp.exp2(m_prev - m_new)
        p = jnp.exp2(s_ref[slot] - m_new).astype(BF16)
        acc_ref[...] = alpha * acc_ref[...] + _dot(vt_ref[0, kc], p)
        m_ref[...] = m_new

    scores(0, 0)

    def body(j, carry):
        c = 2 * j
        scores(c + 1, 1)
        softmax_pv(c, 0)
        scores(jnp.minimum(c + 2, nk - 1), 0)
        softmax_pv(c + 1, 1)
        return carry

    lax.fori_loop(0, nk // 2, body, 0)
    acc = acc_ref[...]
    o = acc / acc[ONES_ROW:ONES_ROW + 1, :]
    o = jnp.concatenate([o, jnp.zeros((LANES - V_ROWS, o.shape[1]), F32)], axis=0)
    o_ref[0] = o.T.astype(BF16)


def _attention(qt3, k3, vt4):
    bsz, seq, _ = k3.shape
    tq = min(ATTN_Q_TILE, seq)
    tk = _key_chunk(seq)
    nk = seq // tk
    qt = pl.BlockSpec((1, LANES, tq), lambda b, h, i: (b, h, i))
    qo = pl.BlockSpec((1, tq, LANES), lambda b, h, i: (b, i, h))
    kk = pl.BlockSpec((1, seq, LANES), lambda b, h, i: (b, 0, h))
    vt = pl.BlockSpec((1, nk, V_ROWS, tk), lambda b, h, i: (b, 0, h, 0))
    return pl.pallas_call(
        functools.partial(_attn_kernel, tk=tk, nk=nk),
        grid=(bsz, N_HEADS, seq // tq),
        in_specs=[qt, kk, vt],
        out_specs=qo,
        out_shape=jax.ShapeDtypeStruct((bsz, seq, HEAD_PAD), BF16),
        scratch_shapes=[pltpu.VMEM((1, tq), F32), pltpu.VMEM((V_ROWS, tq), F32), pltpu.VMEM((2, tk, tq), F32),
                        pltpu.VMEM((2, 1, tq), F32)],
        compiler_params=_params("parallel", "parallel", "arbitrary"),
        name="attn",
    )(qt3, k3, vt4)


def _merge_kernel(x_ref, f_ref, o_ref, gate_ref, wfo_ref, wao_ref, wout_ref, fg_ref, wrh_ref, wrl_ref, br_ref,
                  x1_ref, xn_ref, comb_ref, cnt_ref):
    y_f = _dot(f_ref[...], wfo_ref[...])
    y_a = _dot(o_ref[...], wao_ref[...])
    gate = gate_ref[...].astype(F32)
    merged = gate[:, :D_MODEL] * y_f + gate[:, D_MODEL:] * y_a
    x1 = x_ref[...] + _dot(merged.astype(BF16), wout_ref[...])
    x1_ref[...] = x1
    xn = _rms(x1, fg_ref[...])
    xh = xn.astype(BF16)
    xn_ref[...] = xh
    xl = (xn - xh.astype(F32)).astype(BF16)
    logit = _dot(xh, wrh_ref[...]) + _dot(xl, wrh_ref[...]) + _dot(xh, wrl_ref[...]) + br_ref[...]
    lane = lax.broadcasted_iota(jnp.int32, logit.shape, 1)
    neg = jnp.float32(-jnp.inf)

    def top(mask):
        val = jnp.max(jnp.where(mask, logit, neg), axis=1, keepdims=True)
        idx = jnp.min(jnp.where(mask & (logit == val), lane, ROUTER_LANES), axis=1, keepdims=True)
        return val, idx

    is_grp = lane < N_GROUPS
    g_max, g_idx = top(is_grp)
    g_sum = jnp.sum(jnp.where(is_grp, jnp.exp(logit - g_max), 0.0), axis=1, keepdims=True)
    grp_p = 1.0 / g_sum
    lo = EXPERT_LANE0 + g_idx * EXPERTS_PER_GROUP
    in_grp = (lane >= lo) & (lane < lo + EXPERTS_PER_GROUP)
    l1, i1 = top(in_grp)
    l2, i2 = top(in_grp & (lane != i1))
    e2 = jnp.exp(l2 - l1)
    w1 = 1.0 / (1.0 + e2)
    w2 = e2 / (1.0 + e2)
    comb = jnp.where(lane == i1, grp_p * w1, jnp.where(lane == i2, grp_p * w2, 0.0))
    comb_ref[...] = jnp.where(lane == GROUP_LANE, g_idx.astype(F32), comb)
    cnt = jnp.sum(jnp.where(lane == g_idx, 1.0, 0.0), axis=0, keepdims=True)
    cnt_ref[0] = jnp.broadcast_to(cnt, cnt_ref.shape[1:])


def _merge(x2d, f2d, o2d, gate2d, lw):
    t = x2d.shape[0]
    tm = min(MERGE_TILE, t)
    row = lambda w: pl.BlockSpec((tm, w), lambda i: (i, 0))
    weights = [lw["wfo"], lw["wao"], lw["wout"], lw["ffn_g"], lw["wr_hi"], lw["wr_lo"], lw["br"]]
    return pl.pallas_call(
        _merge_kernel,
        grid=(t // tm,),
        in_specs=[row(D_MODEL), row(FOURIER_DIM), row(HEAD_PAD), row(2 * D_MODEL)] + [_full(w.shape) for w in weights],
        out_specs=[row(D_MODEL), row(D_MODEL), row(ROUTER_LANES),
                   pl.BlockSpec((1, SUBLANES, ROUTER_LANES), lambda i: (i, 0, 0))],
        out_shape=[jax.ShapeDtypeStruct((t, D_MODEL), F32), jax.ShapeDtypeStruct((t, D_MODEL), BF16),
                   jax.ShapeDtypeStruct((t, ROUTER_LANES), F32),
                   jax.ShapeDtypeStruct((t // tm, SUBLANES, ROUTER_LANES), F32)],
        compiler_params=_params("parallel"),
        name="merge",
    )(x2d, f2d, o2d, gate2d, *weights)


def _moe_kernel(cnt_ref, x1_ref, xn_ref, comb_ref, wg_ref, wu_ref, wd_ref, fin_ref, out_ref,
                acc_ref, dcol_ref, drow_ref, chi_ref, clo_ref, *, final_norm):
    i, g = pl.program_id(0), pl.program_id(1)
    tm = xn_ref.shape[0]

    def schedule(gg):
        n = cnt_ref[i * N_GROUPS + gg]
        full, rem = n // MOE_CHUNK, n % MOE_CHUNK
        plain = jnp.where(rem > 0, jnp.maximum(full - 1, 0), full)
        need = jnp.where(rem > 0, jnp.where(full > 0, MOE_CHUNK + rem, rem), 0)
        tail = jnp.int32(0)
        for size in reversed(MOE_TAILS):
            tail = jnp.where(need <= size, size, tail)
        return plain, jnp.where(need > 0, tail, 0)

    def rows(gg):
        plain, tail = schedule(gg)
        return plain * MOE_CHUNK + tail

    @pl.when(g == 0)
    def _():
        comb = comb_ref[...]
        lane = lax.broadcasted_iota(jnp.int32, comb.shape, 1)
        grp = comb[:, GROUP_LANE:GROUP_LANE + 1].astype(jnp.int32)
        onehot = jnp.where((lane == grp) & (lane < N_GROUPS), 1.0, 0.0)
        earlier = lax.broadcasted_iota(jnp.int32, (tm, tm), 0) > lax.broadcasted_iota(jnp.int32, (tm, tm), 1)
        before = _dot(jnp.where(earlier, 1.0, 0.0).astype(BF16), onehot.astype(BF16))
        dest = jnp.sum(onehot * before, axis=1, keepdims=True)
        first_row = jnp.int32(0)
        for gg in range(N_GROUPS):
            dest = dest + jnp.where(grp == gg, first_row.astype(F32), 0.0)
            first_row = first_row + rows(gg)
        dcol = jnp.broadcast_to(dest, comb.shape)
        dcol_ref[...] = dcol
        drow_ref[...] = dcol.T[:SUBLANES, :]
        w = jnp.where(lane == GROUP_LANE, 0.0, comb)
        chi = w.astype(BF16)
        chi_ref[...] = chi
        clo_ref[...] = (w - chi.astype(F32)).astype(BF16)
        acc_ref[...] = jnp.zeros_like(acc_ref)

    first_row = jnp.int32(0)
    for gg in range(N_GROUPS):
        first_row = first_row + jnp.where(gg < g, rows(gg), 0)
    plain, tail = schedule(g)

    def chunk(start, size):
        row0 = start.astype(F32)
        slot = lax.broadcasted_iota(jnp.int32, (size, tm), 0).astype(F32) + row0
        gather = jnp.where(drow_ref[0:1, :] == slot, 1.0, 0.0).astype(BF16)
        xs = _dot(gather, xn_ref[...]).astype(BF16)
        cs = _dot(gather, chi_ref[...]) + _dot(gather, clo_ref[...])
        lane = lax.broadcasted_iota(jnp.int32, cs.shape, 1)
        y = None
        for e in range(EXPERTS_PER_GROUP):
            c_e = jnp.sum(jnp.where(lane == EXPERT_LANE0 + g * EXPERTS_PER_GROUP + e, cs, 0.0), axis=1, keepdims=True)
            hg = _dot(xs, wg_ref[e])
            hu = _dot(xs, wu_ref[e])
            hs = (hg * jax.nn.sigmoid(hg) * hu * c_e).astype(BF16)
            ye = _dot(hs, wd_ref[e])
            y = ye if y is None else y + ye
        slot_t = lax.broadcasted_iota(jnp.int32, (tm, size), 1).astype(F32) + row0
        scatter = jnp.where(dcol_ref[:, 0:1] == slot_t, 1.0, 0.0).astype(BF16)
        acc_ref[...] += _dot(scatter, y.astype(BF16))

    def plain_chunk(c, carry):
        chunk(first_row + c * MOE_CHUNK, MOE_CHUNK)
        return carry

    lax.fori_loop(0, plain, plain_chunk, 0)
    for size in MOE_TAILS:
        @pl.when(tail == size)
        def _(size=size):
            chunk(first_row + plain * MOE_CHUNK, size)

    @pl.when(g == pl.num_programs(1) - 1)
    def _():
        x2 = x1_ref[...] + acc_ref[...]
        out_ref[...] = _rms(x2, fin_ref[...]) if final_norm else x2


def _moe(x1, xn, comb, cnt, lw, final_g, final_norm):
    t = x1.shape[0]
    tm = min(MOE_TILE, t)
    per = tm // min(MERGE_TILE, t)
    counts = cnt[:, 0, :N_GROUPS].reshape(t // tm, per, N_GROUPS).sum(axis=1).astype(jnp.int32).reshape(-1)
    row = lambda w: pl.BlockSpec((tm, w), lambda i, g, c: (i, 0))
    wspec = lambda a, b: pl.BlockSpec((EXPERTS_PER_GROUP, a, b), lambda i, g, c: (g, 0, 0))
    grid_spec = pltpu.PrefetchScalarGridSpec(
        num_scalar_prefetch=1,
        grid=(t // tm, N_GROUPS),
        in_specs=[row(D_MODEL), row(D_MODEL), row(ROUTER_LANES), wspec(D_MODEL, D_EXPERT), wspec(D_MODEL, D_EXPERT),
                  wspec(D_EXPERT, D_MODEL), pl.BlockSpec((1, D_MODEL), lambda i, g, c: (0, 0))],
        out_specs=row(D_MODEL),
        scratch_shapes=[pltpu.VMEM((tm, D_MODEL), F32), pltpu.VMEM((tm, ROUTER_LANES), F32),
                        pltpu.VMEM((SUBLANES, tm), F32), pltpu.VMEM((tm, ROUTER_LANES), BF16),
                        pltpu.VMEM((tm, ROUTER_LANES), BF16)],
    )
    return pl.pallas_call(
        functools.partial(_moe_kernel, final_norm=final_norm),
        grid_spec=grid_spec,
        out_shape=jax.ShapeDtypeStruct((t, D_MODEL), F32),
        compiler_params=_params("parallel", "arbitrary"),
        name="moe",
    )(counts, x1, xn, comb, lw["wg"], lw["wu"], lw["wd"], final_g)


def _dft_angles(n):
    return 2.0 * np.pi * ((np.arange(n)[:, None] * np.arange(n)[None, :]) % n) / n


def _dft_tables(seq):
    n1, n2, n = seq // FFT_N2, FFT_N2, FOURIER_GROUP_DIM
    a1, a2, ac = _dft_angles(n1), _dft_angles(n2), _dft_angles(n)
    f1 = np.concatenate([np.cos(a1), -np.sin(a1)], axis=0)
    m2 = np.block([[np.cos(a2), np.sin(a2)], [-np.sin(a2), np.cos(a2)]])
    wcs = np.zeros((2 * FOURIER_DIM, FOURIER_DIM), np.float64)
    for g in range(FOURIER_GROUPS):
        sl = slice(g * n, (g + 1) * n)
        wcs[sl, sl] = np.cos(ac)
        wcs[FOURIER_DIM + g * n:FOURIER_DIM + (g + 1) * n, sl] = np.sin(ac)
    idx = (jnp.arange(n2, dtype=jnp.int32)[:, None] * jnp.arange(n1, dtype=jnp.int32)[None, :]) % seq
    ang = jnp.broadcast_to((idx.astype(F32) * (2.0 * math.pi / seq))[:, :, None], (n2, n1, LANES))
    return {"f1": jnp.asarray(f1, BF16), "m2": jnp.asarray(m2, BF16), "wcs": jnp.asarray(wcs, BF16),
            "twc": jnp.cos(ang), "tws": jnp.sin(ang)}


def _rope_key_placement():
    ekr = np.zeros((LANES, HEAD_PAD), np.float32)
    for h in range(N_HEADS):
        for r in range(QK_ROPE_DIM):
            ekr[r, h * LANES + QK_NOPE_DIM + r] = 1.0
    return jnp.asarray(ekr, BF16)


def _rope_tables(seq):
    inv = 1.0 / (ROPE_BASE ** (jnp.arange(0, QK_ROPE_DIM, 2, dtype=F32) / QK_ROPE_DIM))
    rang = jnp.arange(seq, dtype=F32)[:, None] * inv[None, :]
    c, s = jnp.cos(rang), jnp.sin(rang)
    z = lambda w: jnp.zeros((seq, w), F32)
    tail = LANES - QK_DIM
    cos = jnp.concatenate([jnp.ones((seq, QK_NOPE_DIM), F32), c, c, z(tail)], axis=1)
    sin1 = jnp.concatenate([z(QK_NOPE_DIM), -s, z(ROPE_HALF), z(tail)], axis=1)
    sin2 = jnp.concatenate([z(QK_NOPE_DIM), z(ROPE_HALF), s, z(tail)], axis=1)
    return cos, sin1, sin2


def _pad_heads(w, per_head, lo, hi, width=LANES):
    r = w.shape[0]
    w3 = w.reshape(r, N_HEADS, per_head)[:, :, lo:hi]
    return jnp.pad(w3, ((0, 0), (0, 0), (0, width - (hi - lo)))).reshape(r, N_HEADS * width)


def _layer_weights(l, attn_norm_g, w_in, q_norm_g, kv_norm_g, w_uq, w_ukv, w_fourier_out, w_attn_out, w_out,
                   ffn_norm_g, w_grp, b_grp, w_exp, b_exp, w_gate, w_up, w_down):
    s0, s1, s2, s3 = (FOURIER_DIM, FOURIER_DIM + Q_LORA_RANK, FOURIER_DIM + Q_LORA_RANK + KV_LORA_RANK,
                      FOURIER_DIM + Q_LORA_RANK + KV_LORA_RANK + QK_ROPE_DIM)
    wi = w_in[l]
    wr = jnp.concatenate([w_grp[l], w_exp[l]], axis=1)
    wr = jnp.pad(wr, ((0, 0), (0, ROUTER_LANES - wr.shape[1])))
    wr_hi = wr.astype(BF16)
    br = jnp.concatenate([b_grp[l], b_exp[l]])
    wao = jnp.pad(w_attn_out[l].reshape(N_HEADS, V_HEAD_DIM, D_MODEL), ((0, 0), (0, LANES - V_HEAD_DIM), (0, 0)))
    return {
        "attn_g": attn_norm_g[l][None, :],
        "wf": wi[:, :s0].astype(BF16),
        "wql": wi[:, s0:s1].astype(BF16),
        "wkvl": wi[:, s1:s2].astype(BF16),
        "wkr": jnp.pad(wi[:, s2:s3], ((0, 0), (0, LANES - QK_ROPE_DIM))).astype(BF16),
        "wgate": wi[:, s3:].astype(BF16),
        "qg": q_norm_g[l][None, :],
        "kvg": kv_norm_g[l][None, :],
        "wuqt": _pad_heads(w_uq[l], QK_DIM, 0, QK_DIM).T.astype(BF16),
        "wuk": _pad_heads(w_ukv[l], QK_NOPE_DIM + V_HEAD_DIM, 0, QK_NOPE_DIM).astype(BF16),
        "wuvt": _pad_heads(w_ukv[l], QK_NOPE_DIM + V_HEAD_DIM, QK_NOPE_DIM, QK_NOPE_DIM + V_HEAD_DIM,
                           V_ROWS).T.astype(BF16),
        "ekr": _rope_key_placement(),
        "wfo": w_fourier_out[l].astype(BF16),
        "wao": wao.reshape(HEAD_PAD, D_MODEL).astype(BF16),
        "wout": w_out[l].astype(BF16),
        "ffn_g": ffn_norm_g[l][None, :],
        "wr_hi": wr_hi,
        "wr_lo": (wr - wr_hi.astype(F32)).astype(BF16),
        "br": jnp.pad(br, (0, ROUTER_LANES - br.shape[0]))[None, :],
        "wg": w_gate[l].astype(BF16),
        "wu": w_up[l].astype(BF16),
        "wd": w_down[l].astype(BF16),
    }


def _trunk(x, layers, final_g, tables):
    bsz, seq, d = x.shape
    t = bsz * seq
    dft, rope = tables
    x2d = x.reshape(t, d)
    for l, lw in enumerate(layers):
        u, k, gate, qt, vt = _in_proj(x2d, seq, lw, rope)
        shp = lambda z: z.reshape(bsz, seq, z.shape[-1])
        f = _fourier_mix(u, bsz, seq, dft)
        o = _attention(qt, shp(k), vt).reshape(t, HEAD_PAD)
        x1, xn, comb, cnt = _merge(x2d, f, o, gate, lw)
        x2d = _moe(x1, xn, comb, cnt, lw, final_g, final_norm=(l == len(layers) - 1))
    return x2d.reshape(bsz, seq, d)


def kernel(x_prompt, x_sample, attn_norm_g, w_in, q_norm_g, kv_norm_g, w_uq, w_ukv, w_fourier_out, w_attn_out, w_out, ffn_norm_g, w_grp, b_grp, w_exp, b_exp, w_gate, w_up, w_down, final_norm_g):
    depth = w_in.shape[0]
    layers = [_layer_weights(l, attn_norm_g, w_in, q_norm_g, kv_norm_g, w_uq, w_ukv, w_fourier_out, w_attn_out,
                             w_out, ffn_norm_g, w_grp, b_grp, w_exp, b_exp, w_gate, w_up, w_down)
              for l in range(depth)]
    final_g = final_norm_g[None, :]
    outs = []
    for x in (x_prompt, x_sample):
        seq = x.shape[1]
        outs.append(_trunk(x, layers, final_g, (_dft_tables(seq), _rope_tables(seq))))
    return tuple(outs)
```

```python
import functools
import math

import numpy as np
import jax
import jax.numpy as jnp
from jax import lax
from jax.experimental import pallas as pl
from jax.experimental.pallas import tpu as pltpu

D_MODEL = 1024
FOURIER_GROUPS = 4
FOURIER_GROUP_DIM = 128
FOURIER_DIM = FOURIER_GROUPS * FOURIER_GROUP_DIM
N_HEADS = 8
QK_NOPE_DIM = 64
QK_ROPE_DIM = 32
V_HEAD_DIM = 64
Q_LORA_RANK = 384
KV_LORA_RANK = 256
QK_DIM = QK_NOPE_DIM + QK_ROPE_DIM
ROPE_BASE = 10000.0
N_GROUPS = 4
EXPERTS_PER_GROUP = 8
N_EXPERTS = N_GROUPS * EXPERTS_PER_GROUP
D_EXPERT = 256
EPS = 1e-6

LANES = 128
HEAD_PAD = N_HEADS * LANES
ROPE_HALF = QK_ROPE_DIM // 2
ONES_ROW = V_HEAD_DIM
V_ROWS = 80
FFT_N2 = 128
ROUTER_LANES = LANES
EXPERT_LANE0 = N_GROUPS
GROUP_LANE = 0
SUBLANES = 8
ATTN_Q_TILE = 4096
HEADS_PER_STEP = LANES // V_HEAD_DIM
IN_PROJ_TILE = 1024
MERGE_TILE = 1024
MOE_TILE = 1024
MOE_CHUNK = 256
MOE_TAILS = (256, 320, 512)
VMEM_LIMIT = 56 * 1024 * 1024

BF16 = jnp.bfloat16
F32 = jnp.float32


def _dot(a, b):
    return jnp.dot(a, b, preferred_element_type=F32)


def _rms(x, g):
    return x * lax.rsqrt(jnp.mean(x * x, axis=-1, keepdims=True) + EPS) * g


def _params(*sem):
    return pltpu.CompilerParams(dimension_semantics=sem, vmem_limit_bytes=VMEM_LIMIT)


def _full(shape):
    return pl.BlockSpec(shape, lambda *_: (0,) * len(shape))


def _in_proj_kernel(x_ref, g_ref, wf_ref, wql_ref, wkvl_ref, wkr_ref, wgate_ref, qg_ref, kvg_ref,
                    wuqt_ref, wuk_ref, wuvt_ref, ekr_ref, cos_ref, sin1_ref, sin2_ref, cost_ref, sin1t_ref, sin2t_ref,
                    u_ref, k_ref, gate_ref, qt_ref, vt_ref, *, q_scale):
    xb = _rms(x_ref[...], g_ref[...]).astype(BF16)
    u_ref[...] = _dot(xb, wf_ref[...]).astype(BF16)
    qn = _rms(_dot(xb, wql_ref[...]), qg_ref[...]).astype(BF16)
    kvn = _rms(_dot(xb, wkvl_ref[...]), kvg_ref[...]).astype(BF16)
    ukr = _dot(xb, wkr_ref[...]).astype(BF16)
    k = _dot(kvn, wuk_ref[...]) + _dot(ukr, ekr_ref[...])
    nt = (((1,), (1,)), ((), ()))
    qt = lax.dot_general(wuqt_ref[...], qn, nt, preferred_element_type=F32)
    vt = lax.dot_general(wuvt_ref[...], kvn, nt, preferred_element_type=F32)
    head_row = lax.broadcasted_iota(jnp.int32, vt.shape, 0) % V_ROWS
    vt = jnp.where(head_row == ONES_ROW, 1.0, vt).astype(BF16)
    tk = vt_ref.shape[-1]
    for c in range(vt_ref.shape[1]):
        vt_ref[0, c] = vt[:, c * tk:(c + 1) * tk]
    cos, sin1, sin2 = cos_ref[...], sin1_ref[...], sin2_ref[...]
    cost, sin1t, sin2t = cost_ref[...], sin1t_ref[...], sin2t_ref[...]
    k_up = pltpu.roll(k, HEAD_PAD - ROPE_HALF, 1)
    k_dn = pltpu.roll(k, ROPE_HALF, 1)
    qt_up = jnp.concatenate([qt[ROPE_HALF:], qt[:ROPE_HALF]], axis=0)
    qt_dn = jnp.concatenate([qt[-ROPE_HALF:], qt[:-ROPE_HALF]], axis=0)
    for h in range(N_HEADS):
        sl = slice(h * LANES, (h + 1) * LANES)
        k_ref[:, sl] = (k[:, sl] * cos + k_up[:, sl] * sin1 + k_dn[:, sl] * sin2).astype(BF16)
        qt_ref[0, sl, :] = ((qt[sl] * cost + qt_up[sl] * sin1t + qt_dn[sl] * sin2t) * q_scale).astype(BF16)
    gate_ref[...] = jax.nn.sigmoid(_dot(xb, wgate_ref[...])).astype(BF16)


def _in_proj(x2d, seq, lw, rope):
    t = x2d.shape[0]
    tk = _key_chunk(seq)
    tm = max(tk, min(IN_PROJ_TILE, seq))
    n_seq_tiles = seq // tm
    row = lambda w: pl.BlockSpec((tm, w), lambda i: (i, 0))
    pos = pl.BlockSpec((tm, LANES), lambda i: (i % n_seq_tiles, 0))
    pos_t = pl.BlockSpec((LANES, tm), lambda i: (0, i % n_seq_tiles))
    weights = [lw["attn_g"], lw["wf"], lw["wql"], lw["wkvl"], lw["wkr"], lw["wgate"], lw["qg"], lw["kvg"],
               lw["wuqt"], lw["wuk"], lw["wuvt"], lw["ekr"]]
    out_w = [FOURIER_DIM, HEAD_PAD, 2 * D_MODEL]
    qt_spec = pl.BlockSpec((1, HEAD_PAD, tm), lambda i: (i // n_seq_tiles, 0, i % n_seq_tiles))
    qt_shape = jax.ShapeDtypeStruct((t // seq, HEAD_PAD, seq), BF16)
    vt_spec = pl.BlockSpec((1, tm // tk, N_HEADS * V_ROWS, tk), lambda i: (i // n_seq_tiles, i % n_seq_tiles, 0, 0))
    vt_shape = jax.ShapeDtypeStruct((t // seq, seq // tk, N_HEADS * V_ROWS, tk), BF16)
    return pl.pallas_call(
        functools.partial(_in_proj_kernel, q_scale=(QK_DIM ** -0.5) * math.log2(math.e)),
        grid=(t // tm,),
        in_specs=[row(D_MODEL)] + [_full(w.shape) for w in weights] + [pos] * 3 + [pos_t] * 3,
        out_specs=[row(w) for w in out_w] + [qt_spec, vt_spec],
        out_shape=[jax.ShapeDtypeStruct((t, w), BF16) for w in out_w] + [qt_shape, vt_shape],
        compiler_params=_params("parallel"),
        name="in_proj",
    )(x2d, *weights, *rope, *[r.T for r in rope])


def _dft_a_kernel(u_ref, f1_ref, twc_ref, tws_ref, y_ref, *, tj, n1):
    f1 = f1_ref[...]
    for j in range(tj):
        y = _dot(f1, u_ref[0, j])
        yr, yi = y[:n1], y[n1:]
        c = jnp.concatenate([twc_ref[j]] * FOURIER_GROUPS, axis=1)
        s = jnp.concatenate([tws_ref[j]] * FOURIER_GROUPS, axis=1)
        y_ref[0, j, 0] = (yr * c + yi * s).astype(BF16)
        y_ref[0, j, 1] = (yi * c - yr * s).astype(BF16)


def _dft_a(u4, f1, twc, tws):
    bsz, n2, n1, _ = u4.shape
    tj = 16
    return pl.pallas_call(
        functools.partial(_dft_a_kernel, tj=tj, n1=n1),
        grid=(bsz, n2 // tj),
        in_specs=[pl.BlockSpec((1, tj, n1, FOURIER_DIM), lambda b, j: (b, j, 0, 0)), _full(f1.shape),
                  pl.BlockSpec((tj, n1, LANES), lambda b, j: (j, 0, 0)),
                  pl.BlockSpec((tj, n1, LANES), lambda b, j: (j, 0, 0))],
        out_specs=pl.BlockSpec((1, tj, 2, n1, FOURIER_DIM), lambda b, j: (b, j, 0, 0, 0)),
        out_shape=jax.ShapeDtypeStruct((bsz, n2, 2, n1, FOURIER_DIM), BF16),
        compiler_params=_params("parallel", "parallel"),
        name="dft_a",
    )(u4, f1, twc, tws)


def _dft_b_kernel(y_ref, m2_ref, wcs_ref, f_ref, x_scr, *, tk1, scale):
    m2 = m2_ref[...]
    for j in range(tk1):
        x = _dot(m2, y_ref[0, j].reshape(2 * FFT_N2, FOURIER_DIM))
        rows = slice(j * FFT_N2, (j + 1) * FFT_N2)
        x_scr[rows, :FOURIER_DIM] = x[:FFT_N2].astype(BF16)
        x_scr[rows, FOURIER_DIM:] = x[FFT_N2:].astype(BF16)
    f = _dot(x_scr[...], wcs_ref[...]) * scale
    f_ref[0] = f.reshape(tk1, FFT_N2, FOURIER_DIM).astype(BF16)


def _dft_b(y5, m2, wcs, seq):
    bsz, n1 = y5.shape[:2]
    tk1 = 8
    return pl.pallas_call(
        functools.partial(_dft_b_kernel, tk1=tk1, scale=(seq * FOURIER_GROUP_DIM) ** -0.5),
        grid=(bsz, n1 // tk1),
        in_specs=[pl.BlockSpec((1, tk1, 2, FFT_N2, FOURIER_DIM), lambda b, j: (b, j, 0, 0, 0)),
                  _full(m2.shape), _full(wcs.shape)],
        out_specs=pl.BlockSpec((1, tk1, FFT_N2, FOURIER_DIM), lambda b, j: (b, j, 0, 0)),
        out_shape=jax.ShapeDtypeStruct((bsz, n1, FFT_N2, FOURIER_DIM), BF16),
        scratch_shapes=[pltpu.VMEM((tk1 * FFT_N2, 2 * FOURIER_DIM), BF16)],
        compiler_params=_params("parallel", "parallel"),
        name="dft_b",
    )(y5, m2, wcs)


def _fourier_mix(u2d, bsz, seq, dft):
    n1 = seq // FFT_N2
    u4 = u2d.reshape(bsz, n1, FFT_N2, FOURIER_DIM).transpose(0, 2, 1, 3)
    y5 = _dft_a(u4, dft["f1"], dft["twc"], dft["tws"])
    f4 = _dft_b(y5.transpose(0, 3, 2, 1, 4), dft["m2"], dft["wcs"], seq)
    return f4.transpose(0, 2, 1, 3).reshape(bsz * seq, FOURIER_DIM)


def _key_chunk(seq):
    return min(512, seq // 2)


def _attn_kernel(q_ref, k_ref, vt_ref, o_ref, m_ref, acc_ref, s_ref, smax_ref, ot_ref, *, tk, nk):
    for hh in range(HEADS_PER_STEP):
        q = q_ref[0, hh * LANES:(hh + 1) * LANES, :]
        m_ref[...] = jnp.full_like(m_ref, -jnp.inf)
        acc_ref[...] = jnp.zeros_like(acc_ref)

        def scores(kc, slot):
            ks = pl.multiple_of(kc * tk, tk)
            st = _dot(k_ref[0, pl.ds(ks, tk), hh * LANES:(hh + 1) * LANES], q)
            s_ref[slot] = st
            smax_ref[slot] = jnp.max(st, axis=0, keepdims=True)

        def softmax_pv(kc, slot):
            m_prev = m_ref[...]
            m_new = jnp.maximum(m_prev, smax_ref[slot])
            alpha = jnp.exp2(m_prev - m_new)
            p = jnp.exp2(s_ref[slot] - m_new).astype(BF16)
            vt = vt_ref[0, kc, hh * V_ROWS:(hh + 1) * V_ROWS, :]
            acc_ref[...] = alpha * acc_ref[...] + _dot(vt, p)
            m_ref[...] = m_new

        scores(0, 0)

        def body(j, carry):
            c = 2 * j
            scores(c + 1, 1)
            softmax_pv(c, 0)
            scores(jnp.minimum(c + 2, nk - 1), 0)
            softmax_pv(c + 1, 1)
            return carry

        lax.fori_loop(0, nk // 2, body, 0)
        acc = acc_ref[...]
        ot_ref[hh * V_HEAD_DIM:(hh + 1) * V_HEAD_DIM, :] = acc[:V_HEAD_DIM] / acc[ONES_ROW:ONES_ROW + 1, :]
    o_ref[0] = ot_ref[...].T.astype(BF16)


def _attention(qt3, k3, vt4):
    bsz, seq, _ = k3.shape
    tq = min(ATTN_Q_TILE, seq)
    tk = _key_chunk(seq)
    nk = seq // tk
    hp = HEADS_PER_STEP
    qt = pl.BlockSpec((1, hp * LANES, tq), lambda b, h, i: (b, h, i))
    qo = pl.BlockSpec((1, tq, hp * V_HEAD_DIM), lambda b, h, i: (b, i, h))
    kk = pl.BlockSpec((1, seq, hp * LANES), lambda b, h, i: (b, 0, h))
    vt = pl.BlockSpec((1, nk, hp * V_ROWS, tk), lambda b, h, i: (b, 0, h, 0))
    return pl.pallas_call(
        functools.partial(_attn_kernel, tk=tk, nk=nk),
        grid=(bsz, N_HEADS // hp, seq // tq),
        in_specs=[qt, kk, vt],
        out_specs=qo,
        out_shape=jax.ShapeDtypeStruct((bsz, seq, N_HEADS * V_HEAD_DIM), BF16),
        scratch_shapes=[pltpu.VMEM((1, tq), F32), pltpu.VMEM((V_ROWS, tq), F32), pltpu.VMEM((2, tk, tq), F32),
                        pltpu.VMEM((2, 1, tq), F32), pltpu.VMEM((hp * V_HEAD_DIM, tq), F32)],
        compiler_params=_params("parallel", "parallel", "arbitrary"),
        name="attn",
    )(qt3, k3, vt4)


def _merge_kernel(x_ref, f_ref, o_ref, gate_ref, wfo_ref, wao_ref, wout_ref, fg_ref, wrh_ref, wrl_ref, br_ref,
                  x1_ref, xn_ref, comb_ref, cnt_ref):
    y_f = _dot(f_ref[...], wfo_ref[...])
    y_a = _dot(o_ref[...], wao_ref[...])
    gate = gate_ref[...].astype(F32)
    merged = gate[:, :D_MODEL] * y_f + gate[:, D_MODEL:] * y_a
    x1 = x_ref[...] + _dot(merged.astype(BF16), wout_ref[...])
    x1_ref[...] = x1
    xn = _rms(x1, fg_ref[...])
    xh = xn.astype(BF16)
    xn_ref[...] = xh
    xl = (xn - xh.astype(F32)).astype(BF16)
    logit = _dot(xh, wrh_ref[...]) + _dot(xl, wrh_ref[...]) + _dot(xh, wrl_ref[...]) + br_ref[...]
    lane = lax.broadcasted_iota(jnp.int32, logit.shape, 1)
    neg = jnp.float32(-jnp.inf)

    def top(mask):
        val = jnp.max(jnp.where(mask, logit, neg), axis=1, keepdims=True)
        idx = jnp.min(jnp.where(mask & (logit == val), lane, ROUTER_LANES), axis=1, keepdims=True)
        return val, idx

    is_grp = lane < N_GROUPS
    g_max, g_idx = top(is_grp)
    g_sum = jnp.sum(jnp.where(is_grp, jnp.exp(logit - g_max), 0.0), axis=1, keepdims=True)
    grp_p = 1.0 / g_sum
    lo = EXPERT_LANE0 + g_idx * EXPERTS_PER_GROUP
    in_grp = (lane >= lo) & (lane < lo + EXPERTS_PER_GROUP)
    l1, i1 = top(in_grp)
    l2, i2 = top(in_grp & (lane != i1))
    e2 = jnp.exp(l2 - l1)
    w1 = 1.0 / (1.0 + e2)
    w2 = e2 / (1.0 + e2)
    comb = jnp.where(lane == i1, grp_p * w1, jnp.where(lane == i2, grp_p * w2, 0.0))
    comb_ref[...] = jnp.where(lane == GROUP_LANE, g_idx.astype(F32), comb)
    cnt = jnp.sum(jnp.where(lane == g_idx, 1.0, 0.0), axis=0, keepdims=True)
    cnt_ref[0] = jnp.broadcast_to(cnt, cnt_ref.shape[1:])


def _merge(x2d, f2d, o2d, gate2d, lw):
    t = x2d.shape[0]
    tm = min(MERGE_TILE, t)
    row = lambda w: pl.BlockSpec((tm, w), lambda i: (i, 0))
    weights = [lw["wfo"], lw["wao"], lw["wout"], lw["ffn_g"], lw["wr_hi"], lw["wr_lo"], lw["br"]]
    return pl.pallas_call(
        _merge_kernel,
        grid=(t // tm,),
        in_specs=[row(D_MODEL), row(FOURIER_DIM), row(N_HEADS * V_HEAD_DIM), row(2 * D_MODEL)]
        + [_full(w.shape) for w in weights],
        out_specs=[row(D_MODEL), row(D_MODEL), row(ROUTER_LANES),
                   pl.BlockSpec((1, SUBLANES, ROUTER_LANES), lambda i: (i, 0, 0))],
        out_shape=[jax.ShapeDtypeStruct((t, D_MODEL), F32), jax.ShapeDtypeStruct((t, D_MODEL), BF16),
                   jax.ShapeDtypeStruct((t, ROUTER_LANES), F32),
                   jax.ShapeDtypeStruct((t // tm, SUBLANES, ROUTER_LANES), F32)],
        compiler_params=_params("parallel"),
        name="merge",
    )(x2d, f2d, o2d, gate2d, *weights)


def _moe_kernel(cnt_ref, x1_ref, xn_ref, comb_ref, wg_ref, wu_ref, wd_ref, fin_ref, out_ref,
                acc_ref, dcol_ref, drow_ref, chi_ref, clo_ref, *, final_norm):
    i, g = pl.program_id(0), pl.program_id(1)
    tm = xn_ref.shape[0]

    def schedule(gg):
        n = cnt_ref[i * N_GROUPS + gg]
        full, rem = n // MOE_CHUNK, n % MOE_CHUNK
        plain = jnp.where(rem > 0, jnp.maximum(full - 1, 0), full)
        need = jnp.where(rem > 0, jnp.where(full > 0, MOE_CHUNK + rem, rem), 0)
        tail = jnp.int32(0)
        for size in reversed(MOE_TAILS):
            tail = jnp.where(need <= size, size, tail)
        return plain, jnp.where(need > 0, tail, 0)

    def rows(gg):
        plain, tail = schedule(gg)
        return plain * MOE_CHUNK + tail

    @pl.when(g == 0)
    def _():
        comb = comb_ref[...]
        lane = lax.broadcasted_iota(jnp.int32, comb.shape, 1)
        grp = comb[:, GROUP_LANE:GROUP_LANE + 1].astype(jnp.int32)
        onehot = jnp.where((lane == grp) & (lane < N_GROUPS), 1.0, 0.0)
        earlier = lax.broadcasted_iota(jnp.int32, (tm, tm), 0) > lax.broadcasted_iota(jnp.int32, (tm, tm), 1)
        before = _dot(jnp.where(earlier, 1.0, 0.0).astype(BF16), onehot.astype(BF16))
        dest = jnp.sum(onehot * before, axis=1, keepdims=True)
        first_row = jnp.int32(0)
        for gg in range(N_GROUPS):
            dest = dest + jnp.where(grp == gg, first_row.astype(F32), 0.0)
            first_row = first_row + rows(gg)
        dcol = jnp.broadcast_to(dest, comb.shape)
        dcol_ref[...] = dcol
        drow_ref[...] = dcol.T[:SUBLANES, :]
        w = jnp.where(lane == GROUP_LANE, 0.0, comb)
        chi = w.astype(BF16)
        chi_ref[...] = chi
        clo_ref[...] = (w - chi.astype(F32)).astype(BF16)
        acc_ref[...] = jnp.zeros_like(acc_ref)

    first_row = jnp.int32(0)
    for gg in range(N_GROUPS):
        first_row = first_row + jnp.where(gg < g, rows(gg), 0)
    plain, tail = schedule(g)

    def chunk(start, size):
        row0 = start.astype(F32)
        slot = lax.broadcasted_iota(jnp.int32, (size, tm), 0).astype(F32) + row0
        gather = jnp.where(drow_ref[0:1, :] == slot, 1.0, 0.0).astype(BF16)
        xs = _dot(gather, xn_ref[...]).astype(BF16)
        cs = _dot(gather, chi_ref[...]) + _dot(gather, clo_ref[...])
        lane = lax.broadcasted_iota(jnp.int32, cs.shape, 1)
        y = None
        for e in range(EXPERTS_PER_GROUP):
            c_e = jnp.sum(jnp.where(lane == EXPERT_LANE0 + g * EXPERTS_PER_GROUP + e, cs, 0.0), axis=1, keepdims=True)
            hg = _dot(xs, wg_ref[e])
            hu = _dot(xs, wu_ref[e])
            hs = (hg * jax.nn.sigmoid(hg) * hu * c_e).astype(BF16)
            ye = _dot(hs, wd_ref[e])
            y = ye if y is None else y + ye
        slot_t = lax.broadcasted_iota(jnp.int32, (tm, size), 1).astype(F32) + row0
        scatter = jnp.where(dcol_ref[:, 0:1] == slot_t, 1.0, 0.0).astype(BF16)
        acc_ref[...] += _dot(scatter, y.astype(BF16))

    def plain_chunk(c, carry):
        chunk(first_row + c * MOE_CHUNK, MOE_CHUNK)
        return carry

    lax.fori_loop(0, plain, plain_chunk, 0)
    for size in MOE_TAILS:
        @pl.when(tail == size)
        def _(size=size):
            chunk(first_row + plain * MOE_CHUNK, size)

    @pl.when(g == pl.num_programs(1) - 1)
    def _():
        x2 = x1_ref[...] + acc_ref[...]
        out_ref[...] = _rms(x2, fin_ref[...]) if final_norm else x2


def _moe(x1, xn, comb, cnt, lw, final_g, final_norm):
    t = x1.shape[0]
    tm = min(MOE_TILE, t)
    per = tm // min(MERGE_TILE, t)
    counts = cnt[:, 0, :N_GROUPS].reshape(t // tm, per, N_GROUPS).sum(axis=1).astype(jnp.int32).reshape(-1)
    row = lambda w: pl.BlockSpec((tm, w), lambda i, g, c: (i, 0))
    wspec = lambda a, b: pl.BlockSpec((EXPERTS_PER_GROUP, a, b), lambda i, g, c: (g, 0, 0))
    grid_spec = pltpu.PrefetchScalarGridSpec(
        num_scalar_prefetch=1,
        grid=(t // tm, N_GROUPS),
        in_specs=[row(D_MODEL), row(D_MODEL), row(ROUTER_LANES), wspec(D_MODEL, D_EXPERT), wspec(D_MODEL, D_EXPERT),
                  wspec(D_EXPERT, D_MODEL), pl.BlockSpec((1, D_MODEL), lambda i, g, c: (0, 0))],
        out_specs=row(D_MODEL),
        scratch_shapes=[pltpu.VMEM((tm, D_MODEL), F32), pltpu.VMEM((tm, ROUTER_LANES), F32),
                        pltpu.VMEM((SUBLANES, tm), F32), pltpu.VMEM((tm, ROUTER_LANES), BF16),
                        pltpu.VMEM((tm, ROUTER_LANES), BF16)],
    )
    return pl.pallas_call(
        functools.partial(_moe_kernel, final_norm=final_norm),
        grid_spec=grid_spec,
        out_shape=jax.ShapeDtypeStruct((t, D_MODEL), F32),
        compiler_params=_params("parallel", "arbitrary"),
        name="moe",
    )(counts, x1, xn, comb, lw["wg"], lw["wu"], lw["wd"], final_g)


def _dft_angles(n):
    return 2.0 * np.pi * ((np.arange(n)[:, None] * np.arange(n)[None, :]) % n) / n


def _dft_tables(seq):
    n1, n2, n = seq // FFT_N2, FFT_N2, FOURIER_GROUP_DIM
    a1, a2, ac = _dft_angles(n1), _dft_angles(n2), _dft_angles(n)
    f1 = np.concatenate([np.cos(a1), -np.sin(a1)], axis=0)
    m2 = np.block([[np.cos(a2), np.sin(a2)], [-np.sin(a2), np.cos(a2)]])
    wcs = np.zeros((2 * FOURIER_DIM, FOURIER_DIM), np.float64)
    for g in range(FOURIER_GROUPS):
        sl = slice(g * n, (g + 1) * n)
        wcs[sl, sl] = np.cos(ac)
        wcs[FOURIER_DIM + g * n:FOURIER_DIM + (g + 1) * n, sl] = np.sin(ac)
    idx = (jnp.arange(n2, dtype=jnp.int32)[:, None] * jnp.arange(n1, dtype=jnp.int32)[None, :]) % seq
    ang = jnp.broadcast_to((idx.astype(F32) * (2.0 * math.pi / seq))[:, :, None], (n2, n1, LANES))
    return {"f1": jnp.asarray(f1, BF16), "m2": jnp.asarray(m2, BF16), "wcs": jnp.asarray(wcs, BF16),
            "twc": jnp.cos(ang), "tws": jnp.sin(ang)}


def _rope_key_placement():
    ekr = np.zeros((LANES, HEAD_PAD), np.float32)
    for h in range(N_HEADS):
        for r in range(QK_ROPE_DIM):
            ekr[r, h * LANES + QK_NOPE_DIM + r] = 1.0
    return jnp.asarray(ekr, BF16)


def _rope_tables(seq):
    inv = 1.0 / (ROPE_BASE ** (jnp.arange(0, QK_ROPE_DIM, 2, dtype=F32) / QK_ROPE_DIM))
    rang = jnp.arange(seq, dtype=F32)[:, None] * inv[None, :]
    c, s = jnp.cos(rang), jnp.sin(rang)
    z = lambda w: jnp.zeros((seq, w), F32)
    tail = LANES - QK_DIM
    cos = jnp.concatenate([jnp.ones((seq, QK_NOPE_DIM), F32), c, c, z(tail)], axis=1)
    sin1 = jnp.concatenate([z(QK_NOPE_DIM), -s, z(ROPE_HALF), z(tail)], axis=1)
    sin2 = jnp.concatenate([z(QK_NOPE_DIM), z(ROPE_HALF), s, z(tail)], axis=1)
    return cos, sin1, sin2


def _pad_heads(w, per_head, lo, hi, width=LANES):
    r = w.shape[0]
    w3 = w.reshape(r, N_HEADS, per_head)[:, :, lo:hi]
    return jnp.pad(w3, ((0, 0), (0, 0), (0, width - (hi - lo)))).reshape(r, N_HEADS * width)


def _layer_weights(l, attn_norm_g, w_in, q_norm_g, kv_norm_g, w_uq, w_ukv, w_fourier_out, w_attn_out, w_out,
                   ffn_norm_g, w_grp, b_grp, w_exp, b_exp, w_gate, w_up, w_down):
    s0, s1, s2, s3 = (FOURIER_DIM, FOURIER_DIM + Q_LORA_RANK, FOURIER_DIM + Q_LORA_RANK + KV_LORA_RANK,
                      FOURIER_DIM + Q_LORA_RANK + KV_LORA_RANK + QK_ROPE_DIM)
    wi = w_in[l]
    wr = jnp.concatenate([w_grp[l], w_exp[l]], axis=1)
    wr = jnp.pad(wr, ((0, 0), (0, ROUTER_LANES - wr.shape[1])))
    wr_hi = wr.astype(BF16)
    br = jnp.concatenate([b_grp[l], b_exp[l]])
    return {
        "attn_g": attn_norm_g[l][None, :],
        "wf": wi[:, :s0].astype(BF16),
        "wql": wi[:, s0:s1].astype(BF16),
        "wkvl": wi[:, s1:s2].astype(BF16),
        "wkr": jnp.pad(wi[:, s2:s3], ((0, 0), (0, LANES - QK_ROPE_DIM))).astype(BF16),
        "wgate": wi[:, s3:].astype(BF16),
        "qg": q_norm_g[l][None, :],
        "kvg": kv_norm_g[l][None, :],
        "wuqt": _pad_heads(w_uq[l], QK_DIM, 0, QK_DIM).T.astype(BF16),
        "wuk": _pad_heads(w_ukv[l], QK_NOPE_DIM + V_HEAD_DIM, 0, QK_NOPE_DIM).astype(BF16),
        "wuvt": _pad_heads(w_ukv[l], QK_NOPE_DIM + V_HEAD_DIM, QK_NOPE_DIM, QK_NOPE_DIM + V_HEAD_DIM,
                           V_ROWS).T.astype(BF16),
        "ekr": _rope_key_placement(),
        "wfo": w_fourier_out[l].astype(BF16),
        "wao": w_attn_out[l].astype(BF16),
        "wout": w_out[l].astype(BF16),
        "ffn_g": ffn_norm_g[l][None, :],
        "wr_hi": wr_hi,
        "wr_lo": (wr - wr_hi.astype(F32)).astype(BF16),
        "br": jnp.pad(br, (0, ROUTER_LANES - br.shape[0]))[None, :],
        "wg": w_gate[l].astype(BF16),
        "wu": w_up[l].astype(BF16),
        "wd": w_down[l].astype(BF16),
    }


def _trunk(x, layers, final_g, tables):
    bsz, seq, d = x.shape
    t = bsz * seq
    dft, rope = tables
    x2d = x.reshape(t, d)
    for l, lw in enumerate(layers):
        u, k, gate, qt, vt = _in_proj(x2d, seq, lw, rope)
        shp = lambda z: z.reshape(bsz, seq, z.shape[-1])
        f = _fourier_mix(u, bsz, seq, dft)
        o = _attention(qt, shp(k), vt).reshape(t, N_HEADS * V_HEAD_DIM)
        x1, xn, comb, cnt = _merge(x2d, f, o, gate, lw)
        x2d = _moe(x1, xn, comb, cnt, lw, final_g, final_norm=(l == len(layers) - 1))
    return x2d.reshape(bsz, seq, d)


def kernel(x_prompt, x_sample, attn_norm_g, w_in, q_norm_g, kv_norm_g, w_uq, w_ukv, w_fourier_out, w_attn_out, w_out, ffn_norm_g, w_grp, b_grp, w_exp, b_exp, w_gate, w_up, w_down, final_norm_g):
    depth = w_in.shape[0]
    layers = [_layer_weights(l, attn_norm_g, w_in, q_norm_g, kv_norm_g, w_uq, w_ukv, w_fourier_out, w_attn_out,
                             w_out, ffn_norm_g, w_grp, b_grp, w_exp, b_exp, w_gate, w_up, w_down)
              for l in range(depth)]
    final_g = final_norm_g[None, :]
    outs = []
    for x in (x_prompt, x_sample):
        seq = x.shape[1]
        outs.append(_trunk(x, layers, final_g, (_dft_tables(seq), _rope_tables(seq))))
    return tuple(outs)
```

```python
import functools
import math

import numpy as np
import jax
import jax.numpy as jnp
from jax import lax
from jax.experimental import pallas as pl
from jax.experimental.pallas import tpu as pltpu

D_MODEL = 1024
FOURIER_GROUPS = 4
FOURIER_GROUP_DIM = 128
FOURIER_DIM = FOURIER_GROUPS * FOURIER_GROUP_DIM
N_HEADS = 8
QK_NOPE_DIM = 64
QK_ROPE_DIM = 32
V_HEAD_DIM = 64
Q_LORA_RANK = 384
KV_LORA_RANK = 256
QK_DIM = QK_NOPE_DIM + QK_ROPE_DIM
ROPE_BASE = 10000.0
N_GROUPS = 4
EXPERTS_PER_GROUP = 8
N_EXPERTS = N_GROUPS * EXPERTS_PER_GROUP
D_EXPERT = 256
EPS = 1e-6

LANES = 128
HEAD_PAD = N_HEADS * LANES
ROPE_HALF = QK_ROPE_DIM // 2
ONES_ROW = V_HEAD_DIM
V_ROWS = 80
FFT_N2 = 128
ROUTER_LANES = LANES
EXPERT_LANE0 = N_GROUPS
GROUP_LANE = 0
SUBLANES = 8
ATTN_Q_TILE = 4096
ATTN_STRIP = 256
ATTN_KEY_CHUNK = 512
HEADS_PER_STEP = LANES // V_HEAD_DIM
IN_PROJ_TILE = 1024
MERGE_TILE = 1024
MOE_TILE = 1024
MOE_CHUNK = 256
MOE_TAILS = (256, 320, 512)
VMEM_LIMIT = 56 * 1024 * 1024

BF16 = jnp.bfloat16
F32 = jnp.float32


def _dot(a, b):
    return jnp.dot(a, b, preferred_element_type=F32)


def _rms(x, g):
    return x * lax.rsqrt(jnp.mean(x * x, axis=-1, keepdims=True) + EPS) * g


def _params(*sem):
    return pltpu.CompilerParams(dimension_semantics=sem, vmem_limit_bytes=VMEM_LIMIT)


def _full(shape):
    return pl.BlockSpec(shape, lambda *_: (0,) * len(shape))


def _in_proj_kernel(x_ref, g_ref, wf_ref, wql_ref, wkvl_ref, wkr_ref, wgate_ref, qg_ref, kvg_ref,
                    wuqt_ref, wuk_ref, wuvt_ref, ekr_ref, cos_ref, sin1_ref, sin2_ref, cost_ref, sin1t_ref, sin2t_ref,
                    u_ref, k_ref, gate_ref, qt_ref, vt_ref, *, q_scale):
    xb = _rms(x_ref[...], g_ref[...]).astype(BF16)
    u_ref[...] = _dot(xb, wf_ref[...]).astype(BF16)
    qn = _rms(_dot(xb, wql_ref[...]), qg_ref[...]).astype(BF16)
    kvn = _rms(_dot(xb, wkvl_ref[...]), kvg_ref[...]).astype(BF16)
    ukr = _dot(xb, wkr_ref[...]).astype(BF16)
    k = _dot(kvn, wuk_ref[...]) + _dot(ukr, ekr_ref[...])
    nt = (((1,), (1,)), ((), ()))
    qt = lax.dot_general(wuqt_ref[...], qn, nt, preferred_element_type=F32)
    vt = lax.dot_general(wuvt_ref[...], kvn, nt, preferred_element_type=F32)
    head_row = lax.broadcasted_iota(jnp.int32, vt.shape, 0) % V_ROWS
    vt = jnp.where(head_row == ONES_ROW, 1.0, vt).astype(BF16)
    tk = vt_ref.shape[-1]
    for c in range(vt_ref.shape[1]):
        vt_ref[0, c] = vt[:, c * tk:(c + 1) * tk]
    cos, sin1, sin2 = cos_ref[...], sin1_ref[...], sin2_ref[...]
    cost, sin1t, sin2t = cost_ref[...], sin1t_ref[...], sin2t_ref[...]
    k_up = pltpu.roll(k, HEAD_PAD - ROPE_HALF, 1)
    k_dn = pltpu.roll(k, ROPE_HALF, 1)
    qt_up = jnp.concatenate([qt[ROPE_HALF:], qt[:ROPE_HALF]], axis=0)
    qt_dn = jnp.concatenate([qt[-ROPE_HALF:], qt[:-ROPE_HALF]], axis=0)
    for h in range(N_HEADS):
        sl = slice(h * LANES, (h + 1) * LANES)
        k_ref[:, sl] = (k[:, sl] * cos + k_up[:, sl] * sin1 + k_dn[:, sl] * sin2).astype(BF16)
        qt_ref[0, sl, :] = ((qt[sl] * cost + qt_up[sl] * sin1t + qt_dn[sl] * sin2t) * q_scale).astype(BF16)
    gate_ref[...] = jax.nn.sigmoid(_dot(xb, wgate_ref[...])).astype(BF16)


def _in_proj(x2d, seq, lw, rope):
    t = x2d.shape[0]
    tk = _key_chunk(seq)
    tm = max(tk, min(IN_PROJ_TILE, seq))
    n_seq_tiles = seq // tm
    row = lambda w: pl.BlockSpec((tm, w), lambda i: (i, 0))
    pos = pl.BlockSpec((tm, LANES), lambda i: (i % n_seq_tiles, 0))
    pos_t = pl.BlockSpec((LANES, tm), lambda i: (0, i % n_seq_tiles))
    weights = [lw["attn_g"], lw["wf"], lw["wql"], lw["wkvl"], lw["wkr"], lw["wgate"], lw["qg"], lw["kvg"],
               lw["wuqt"], lw["wuk"], lw["wuvt"], lw["ekr"]]
    out_w = [FOURIER_DIM, HEAD_PAD, 2 * D_MODEL]
    qt_spec = pl.BlockSpec((1, HEAD_PAD, tm), lambda i: (i // n_seq_tiles, 0, i % n_seq_tiles))
    qt_shape = jax.ShapeDtypeStruct((t // seq, HEAD_PAD, seq), BF16)
    vt_spec = pl.BlockSpec((1, tm // tk, N_HEADS * V_ROWS, tk), lambda i: (i // n_seq_tiles, i % n_seq_tiles, 0, 0))
    vt_shape = jax.ShapeDtypeStruct((t // seq, seq // tk, N_HEADS * V_ROWS, tk), BF16)
    return pl.pallas_call(
        functools.partial(_in_proj_kernel, q_scale=(QK_DIM ** -0.5) * math.log2(math.e)),
        grid=(t // tm,),
        in_specs=[row(D_MODEL)] + [_full(w.shape) for w in weights] + [pos] * 3 + [pos_t] * 3,
        out_specs=[row(w) for w in out_w] + [qt_spec, vt_spec],
        out_shape=[jax.ShapeDtypeStruct((t, w), BF16) for w in out_w] + [qt_shape, vt_shape],
        compiler_params=_params("parallel"),
        name="in_proj",
    )(x2d, *weights, *rope, *[r.T for r in rope])


def _dft_a_kernel(u_ref, f1_ref, twc_ref, tws_ref, y_ref, *, tj, n1):
    f1 = f1_ref[...]
    for j in range(tj):
        y = _dot(f1, u_ref[0, j])
        yr, yi = y[:n1], y[n1:]
        c = jnp.concatenate([twc_ref[j]] * FOURIER_GROUPS, axis=1)
        s = jnp.concatenate([tws_ref[j]] * FOURIER_GROUPS, axis=1)
        y_ref[0, j, 0] = (yr * c + yi * s).astype(BF16)
        y_ref[0, j, 1] = (yi * c - yr * s).astype(BF16)


def _dft_a(u4, f1, twc, tws):
    bsz, n2, n1, _ = u4.shape
    tj = 16
    return pl.pallas_call(
        functools.partial(_dft_a_kernel, tj=tj, n1=n1),
        grid=(bsz, n2 // tj),
        in_specs=[pl.BlockSpec((1, tj, n1, FOURIER_DIM), lambda b, j: (b, j, 0, 0)), _full(f1.shape),
                  pl.BlockSpec((tj, n1, LANES), lambda b, j: (j, 0, 0)),
                  pl.BlockSpec((tj, n1, LANES), lambda b, j: (j, 0, 0))],
        out_specs=pl.BlockSpec((1, tj, 2, n1, FOURIER_DIM), lambda b, j: (b, j, 0, 0, 0)),
        out_shape=jax.ShapeDtypeStruct((bsz, n2, 2, n1, FOURIER_DIM), BF16),
        compiler_params=_params("parallel", "parallel"),
        name="dft_a",
    )(u4, f1, twc, tws)


def _dft_b_kernel(y_ref, m2_ref, wcs_ref, f_ref, x_scr, *, tk1, scale):
    m2 = m2_ref[...]
    for j in range(tk1):
        x = _dot(m2, y_ref[0, j].reshape(2 * FFT_N2, FOURIER_DIM))
        rows = slice(j * FFT_N2, (j + 1) * FFT_N2)
        x_scr[rows, :FOURIER_DIM] = x[:FFT_N2].astype(BF16)
        x_scr[rows, FOURIER_DIM:] = x[FFT_N2:].astype(BF16)
    f = _dot(x_scr[...], wcs_ref[...]) * scale
    f_ref[0] = f.reshape(tk1, FFT_N2, FOURIER_DIM).astype(BF16)


def _dft_b(y5, m2, wcs, seq):
    bsz, n1 = y5.shape[:2]
    tk1 = 8
    return pl.pallas_call(
        functools.partial(_dft_b_kernel, tk1=tk1, scale=(seq * FOURIER_GROUP_DIM) ** -0.5),
        grid=(bsz, n1 // tk1),
        in_specs=[pl.BlockSpec((1, tk1, 2, FFT_N2, FOURIER_DIM), lambda b, j: (b, j, 0, 0, 0)),
                  _full(m2.shape), _full(wcs.shape)],
        out_specs=pl.BlockSpec((1, tk1, FFT_N2, FOURIER_DIM), lambda b, j: (b, j, 0, 0)),
        out_shape=jax.ShapeDtypeStruct((bsz, n1, FFT_N2, FOURIER_DIM), BF16),
        scratch_shapes=[pltpu.VMEM((tk1 * FFT_N2, 2 * FOURIER_DIM), BF16)],
        compiler_params=_params("parallel", "parallel"),
        name="dft_b",
    )(y5, m2, wcs)


def _fourier_mix(u2d, bsz, seq, dft):
    n1 = seq // FFT_N2
    u4 = u2d.reshape(bsz, n1, FFT_N2, FOURIER_DIM).transpose(0, 2, 1, 3)
    y5 = _dft_a(u4, dft["f1"], dft["twc"], dft["tws"])
    f4 = _dft_b(y5.transpose(0, 3, 2, 1, 4), dft["m2"], dft["wcs"], seq)
    return f4.transpose(0, 2, 1, 3).reshape(bsz * seq, FOURIER_DIM)


def _key_chunk(seq):
    return min(ATTN_KEY_CHUNK, seq // 2)


def _attn_kernel(q_ref, k_ref, vt_ref, o_ref, m_ref, acc_ref, s_ref, smax_ref, ot_ref, *, tk, nk):
    tq = q_ref.shape[2]
    strips = [slice(j * ATTN_STRIP, (j + 1) * ATTN_STRIP) for j in range(tq // ATTN_STRIP)]
    for hh in range(HEADS_PER_STEP):
        head = slice(hh * LANES, (hh + 1) * LANES)
        m_ref[...] = jnp.full_like(m_ref, -jnp.inf)
        acc_ref[...] = jnp.zeros_like(acc_ref)

        def scores(kc, slot, sl):
            ks = pl.multiple_of(kc * tk, tk)
            st = _dot(k_ref[0, pl.ds(ks, tk), head], q_ref[0, head, sl])
            s_ref[slot, :, sl] = st
            smax_ref[slot, :, sl] = jnp.max(st, axis=0, keepdims=True)

        def softmax_pv(kc, slot, sl):
            m_prev = m_ref[:, sl]
            m_new = jnp.maximum(m_prev, smax_ref[slot, :, sl])
            alpha = jnp.exp2(m_prev - m_new)
            p = jnp.exp2(s_ref[slot, :, sl] - m_new).astype(BF16)
            vt = vt_ref[0, kc, hh * V_ROWS:(hh + 1) * V_ROWS, :]
            acc_ref[:, sl] = alpha * acc_ref[:, sl] + _dot(vt, p)
            m_ref[:, sl] = m_new

        for sl in strips:
            scores(0, 0, sl)

        def body(j, carry):
            c = 2 * j
            for sl in strips:
                scores(c + 1, 1, sl)
                softmax_pv(c, 0, sl)
            for sl in strips:
                scores(jnp.minimum(c + 2, nk - 1), 0, sl)
                softmax_pv(c + 1, 1, sl)
            return carry

        lax.fori_loop(0, nk // 2, body, 0)
        acc = acc_ref[...]
        ot_ref[hh * V_HEAD_DIM:(hh + 1) * V_HEAD_DIM, :] = acc[:V_HEAD_DIM] / acc[ONES_ROW:ONES_ROW + 1, :]
    o_ref[0] = ot_ref[...].T.astype(BF16)


def _attention(qt3, k3, vt4):
    bsz, seq, _ = k3.shape
    tq = min(ATTN_Q_TILE, seq)
    tk = _key_chunk(seq)
    nk = seq // tk
    hp = HEADS_PER_STEP
    qt = pl.BlockSpec((1, hp * LANES, tq), lambda b, h, i: (b, h, i))
    qo = pl.BlockSpec((1, tq, hp * V_HEAD_DIM), lambda b, h, i: (b, i, h))
    kk = pl.BlockSpec((1, seq, hp * LANES), lambda b, h, i: (b, 0, h))
    vt = pl.BlockSpec((1, nk, hp * V_ROWS, tk), lambda b, h, i: (b, 0, h, 0))
    return pl.pallas_call(
        functools.partial(_attn_kernel, tk=tk, nk=nk),
        grid=(bsz, N_HEADS // hp, seq // tq),
        in_specs=[qt, kk, vt],
        out_specs=qo,
        out_shape=jax.ShapeDtypeStruct((bsz, seq, N_HEADS * V_HEAD_DIM), BF16),
        scratch_shapes=[pltpu.VMEM((1, tq), F32), pltpu.VMEM((V_ROWS, tq), F32), pltpu.VMEM((2, tk, tq), F32),
                        pltpu.VMEM((2, 1, tq), F32), pltpu.VMEM((hp * V_HEAD_DIM, tq), F32)],
        compiler_params=_params("parallel", "parallel", "arbitrary"),
        name="attn",
    )(qt3, k3, vt4)


def _merge_kernel(x_ref, f_ref, o_ref, gate_ref, wfo_ref, wao_ref, wout_ref, fg_ref, wrh_ref, wrl_ref, br_ref,
                  x1_ref, xn_ref, comb_ref, cnt_ref):
    y_f = _dot(f_ref[...], wfo_ref[...])
    y_a = _dot(o_ref[...], wao_ref[...])
    gate = gate_ref[...].astype(F32)
    merged = gate[:, :D_MODEL] * y_f + gate[:, D_MODEL:] * y_a
    x1 = x_ref[...] + _dot(merged.astype(BF16), wout_ref[...])
    x1_ref[...] = x1
    xn = _rms(x1, fg_ref[...])
    xh = xn.astype(BF16)
    xn_ref[...] = xh
    xl = (xn - xh.astype(F32)).astype(BF16)
    logit = _dot(xh, wrh_ref[...]) + _dot(xl, wrh_ref[...]) + _dot(xh, wrl_ref[...]) + br_ref[...]
    lane = lax.broadcasted_iota(jnp.int32, logit.shape, 1)
    neg = jnp.float32(-jnp.inf)

    def top(mask):
        val = jnp.max(jnp.where(mask, logit, neg), axis=1, keepdims=True)
        idx = jnp.min(jnp.where(mask & (logit == val), lane, ROUTER_LANES), axis=1, keepdims=True)
        return val, idx

    is_grp = lane < N_GROUPS
    g_max, g_idx = top(is_grp)
    g_sum = jnp.sum(jnp.where(is_grp, jnp.exp(logit - g_max), 0.0), axis=1, keepdims=True)
    grp_p = 1.0 / g_sum
    lo = EXPERT_LANE0 + g_idx * EXPERTS_PER_GROUP
    in_grp = (lane >= lo) & (lane < lo + EXPERTS_PER_GROUP)
    l1, i1 = top(in_grp)
    l2, i2 = top(in_grp & (lane != i1))
    e2 = jnp.exp(l2 - l1)
    w1 = 1.0 / (1.0 + e2)
    w2 = e2 / (1.0 + e2)
    comb = jnp.where(lane == i1, grp_p * w1, jnp.where(lane == i2, grp_p * w2, 0.0))
    comb_ref[...] = jnp.where(lane == GROUP_LANE, g_idx.astype(F32), comb)
    cnt = jnp.sum(jnp.where(lane == g_idx, 1.0, 0.0), axis=0, keepdims=True)
    cnt_ref[0] = jnp.broadcast_to(cnt, cnt_ref.shape[1:])


def _merge(x2d, f2d, o2d, gate2d, lw):
    t = x2d.shape[0]
    tm = min(MERGE_TILE, t)
    row = lambda w: pl.BlockSpec((tm, w), lambda i: (i, 0))
    weights = [lw["wfo"], lw["wao"], lw["wout"], lw["ffn_g"], lw["wr_hi"], lw["wr_lo"], lw["br"]]
    return pl.pallas_call(
        _merge_kernel,
        grid=(t // tm,),
        in_specs=[row(D_MODEL), row(FOURIER_DIM), row(N_HEADS * V_HEAD_DIM), row(2 * D_MODEL)]
        + [_full(w.shape) for w in weights],
        out_specs=[row(D_MODEL), row(D_MODEL), row(ROUTER_LANES),
                   pl.BlockSpec((1, SUBLANES, ROUTER_LANES), lambda i: (i, 0, 0))],
        out_shape=[jax.ShapeDtypeStruct((t, D_MODEL), F32), jax.ShapeDtypeStruct((t, D_MODEL), BF16),
                   jax.ShapeDtypeStruct((t, ROUTER_LANES), F32),
                   jax.ShapeDtypeStruct((t // tm, SUBLANES, ROUTER_LANES), F32)],
        compiler_params=_params("parallel"),
        name="merge",
    )(x2d, f2d, o2d, gate2d, *weights)


def _moe_kernel(cnt_ref, x1_ref, xn_ref, comb_ref, wg_ref, wu_ref, wd_ref, fin_ref, out_ref,
                acc_ref, dcol_ref, drow_ref, chi_ref, clo_ref, *, final_norm):
    i, g = pl.program_id(0), pl.program_id(1)
    tm = xn_ref.shape[0]

    def schedule(gg):
        n = cnt_ref[i * N_GROUPS + gg]
        full, rem = n // MOE_CHUNK, n % MOE_CHUNK
        plain = jnp.where(rem > 0, jnp.maximum(full - 1, 0), full)
        need = jnp.where(rem > 0, jnp.where(full > 0, MOE_CHUNK + rem, rem), 0)
        tail = jnp.int32(0)
        for size in reversed(MOE_TAILS):
            tail = jnp.where(need <= size, size, tail)
        return plain, jnp.where(need > 0, tail, 0)

    def rows(gg):
        plain, tail = schedule(gg)
        return plain * MOE_CHUNK + tail

    @pl.when(g == 0)
    def _():
        comb = comb_ref[...]
        lane = lax.broadcasted_iota(jnp.int32, comb.shape, 1)
        grp = comb[:, GROUP_LANE:GROUP_LANE + 1].astype(jnp.int32)
        onehot = jnp.where((lane == grp) & (lane < N_GROUPS), 1.0, 0.0)
        earlier = lax.broadcasted_iota(jnp.int32, (tm, tm), 0) > lax.broadcasted_iota(jnp.int32, (tm, tm), 1)
        before = _dot(jnp.where(earlier, 1.0, 0.0).astype(BF16), onehot.astype(BF16))
        dest = jnp.sum(onehot * before, axis=1, keepdims=True)
        first_row = jnp.int32(0)
        for gg in range(N_GROUPS):
            dest = dest + jnp.where(grp == gg, first_row.astype(F32), 0.0)
            first_row = first_row + rows(gg)
        dcol = jnp.broadcast_to(dest, comb.shape)
        dcol_ref[...] = dcol
        drow_ref[...] = dcol.T[:SUBLANES, :]
        w = jnp.where(lane == GROUP_LANE, 0.0, comb)
        chi = w.astype(BF16)
        chi_ref[...] = chi
        clo_ref[...] = (w - chi.astype(F32)).astype(BF16)
        acc_ref[...] = jnp.zeros_like(acc_ref)

    first_row = jnp.int32(0)
    for gg in range(N_GROUPS):
        first_row = first_row + jnp.where(gg < g, rows(gg), 0)
    plain, tail = schedule(g)

    def chunk(start, size):
        row0 = start.astype(F32)
        slot = lax.broadcasted_iota(jnp.int32, (size, tm), 0).astype(F32) + row0
        gather = jnp.where(drow_ref[0:1, :] == slot, 1.0, 0.0).astype(BF16)
        xs = _dot(gather, xn_ref[...]).astype(BF16)
        cs = _dot(gather, chi_ref[...]) + _dot(gather, clo_ref[...])
        lane = lax.broadcasted_iota(jnp.int32, cs.shape, 1)
        y = None
        for e in range(EXPERTS_PER_GROUP):
            c_e = jnp.sum(jnp.where(lane == EXPERT_LANE0 + g * EXPERTS_PER_GROUP + e, cs, 0.0), axis=1, keepdims=True)
            hg = _dot(xs, wg_ref[e])
            hu = _dot(xs, wu_ref[e])
            hs = (hg * jax.nn.sigmoid(hg) * hu * c_e).astype(BF16)
            ye = _dot(hs, wd_ref[e])
            y = ye if y is None else y + ye
        slot_t = lax.broadcasted_iota(jnp.int32, (tm, size), 1).astype(F32) + row0
        scatter = jnp.where(dcol_ref[:, 0:1] == slot_t, 1.0, 0.0).astype(BF16)
        acc_ref[...] += _dot(scatter, y.astype(BF16))

    def plain_chunk(c, carry):
        chunk(first_row + c * MOE_CHUNK, MOE_CHUNK)
        return carry

    lax.fori_loop(0, plain, plain_chunk, 0)
    for size in MOE_TAILS:
        @pl.when(tail == size)
        def _(size=size):
            chunk(first_row + plain * MOE_CHUNK, size)

    @pl.when(g == pl.num_programs(1) - 1)
    def _():
        x2 = x1_ref[...] + acc_ref[...]
        out_ref[...] = _rms(x2, fin_ref[...]) if final_norm else x2


def _moe(x1, xn, comb, cnt, lw, final_g, final_norm):
    t = x1.shape[0]
    tm = min(MOE_TILE, t)
    per = tm // min(MERGE_TILE, t)
    counts = cnt[:, 0, :N_GROUPS].reshape(t // tm, per, N_GROUPS).sum(axis=1).astype(jnp.int32).reshape(-1)
    row = lambda w: pl.BlockSpec((tm, w), lambda i, g, c: (i, 0))
    wspec = lambda a, b: pl.BlockSpec((EXPERTS_PER_GROUP, a, b), lambda i, g, c: (g, 0, 0))
    grid_spec = pltpu.PrefetchScalarGridSpec(
        num_scalar_prefetch=1,
        grid=(t // tm, N_GROUPS),
        in_specs=[row(D_MODEL), row(D_MODEL), row(ROUTER_LANES), wspec(D_MODEL, D_EXPERT), wspec(D_MODEL, D_EXPERT),
                  wspec(D_EXPERT, D_MODEL), pl.BlockSpec((1, D_MODEL), lambda i, g, c: (0, 0))],
        out_specs=row(D_MODEL),
        scratch_shapes=[pltpu.VMEM((tm, D_MODEL), F32), pltpu.VMEM((tm, ROUTER_LANES), F32),
                        pltpu.VMEM((SUBLANES, tm), F32), pltpu.VMEM((tm, ROUTER_LANES), BF16),
                        pltpu.VMEM((tm, ROUTER_LANES), BF16)],
    )
    return pl.pallas_call(
        functools.partial(_moe_kernel, final_norm=final_norm),
        grid_spec=grid_spec,
        out_shape=jax.ShapeDtypeStruct((t, D_MODEL), F32),
        compiler_params=_params("parallel", "arbitrary"),
        name="moe",
    )(counts, x1, xn, comb, lw["wg"], lw["wu"], lw["wd"], final_g)


def _dft_angles(n):
    return 2.0 * np.pi * ((np.arange(n)[:, None] * np.arange(n)[None, :]) % n) / n


def _dft_tables(seq):
    n1, n2, n = seq // FFT_N2, FFT_N2, FOURIER_GROUP_DIM
    a1, a2, ac = _dft_angles(n1), _dft_angles(n2), _dft_angles(n)
    f1 = np.concatenate([np.cos(a1), -np.sin(a1)], axis=0)
    m2 = np.block([[np.cos(a2), np.sin(a2)], [-np.sin(a2), np.cos(a2)]])
    wcs = np.zeros((2 * FOURIER_DIM, FOURIER_DIM), np.float64)
    for g in range(FOURIER_GROUPS):
        sl = slice(g * n, (g + 1) * n)
        wcs[sl, sl] = np.cos(ac)
        wcs[FOURIER_DIM + g * n:FOURIER_DIM + (g + 1) * n, sl] = np.sin(ac)
    idx = (jnp.arange(n2, dtype=jnp.int32)[:, None] * jnp.arange(n1, dtype=jnp.int32)[None, :]) % seq
    ang = jnp.broadcast_to((idx.astype(F32) * (2.0 * math.pi / seq))[:, :, None], (n2, n1, LANES))
    return {"f1": jnp.asarray(f1, BF16), "m2": jnp.asarray(m2, BF16), "wcs": jnp.asarray(wcs, BF16),
            "twc": jnp.cos(ang), "tws": jnp.sin(ang)}


def _rope_key_placement():
    ekr = np.zeros((LANES, HEAD_PAD), np.float32)
    for h in range(N_HEADS):
        for r in range(QK_ROPE_DIM):
            ekr[r, h * LANES + QK_NOPE_DIM + r] = 1.0
    return jnp.asarray(ekr, BF16)


def _rope_tables(seq):
    inv = 1.0 / (ROPE_BASE ** (jnp.arange(0, QK_ROPE_DIM, 2, dtype=F32) / QK_ROPE_DIM))
    rang = jnp.arange(seq, dtype=F32)[:, None] * inv[None, :]
    c, s = jnp.cos(rang), jnp.sin(rang)
    z = lambda w: jnp.zeros((seq, w), F32)
    tail = LANES - QK_DIM
    cos = jnp.concatenate([jnp.ones((seq, QK_NOPE_DIM), F32), c, c, z(tail)], axis=1)
    sin1 = jnp.concatenate([z(QK_NOPE_DIM), -s, z(ROPE_HALF), z(tail)], axis=1)
    sin2 = jnp.concatenate([z(QK_NOPE_DIM), z(ROPE_HALF), s, z(tail)], axis=1)
    return cos, sin1, sin2


def _pad_heads(w, per_head, lo, hi, width=LANES):
    r = w.shape[0]
    w3 = w.reshape(r, N_HEADS, per_head)[:, :, lo:hi]
    return jnp.pad(w3, ((0, 0), (0, 0), (0, width - (hi - lo)))).reshape(r, N_HEADS * width)


def _layer_weights(l, attn_norm_g, w_in, q_norm_g, kv_norm_g, w_uq, w_ukv, w_fourier_out, w_attn_out, w_out,
                   ffn_norm_g, w_grp, b_grp, w_exp, b_exp, w_gate, w_up, w_down):
    s0, s1, s2, s3 = (FOURIER_DIM, FOURIER_DIM + Q_LORA_RANK, FOURIER_DIM + Q_LORA_RANK + KV_LORA_RANK,
                      FOURIER_DIM + Q_LORA_RANK + KV_LORA_RANK + QK_ROPE_DIM)
    wi = w_in[l]
    wr = jnp.concatenate([w_grp[l], w_exp[l]], axis=1)
    wr = jnp.pad(wr, ((0, 0), (0, ROUTER_LANES - wr.shape[1])))
    wr_hi = wr.astype(BF16)
    br = jnp.concatenate([b_grp[l], b_exp[l]])
    return {
        "attn_g": attn_norm_g[l][None, :],
        "wf": wi[:, :s0].astype(BF16),
        "wql": wi[:, s0:s1].astype(BF16),
        "wkvl": wi[:, s1:s2].astype(BF16),
        "wkr": jnp.pad(wi[:, s2:s3], ((0, 0), (0, LANES - QK_ROPE_DIM))).astype(BF16),
        "wgate": wi[:, s3:].astype(BF16),
        "qg": q_norm_g[l][None, :],
        "kvg": kv_norm_g[l][None, :],
        "wuqt": _pad_heads(w_uq[l], QK_DIM, 0, QK_DIM).T.astype(BF16),
        "wuk": _pad_heads(w_ukv[l], QK_NOPE_DIM + V_HEAD_DIM, 0, QK_NOPE_DIM).astype(BF16),
        "wuvt": _pad_heads(w_ukv[l], QK_NOPE_DIM + V_HEAD_DIM, QK_NOPE_DIM, QK_NOPE_DIM + V_HEAD_DIM,
                           V_ROWS).T.astype(BF16),
        "ekr": _rope_key_placement(),
        "wfo": w_fourier_out[l].astype(BF16),
        "wao": w_attn_out[l].astype(BF16),
        "wout": w_out[l].astype(BF16),
        "ffn_g": ffn_norm_g[l][None, :],
        "wr_hi": wr_hi,
        "wr_lo": (wr - wr_hi.astype(F32)).astype(BF16),
        "br": jnp.pad(br, (0, ROUTER_LANES - br.shape[0]))[None, :],
        "wg": w_gate[l].astype(BF16),
        "wu": w_up[l].astype(BF16),
        "wd": w_down[l].astype(BF16),
    }


def _trunk(x, layers, final_g, tables):
    bsz, seq, d = x.shape
    t = bsz * seq
    dft, rope = tables
    x2d = x.reshape(t, d)
    for l, lw in enumerate(layers):
        u, k, gate, qt, vt = _in_proj(x2d, seq, lw, rope)
        shp = lambda z: z.reshape(bsz, seq, z.shape[-1])
        f = _fourier_mix(u, bsz, seq, dft)
        o = _attention(qt, shp(k), vt).reshape(t, N_HEADS * V_HEAD_DIM)
        x1, xn, comb, cnt = _merge(x2d, f, o, gate, lw)
        x2d = _moe(x1, xn, comb, cnt, lw, final_g, final_norm=(l == len(layers) - 1))
    return x2d.reshape(bsz, seq, d)


def kernel(x_prompt, x_sample, attn_norm_g, w_in, q_norm_g, kv_norm_g, w_uq, w_ukv, w_fourier_out, w_attn_out, w_out, ffn_norm_g, w_grp, b_grp, w_exp, b_exp, w_gate, w_up, w_down, final_norm_g):
    depth = w_in.shape[0]
    layers = [_layer_weights(l, attn_norm_g, w_in, q_norm_g, kv_norm_g, w_uq, w_ukv, w_fourier_out, w_attn_out,
                             w_out, ffn_norm_g, w_grp, b_grp, w_exp, b_exp, w_gate, w_up, w_down)
              for l in range(depth)]
    final_g = final_norm_g[None, :]
    outs = []
    for x in (x_prompt, x_sample):
        seq = x.shape[1]
        outs.append(_trunk(x, layers, final_g, (_dft_tables(seq), _rope_tables(seq))))
    return tuple(outs)
```

```python
import functools
import math

import numpy as np
import jax
import jax.numpy as jnp
from jax import lax
from jax.experimental import pallas as pl
from jax.experimental.pallas import tpu as pltpu

D_MODEL = 1024
FOURIER_GROUPS = 4
FOURIER_GROUP_DIM = 128
FOURIER_DIM = FOURIER_GROUPS * FOURIER_GROUP_DIM
N_HEADS = 8
QK_NOPE_DIM = 64
QK_ROPE_DIM = 32
V_HEAD_DIM = 64
Q_LORA_RANK = 384
KV_LORA_RANK = 256
QK_DIM = QK_NOPE_DIM + QK_ROPE_DIM
ROPE_BASE = 10000.0
N_GROUPS = 4
EXPERTS_PER_GROUP = 8
N_EXPERTS = N_GROUPS * EXPERTS_PER_GROUP
D_EXPERT = 256
EPS = 1e-6

LANES = 128
HEAD_PAD = N_HEADS * LANES
ROPE_HALF = QK_ROPE_DIM // 2
ONES_ROW = V_HEAD_DIM
V_ROWS = 80
FFT_N2 = 128
ROUTER_LANES = LANES
EXPERT_LANE0 = N_GROUPS
GROUP_LANE = 0
SUBLANES = 8
ATTN_Q_TILE = 8192
ATTN_STRIP = 256
ATTN_KEY_CHUNK = 512
ATTN_SCORE_BYTES = 16 * 1024 * 1024
HEADS_PER_STEP = LANES // V_HEAD_DIM
IN_PROJ_TILE = 1024
MERGE_TILE = 1024
MOE_TILE = 1024
MOE_CHUNK = 256
MOE_TAILS = (256, 320, 512)
VMEM_LIMIT = 56 * 1024 * 1024

BF16 = jnp.bfloat16
F32 = jnp.float32


def _dot(a, b):
    return jnp.dot(a, b, preferred_element_type=F32)


def _rms(x, g):
    return x * lax.rsqrt(jnp.mean(x * x, axis=-1, keepdims=True) + EPS) * g


def _params(*sem):
    return pltpu.CompilerParams(dimension_semantics=sem, vmem_limit_bytes=VMEM_LIMIT)


def _full(shape):
    return pl.BlockSpec(shape, lambda *_: (0,) * len(shape))


def _in_proj_kernel(x_ref, g_ref, wf_ref, wql_ref, wkvl_ref, wkr_ref, wgate_ref, qg_ref, kvg_ref,
                    wuqt_ref, wuk_ref, wuvt_ref, ekr_ref, cos_ref, sin1_ref, sin2_ref, cost_ref, sin1t_ref, sin2t_ref,
                    u_ref, k_ref, gate_ref, qt_ref, vt_ref, *, q_scale):
    xb = _rms(x_ref[...], g_ref[...]).astype(BF16)
    u_ref[...] = _dot(xb, wf_ref[...]).astype(BF16)
    qn = _rms(_dot(xb, wql_ref[...]), qg_ref[...]).astype(BF16)
    kvn = _rms(_dot(xb, wkvl_ref[...]), kvg_ref[...]).astype(BF16)
    ukr = _dot(xb, wkr_ref[...]).astype(BF16)
    k = _dot(kvn, wuk_ref[...]) + _dot(ukr, ekr_ref[...])
    nt = (((1,), (1,)), ((), ()))
    qt = lax.dot_general(wuqt_ref[...], qn, nt, preferred_element_type=F32)
    vt = lax.dot_general(wuvt_ref[...], kvn, nt, preferred_element_type=F32)
    head_row = lax.broadcasted_iota(jnp.int32, vt.shape, 0) % V_ROWS
    vt = jnp.where(head_row == ONES_ROW, 1.0, vt).astype(BF16)
    tk = vt_ref.shape[-1]
    for c in range(vt_ref.shape[1]):
        vt_ref[0, c] = vt[:, c * tk:(c + 1) * tk]
    cos, sin1, sin2 = cos_ref[...], sin1_ref[...], sin2_ref[...]
    cost, sin1t, sin2t = cost_ref[...], sin1t_ref[...], sin2t_ref[...]
    k_up = pltpu.roll(k, HEAD_PAD - ROPE_HALF, 1)
    k_dn = pltpu.roll(k, ROPE_HALF, 1)
    qt_up = jnp.concatenate([qt[ROPE_HALF:], qt[:ROPE_HALF]], axis=0)
    qt_dn = jnp.concatenate([qt[-ROPE_HALF:], qt[:-ROPE_HALF]], axis=0)
    for h in range(N_HEADS):
        sl = slice(h * LANES, (h + 1) * LANES)
        k_ref[:, sl] = (k[:, sl] * cos + k_up[:, sl] * sin1 + k_dn[:, sl] * sin2).astype(BF16)
        qt_ref[0, sl, :] = ((qt[sl] * cost + qt_up[sl] * sin1t + qt_dn[sl] * sin2t) * q_scale).astype(BF16)
    gate_ref[...] = jax.nn.sigmoid(_dot(xb, wgate_ref[...])).astype(BF16)


def _in_proj(x2d, seq, lw, rope):
    t = x2d.shape[0]
    tk = _key_chunk(seq)
    tm = max(tk, min(IN_PROJ_TILE, seq))
    n_seq_tiles = seq // tm
    row = lambda w: pl.BlockSpec((tm, w), lambda i: (i, 0))
    pos = pl.BlockSpec((tm, LANES), lambda i: (i % n_seq_tiles, 0))
    pos_t = pl.BlockSpec((LANES, tm), lambda i: (0, i % n_seq_tiles))
    weights = [lw["attn_g"], lw["wf"], lw["wql"], lw["wkvl"], lw["wkr"], lw["wgate"], lw["qg"], lw["kvg"],
               lw["wuqt"], lw["wuk"], lw["wuvt"], lw["ekr"]]
    out_w = [FOURIER_DIM, HEAD_PAD, 2 * D_MODEL]
    qt_spec = pl.BlockSpec((1, HEAD_PAD, tm), lambda i: (i // n_seq_tiles, 0, i % n_seq_tiles))
    qt_shape = jax.ShapeDtypeStruct((t // seq, HEAD_PAD, seq), BF16)
    vt_spec = pl.BlockSpec((1, tm // tk, N_HEADS * V_ROWS, tk), lambda i: (i // n_seq_tiles, i % n_seq_tiles, 0, 0))
    vt_shape = jax.ShapeDtypeStruct((t // seq, seq // tk, N_HEADS * V_ROWS, tk), BF16)
    return pl.pallas_call(
        functools.partial(_in_proj_kernel, q_scale=(QK_DIM ** -0.5) * math.log2(math.e)),
        grid=(t // tm,),
        in_specs=[row(D_MODEL)] + [_full(w.shape) for w in weights] + [pos] * 3 + [pos_t] * 3,
        out_specs=[row(w) for w in out_w] + [qt_spec, vt_spec],
        out_shape=[jax.ShapeDtypeStruct((t, w), BF16) for w in out_w] + [qt_shape, vt_shape],
        compiler_params=_params("parallel"),
        name="in_proj",
    )(x2d, *weights, *rope, *[r.T for r in rope])


def _dft_a_kernel(u_ref, f1_ref, twc_ref, tws_ref, y_ref, *, tj, n1):
    f1 = f1_ref[...]
    for j in range(tj):
        y = _dot(f1, u_ref[0, j])
        yr, yi = y[:n1], y[n1:]
        c = jnp.concatenate([twc_ref[j]] * FOURIER_GROUPS, axis=1)
        s = jnp.concatenate([tws_ref[j]] * FOURIER_GROUPS, axis=1)
        y_ref[0, j, 0] = (yr * c + yi * s).astype(BF16)
        y_ref[0, j, 1] = (yi * c - yr * s).astype(BF16)


def _dft_a(u4, f1, twc, tws):
    bsz, n2, n1, _ = u4.shape
    tj = 16
    return pl.pallas_call(
        functools.partial(_dft_a_kernel, tj=tj, n1=n1),
        grid=(bsz, n2 // tj),
        in_specs=[pl.BlockSpec((1, tj, n1, FOURIER_DIM), lambda b, j: (b, j, 0, 0)), _full(f1.shape),
                  pl.BlockSpec((tj, n1, LANES), lambda b, j: (j, 0, 0)),
                  pl.BlockSpec((tj, n1, LANES), lambda b, j: (j, 0, 0))],
        out_specs=pl.BlockSpec((1, tj, 2, n1, FOURIER_DIM), lambda b, j: (b, j, 0, 0, 0)),
        out_shape=jax.ShapeDtypeStruct((bsz, n2, 2, n1, FOURIER_DIM), BF16),
        compiler_params=_params("parallel", "parallel"),
        name="dft_a",
    )(u4, f1, twc, tws)


def _dft_b_kernel(y_ref, m2_ref, wcs_ref, f_ref, x_scr, *, tk1, scale):
    m2 = m2_ref[...]
    for j in range(tk1):
        x = _dot(m2, y_ref[0, j].reshape(2 * FFT_N2, FOURIER_DIM))
        rows = slice(j * FFT_N2, (j + 1) * FFT_N2)
        x_scr[rows, :FOURIER_DIM] = x[:FFT_N2].astype(BF16)
        x_scr[rows, FOURIER_DIM:] = x[FFT_N2:].astype(BF16)
    f = _dot(x_scr[...], wcs_ref[...]) * scale
    f_ref[0] = f.reshape(tk1, FFT_N2, FOURIER_DIM).astype(BF16)


def _dft_b(y5, m2, wcs, seq):
    bsz, n1 = y5.shape[:2]
    tk1 = 8
    return pl.pallas_call(
        functools.partial(_dft_b_kernel, tk1=tk1, scale=(seq * FOURIER_GROUP_DIM) ** -0.5),
        grid=(bsz, n1 // tk1),
        in_specs=[pl.BlockSpec((1, tk1, 2, FFT_N2, FOURIER_DIM), lambda b, j: (b, j, 0, 0, 0)),
                  _full(m2.shape), _full(wcs.shape)],
        out_specs=pl.BlockSpec((1, tk1, FFT_N2, FOURIER_DIM), lambda b, j: (b, j, 0, 0)),
        out_shape=jax.ShapeDtypeStruct((bsz, n1, FFT_N2, FOURIER_DIM), BF16),
        scratch_shapes=[pltpu.VMEM((tk1 * FFT_N2, 2 * FOURIER_DIM), BF16)],
        compiler_params=_params("parallel", "parallel"),
        name="dft_b",
    )(y5, m2, wcs)


def _fourier_mix(u2d, bsz, seq, dft):
    n1 = seq // FFT_N2
    u4 = u2d.reshape(bsz, n1, FFT_N2, FOURIER_DIM).transpose(0, 2, 1, 3)
    y5 = _dft_a(u4, dft["f1"], dft["twc"], dft["tws"])
    f4 = _dft_b(y5.transpose(0, 3, 2, 1, 4), dft["m2"], dft["wcs"], seq)
    return f4.transpose(0, 2, 1, 3).reshape(bsz * seq, FOURIER_DIM)


def _key_chunk(seq):
    fit = ATTN_SCORE_BYTES // (2 * 4 * min(ATTN_Q_TILE, seq))
    return min(ATTN_KEY_CHUNK, fit, seq // 2)


def _attn_kernel(q_ref, k_ref, vt_ref, o_ref, m_ref, acc_ref, s_ref, smax_ref, ot_ref, *, tk, nk):
    tq = q_ref.shape[2]
    strips = [slice(j * ATTN_STRIP, (j + 1) * ATTN_STRIP) for j in range(tq // ATTN_STRIP)]
    for hh in range(HEADS_PER_STEP):
        head = slice(hh * LANES, (hh + 1) * LANES)
        m_ref[...] = jnp.full_like(m_ref, -jnp.inf)
        acc_ref[...] = jnp.zeros_like(acc_ref)

        def scores(kc, slot, sl):
            ks = pl.multiple_of(kc * tk, tk)
            st = _dot(k_ref[0, pl.ds(ks, tk), head], q_ref[0, head, sl])
            s_ref[slot, :, sl] = st
            smax_ref[slot, :, sl] = jnp.max(st, axis=0, keepdims=True)

        def softmax_pv(kc, slot, sl):
            m_prev = m_ref[:, sl]
            m_new = jnp.maximum(m_prev, smax_ref[slot, :, sl])
            alpha = jnp.exp2(m_prev - m_new)
            p = jnp.exp2(s_ref[slot, :, sl] - m_new).astype(BF16)
            vt = vt_ref[0, kc, hh * V_ROWS:(hh + 1) * V_ROWS, :]
            acc_ref[:, sl] = alpha * acc_ref[:, sl] + _dot(vt, p)
            m_ref[:, sl] = m_new

        for sl in strips:
            scores(0, 0, sl)

        def body(j, carry):
            c = 2 * j
            for sl in strips:
                scores(c + 1, 1, sl)
                softmax_pv(c, 0, sl)
            for sl in strips:
                scores(jnp.minimum(c + 2, nk - 1), 0, sl)
                softmax_pv(c + 1, 1, sl)
            return carry

        lax.fori_loop(0, nk // 2, body, 0)
        acc = acc_ref[...]
        ot_ref[hh * V_HEAD_DIM:(hh + 1) * V_HEAD_DIM, :] = acc[:V_HEAD_DIM] / acc[ONES_ROW:ONES_ROW + 1, :]
    o_ref[0] = ot_ref[...].T.astype(BF16)


def _attention(qt3, k3, vt4):
    bsz, seq, _ = k3.shape
    tq = min(ATTN_Q_TILE, seq)
    tk = _key_chunk(seq)
    nk = seq // tk
    hp = HEADS_PER_STEP
    qt = pl.BlockSpec((1, hp * LANES, tq), lambda b, h, i: (b, h, i))
    qo = pl.BlockSpec((1, tq, hp * V_HEAD_DIM), lambda b, h, i: (b, i, h))
    kk = pl.BlockSpec((1, seq, hp * LANES), lambda b, h, i: (b, 0, h))
    vt = pl.BlockSpec((1, nk, hp * V_ROWS, tk), lambda b, h, i: (b, 0, h, 0))
    return pl.pallas_call(
        functools.partial(_attn_kernel, tk=tk, nk=nk),
        grid=(bsz, N_HEADS // hp, seq // tq),
        in_specs=[qt, kk, vt],
        out_specs=qo,
        out_shape=jax.ShapeDtypeStruct((bsz, seq, N_HEADS * V_HEAD_DIM), BF16),
        scratch_shapes=[pltpu.VMEM((1, tq), F32), pltpu.VMEM((V_ROWS, tq), F32), pltpu.VMEM((2, tk, tq), F32),
                        pltpu.VMEM((2, 1, tq), F32), pltpu.VMEM((hp * V_HEAD_DIM, tq), F32)],
        compiler_params=_params("parallel", "parallel", "arbitrary"),
        name="attn",
    )(qt3, k3, vt4)


def _merge_kernel(x_ref, f_ref, o_ref, gate_ref, wfo_ref, wao_ref, wout_ref, fg_ref, wrh_ref, wrl_ref, br_ref,
                  x1_ref, xn_ref, comb_ref, cnt_ref):
    y_f = _dot(f_ref[...], wfo_ref[...])
    y_a = _dot(o_ref[...], wao_ref[...])
    gate = gate_ref[...].astype(F32)
    merged = gate[:, :D_MODEL] * y_f + gate[:, D_MODEL:] * y_a
    x1 = x_ref[...] + _dot(merged.astype(BF16), wout_ref[...])
    x1_ref[...] = x1
    xn = _rms(x1, fg_ref[...])
    xh = xn.astype(BF16)
    xn_ref[...] = xh
    xl = (xn - xh.astype(F32)).astype(BF16)
    logit = _dot(xh, wrh_ref[...]) + _dot(xl, wrh_ref[...]) + _dot(xh, wrl_ref[...]) + br_ref[...]
    lane = lax.broadcasted_iota(jnp.int32, logit.shape, 1)
    neg = jnp.float32(-jnp.inf)

    def top(mask):
        val = jnp.max(jnp.where(mask, logit, neg), axis=1, keepdims=True)
        idx = jnp.min(jnp.where(mask & (logit == val), lane, ROUTER_LANES), axis=1, keepdims=True)
        return val, idx

    is_grp = lane < N_GROUPS
    g_max, g_idx = top(is_grp)
    g_sum = jnp.sum(jnp.where(is_grp, jnp.exp(logit - g_max), 0.0), axis=1, keepdims=True)
    grp_p = 1.0 / g_sum
    lo = EXPERT_LANE0 + g_idx * EXPERTS_PER_GROUP
    in_grp = (lane >= lo) & (lane < lo + EXPERTS_PER_GROUP)
    l1, i1 = top(in_grp)
    l2, i2 = top(in_grp & (lane != i1))
    e2 = jnp.exp(l2 - l1)
    w1 = 1.0 / (1.0 + e2)
    w2 = e2 / (1.0 + e2)
    comb = jnp.where(lane == i1, grp_p * w1, jnp.where(lane == i2, grp_p * w2, 0.0))
    comb_ref[...] = jnp.where(lane == GROUP_LANE, g_idx.astype(F32), comb)
    cnt = jnp.sum(jnp.where(lane == g_idx, 1.0, 0.0), axis=0, keepdims=True)
    cnt_ref[0] = jnp.broadcast_to(cnt, cnt_ref.shape[1:])


def _merge(x2d, f2d, o2d, gate2d, lw):
    t = x2d.shape[0]
    tm = min(MERGE_TILE, t)
    row = lambda w: pl.BlockSpec((tm, w), lambda i: (i, 0))
    weights = [lw["wfo"], lw["wao"], lw["wout"], lw["ffn_g"], lw["wr_hi"], lw["wr_lo"], lw["br"]]
    return pl.pallas_call(
        _merge_kernel,
        grid=(t // tm,),
        in_specs=[row(D_MODEL), row(FOURIER_DIM), row(N_HEADS * V_HEAD_DIM), row(2 * D_MODEL)]
        + [_full(w.shape) for w in weights],
        out_specs=[row(D_MODEL), row(D_MODEL), row(ROUTER_LANES),
                   pl.BlockSpec((1, SUBLANES, ROUTER_LANES), lambda i: (i, 0, 0))],
        out_shape=[jax.ShapeDtypeStruct((t, D_MODEL), F32), jax.ShapeDtypeStruct((t, D_MODEL), BF16),
                   jax.ShapeDtypeStruct((t, ROUTER_LANES), F32),
                   jax.ShapeDtypeStruct((t // tm, SUBLANES, ROUTER_LANES), F32)],
        compiler_params=_params("parallel"),
        name="merge",
    )(x2d, f2d, o2d, gate2d, *weights)


def _moe_kernel(cnt_ref, x1_ref, xn_ref, comb_ref, wg_ref, wu_ref, wd_ref, fin_ref, out_ref,
                acc_ref, dcol_ref, drow_ref, chi_ref, clo_ref, *, final_norm):
    i, g = pl.program_id(0), pl.program_id(1)
    tm = xn_ref.shape[0]

    def schedule(gg):
        n = cnt_ref[i * N_GROUPS + gg]
        full, rem = n // MOE_CHUNK, n % MOE_CHUNK
        plain = jnp.where(rem > 0, jnp.maximum(full - 1, 0), full)
        need = jnp.where(rem > 0, jnp.where(full > 0, MOE_CHUNK + rem, rem), 0)
        tail = jnp.int32(0)
        for size in reversed(MOE_TAILS):
            tail = jnp.where(need <= size, size, tail)
        return plain, jnp.where(need > 0, tail, 0)

    def rows(gg):
        plain, tail = schedule(gg)
        return plain * MOE_CHUNK + tail

    @pl.when(g == 0)
    def _():
        comb = comb_ref[...]
        lane = lax.broadcasted_iota(jnp.int32, comb.shape, 1)
        grp = comb[:, GROUP_LANE:GROUP_LANE + 1].astype(jnp.int32)
        onehot = jnp.where((lane == grp) & (lane < N_GROUPS), 1.0, 0.0)
        earlier = lax.broadcasted_iota(jnp.int32, (tm, tm), 0) > lax.broadcasted_iota(jnp.int32, (tm, tm), 1)
        before = _dot(jnp.where(earlier, 1.0, 0.0).astype(BF16), onehot.astype(BF16))
        dest = jnp.sum(onehot * before, axis=1, keepdims=True)
        first_row = jnp.int32(0)
        for gg in range(N_GROUPS):
            dest = dest + jnp.where(grp == gg, first_row.astype(F32), 0.0)
            first_row = first_row + rows(gg)
        dcol = jnp.broadcast_to(dest, comb.shape)
        dcol_ref[...] = dcol
        drow_ref[...] = dcol.T[:SUBLANES, :]
        w = jnp.where(lane == GROUP_LANE, 0.0, comb)
        chi = w.astype(BF16)
        chi_ref[...] = chi
        clo_ref[...] = (w - chi.astype(F32)).astype(BF16)
        acc_ref[...] = jnp.zeros_like(acc_ref)

    first_row = jnp.int32(0)
    for gg in range(N_GROUPS):
        first_row = first_row + jnp.where(gg < g, rows(gg), 0)
    plain, tail = schedule(g)

    def chunk(start, size):
        row0 = start.astype(F32)
        slot = lax.broadcasted_iota(jnp.int32, (size, tm), 0).astype(F32) + row0
        gather = jnp.where(drow_ref[0:1, :] == slot, 1.0, 0.0).astype(BF16)
        xs = _dot(gather, xn_ref[...]).astype(BF16)
        cs = _dot(gather, chi_ref[...]) + _dot(gather, clo_ref[...])
        lane = lax.broadcasted_iota(jnp.int32, cs.shape, 1)
        y = None
        for e in range(EXPERTS_PER_GROUP):
            c_e = jnp.sum(jnp.where(lane == EXPERT_LANE0 + g * EXPERTS_PER_GROUP + e, cs, 0.0), axis=1, keepdims=True)
            hg = _dot(xs, wg_ref[e])
            hu = _dot(xs, wu_ref[e])
            hs = (hg * jax.nn.sigmoid(hg) * hu * c_e).astype(BF16)
            ye = _dot(hs, wd_ref[e])
            y = ye if y is None else y + ye
        slot_t = lax.broadcasted_iota(jnp.int32, (tm, size), 1).astype(F32) + row0
        scatter = jnp.where(dcol_ref[:, 0:1] == slot_t, 1.0, 0.0).astype(BF16)
        acc_ref[...] += _dot(scatter, y.astype(BF16))

    def plain_chunk(c, carry):
        chunk(first_row + c * MOE_CHUNK, MOE_CHUNK)
        return carry

    lax.fori_loop(0, plain, plain_chunk, 0)
    for size in MOE_TAILS:
        @pl.when(tail == size)
        def _(size=size):
            chunk(first_row + plain * MOE_CHUNK, size)

    @pl.when(g == pl.num_programs(1) - 1)
    def _():
        x2 = x1_ref[...] + acc_ref[...]
        out_ref[...] = _rms(x2, fin_ref[...]) if final_norm else x2


def _moe(x1, xn, comb, cnt, lw, final_g, final_norm):
    t = x1.shape[0]
    tm = min(MOE_TILE, t)
    per = tm // min(MERGE_TILE, t)
    counts = cnt[:, 0, :N_GROUPS].reshape(t // tm, per, N_GROUPS).sum(axis=1).astype(jnp.int32).reshape(-1)
    row = lambda w: pl.BlockSpec((tm, w), lambda i, g, c: (i, 0))
    wspec = lambda a, b: pl.BlockSpec((EXPERTS_PER_GROUP, a, b), lambda i, g, c: (g, 0, 0))
    grid_spec = pltpu.PrefetchScalarGridSpec(
        num_scalar_prefetch=1,
        grid=(t // tm, N_GROUPS),
        in_specs=[row(D_MODEL), row(D_MODEL), row(ROUTER_LANES), wspec(D_MODEL, D_EXPERT), wspec(D_MODEL, D_EXPERT),
                  wspec(D_EXPERT, D_MODEL), pl.BlockSpec((1, D_MODEL), lambda i, g, c: (0, 0))],
        out_specs=row(D_MODEL),
        scratch_shapes=[pltpu.VMEM((tm, D_MODEL), F32), pltpu.VMEM((tm, ROUTER_LANES), F32),
                        pltpu.VMEM((SUBLANES, tm), F32), pltpu.VMEM((tm, ROUTER_LANES), BF16),
                        pltpu.VMEM((tm, ROUTER_LANES), BF16)],
    )
    return pl.pallas_call(
        functools.partial(_moe_kernel, final_norm=final_norm),
        grid_spec=grid_spec,
        out_shape=jax.ShapeDtypeStruct((t, D_MODEL), F32),
        compiler_params=_params("parallel", "arbitrary"),
        name="moe",
    )(counts, x1, xn, comb, lw["wg"], lw["wu"], lw["wd"], final_g)


def _dft_angles(n):
    return 2.0 * np.pi * ((np.arange(n)[:, None] * np.arange(n)[None, :]) % n) / n


def _dft_tables(seq):
    n1, n2, n = seq // FFT_N2, FFT_N2, FOURIER_GROUP_DIM
    a1, a2, ac = _dft_angles(n1), _dft_angles(n2), _dft_angles(n)
    f1 = np.concatenate([np.cos(a1), -np.sin(a1)], axis=0)
    m2 = np.block([[np.cos(a2), np.sin(a2)], [-np.sin(a2), np.cos(a2)]])
    wcs = np.zeros((2 * FOURIER_DIM, FOURIER_DIM), np.float64)
    for g in range(FOURIER_GROUPS):
        sl = slice(g * n, (g + 1) * n)
        wcs[sl, sl] = np.cos(ac)
        wcs[FOURIER_DIM + g * n:FOURIER_DIM + (g + 1) * n, sl] = np.sin(ac)
    idx = (jnp.arange(n2, dtype=jnp.int32)[:, None] * jnp.arange(n1, dtype=jnp.int32)[None, :]) % seq
    ang = jnp.broadcast_to((idx.astype(F32) * (2.0 * math.pi / seq))[:, :, None], (n2, n1, LANES))
    return {"f1": jnp.asarray(f1, BF16), "m2": jnp.asarray(m2, BF16), "wcs": jnp.asarray(wcs, BF16),
            "twc": jnp.cos(ang), "tws": jnp.sin(ang)}


def _rope_key_placement():
    ekr = np.zeros((LANES, HEAD_PAD), np.float32)
    for h in range(N_HEADS):
        for r in range(QK_ROPE_DIM):
            ekr[r, h * LANES + QK_NOPE_DIM + r] = 1.0
    return jnp.asarray(ekr, BF16)


def _rope_tables(seq):
    inv = 1.0 / (ROPE_BASE ** (jnp.arange(0, QK_ROPE_DIM, 2, dtype=F32) / QK_ROPE_DIM))
    rang = jnp.arange(seq, dtype=F32)[:, None] * inv[None, :]
    c, s = jnp.cos(rang), jnp.sin(rang)
    z = lambda w: jnp.zeros((seq, w), F32)
    tail = LANES - QK_DIM
    cos = jnp.concatenate([jnp.ones((seq, QK_NOPE_DIM), F32), c, c, z(tail)], axis=1)
    sin1 = jnp.concatenate([z(QK_NOPE_DIM), -s, z(ROPE_HALF), z(tail)], axis=1)
    sin2 = jnp.concatenate([z(QK_NOPE_DIM), z(ROPE_HALF), s, z(tail)], axis=1)
    return cos, sin1, sin2


def _pad_heads(w, per_head, lo, hi, width=LANES):
    r = w.shape[0]
    w3 = w.reshape(r, N_HEADS, per_head)[:, :, lo:hi]
    return jnp.pad(w3, ((0, 0), (0, 0), (0, width - (hi - lo)))).reshape(r, N_HEADS * width)


def _layer_weights(l, attn_norm_g, w_in, q_norm_g, kv_norm_g, w_uq, w_ukv, w_fourier_out, w_attn_out, w_out,
                   ffn_norm_g, w_grp, b_grp, w_exp, b_exp, w_gate, w_up, w_down):
    s0, s1, s2, s3 = (FOURIER_DIM, FOURIER_DIM + Q_LORA_RANK, FOURIER_DIM + Q_LORA_RANK + KV_LORA_RANK,
                      FOURIER_DIM + Q_LORA_RANK + KV_LORA_RANK + QK_ROPE_DIM)
    wi = w_in[l]
    wr = jnp.concatenate([w_grp[l], w_exp[l]], axis=1)
    wr = jnp.pad(wr, ((0, 0), (0, ROUTER_LANES - wr.shape[1])))
    wr_hi = wr.astype(BF16)
    br = jnp.concatenate([b_grp[l], b_exp[l]])
    return {
        "attn_g": attn_norm_g[l][None, :],
        "wf": wi[:, :s0].astype(BF16),
        "wql": wi[:, s0:s1].astype(BF16),
        "wkvl": wi[:, s1:s2].astype(BF16),
        "wkr": jnp.pad(wi[:, s2:s3], ((0, 0), (0, LANES - QK_ROPE_DIM))).astype(BF16),
        "wgate": wi[:, s3:].astype(BF16),
        "qg": q_norm_g[l][None, :],
        "kvg": kv_norm_g[l][None, :],
        "wuqt": _pad_heads(w_uq[l], QK_DIM, 0, QK_DIM).T.astype(BF16),
        "wuk": _pad_heads(w_ukv[l], QK_NOPE_DIM + V_HEAD_DIM, 0, QK_NOPE_DIM).astype(BF16),
        "wuvt": _pad_heads(w_ukv[l], QK_NOPE_DIM + V_HEAD_DIM, QK_NOPE_DIM, QK_NOPE_DIM + V_HEAD_DIM,
                           V_ROWS).T.astype(BF16),
        "ekr": _rope_key_placement(),
        "wfo": w_fourier_out[l].astype(BF16),
        "wao": w_attn_out[l].astype(BF16),
        "wout": w_out[l].astype(BF16),
        "ffn_g": ffn_norm_g[l][None, :],
        "wr_hi": wr_hi,
        "wr_lo": (wr - wr_hi.astype(F32)).astype(BF16),
        "br": jnp.pad(br, (0, ROUTER_LANES - br.shape[0]))[None, :],
        "wg": w_gate[l].astype(BF16),
        "wu": w_up[l].astype(BF16),
        "wd": w_down[l].astype(BF16),
    }


def _trunk(x, layers, final_g, tables):
    bsz, seq, d = x.shape
    t = bsz * seq
    dft, rope = tables
    x2d = x.reshape(t, d)
    for l, lw in enumerate(layers):
        u, k, gate, qt, vt = _in_proj(x2d, seq, lw, rope)
        shp = lambda z: z.reshape(bsz, seq, z.shape[-1])
        f = _fourier_mix(u, bsz, seq, dft)
        o = _attention(qt, shp(k), vt).reshape(t, N_HEADS * V_HEAD_DIM)
        x1, xn, comb, cnt = _merge(x2d, f, o, gate, lw)
        x2d = _moe(x1, xn, comb, cnt, lw, final_g, final_norm=(l == len(layers) - 1))
    return x2d.reshape(bsz, seq, d)


def kernel(x_prompt, x_sample, attn_norm_g, w_in, q_norm_g, kv_norm_g, w_uq, w_ukv, w_fourier_out, w_attn_out, w_out, ffn_norm_g, w_grp, b_grp, w_exp, b_exp, w_gate, w_up, w_down, final_norm_g):
    depth = w_in.shape[0]
    layers = [_layer_weights(l, attn_norm_g, w_in, q_norm_g, kv_norm_g, w_uq, w_ukv, w_fourier_out, w_attn_out,
                             w_out, ffn_norm_g, w_grp, b_grp, w_exp, b_exp, w_gate, w_up, w_down)
              for l in range(depth)]
    final_g = final_norm_g[None, :]
    outs = []
    for x in (x_prompt, x_sample):
        seq = x.shape[1]
        outs.append(_trunk(x, layers, final_g, (_dft_tables(seq), _rope_tables(seq))))
    return tuple(outs)
```

```python
import functools
import math

import numpy as np
import jax
import jax.numpy as jnp
from jax import lax
from jax.experimental import pallas as pl
from jax.experimental.pallas import tpu as pltpu

D_MODEL = 1024
FOURIER_GROUPS = 4
FOURIER_GROUP_DIM = 128
FOURIER_DIM = FOURIER_GROUPS * FOURIER_GROUP_DIM
N_HEADS = 8
QK_NOPE_DIM = 64
QK_ROPE_DIM = 32
V_HEAD_DIM = 64
Q_LORA_RANK = 384
KV_LORA_RANK = 256
QK_DIM = QK_NOPE_DIM + QK_ROPE_DIM
ROPE_BASE = 10000.0
N_GROUPS = 4
EXPERTS_PER_GROUP = 8
N_EXPERTS = N_GROUPS * EXPERTS_PER_GROUP
D_EXPERT = 256
EPS = 1e-6

LANES = 128
HEAD_PAD = N_HEADS * LANES
ROPE_HALF = QK_ROPE_DIM // 2
ONES_ROW = V_HEAD_DIM
V_ROWS = 80
FFT_N2 = 128
DFT_A_POSITIONS = 16
STAGE_PAD = 8
ROUTER_LANES = LANES
EXPERT_LANE0 = N_GROUPS
GROUP_LANE = 0
SUBLANES = 8
ATTN_Q_TILE = 8192
ATTN_STRIP = 256
ATTN_KEY_CHUNK = 512
ATTN_SCORE_BYTES = 16 * 1024 * 1024
HEADS_PER_STEP = LANES // V_HEAD_DIM
IN_PROJ_TILE = 1024
MERGE_TILE = 1024
MOE_TILE = 1024
MOE_CHUNK = 256
MOE_TAILS = (256, 320, 512)
VMEM_LIMIT = 56 * 1024 * 1024

BF16 = jnp.bfloat16
F32 = jnp.float32


def _dot(a, b):
    return jnp.dot(a, b, preferred_element_type=F32)


def _rms(x, g):
    return x * lax.rsqrt(jnp.mean(x * x, axis=-1, keepdims=True) + EPS) * g


def _params(*sem):
    return pltpu.CompilerParams(dimension_semantics=sem, vmem_limit_bytes=VMEM_LIMIT)


def _full(shape):
    return pl.BlockSpec(shape, lambda *_: (0,) * len(shape))


def _in_proj_kernel(x_ref, g_ref, wf_ref, wql_ref, wkvl_ref, wkr_ref, wgate_ref, qg_ref, kvg_ref,
                    wuqt_ref, wuk_ref, wuvt_ref, ekr_ref, cos_ref, sin1_ref, sin2_ref, cost_ref, sin1t_ref, sin2t_ref,
                    u_ref, k_ref, gate_ref, qt_ref, vt_ref, *, q_scale):
    xb = _rms(x_ref[...], g_ref[...]).astype(BF16)
    u_ref[...] = _dot(xb, wf_ref[...]).astype(BF16)
    qn = _rms(_dot(xb, wql_ref[...]), qg_ref[...]).astype(BF16)
    kvn = _rms(_dot(xb, wkvl_ref[...]), kvg_ref[...]).astype(BF16)
    ukr = _dot(xb, wkr_ref[...]).astype(BF16)
    k = _dot(kvn, wuk_ref[...]) + _dot(ukr, ekr_ref[...])
    nt = (((1,), (1,)), ((), ()))
    qt = lax.dot_general(wuqt_ref[...], qn, nt, preferred_element_type=F32)
    vt = lax.dot_general(wuvt_ref[...], kvn, nt, preferred_element_type=F32)
    head_row = lax.broadcasted_iota(jnp.int32, vt.shape, 0) % V_ROWS
    vt = jnp.where(head_row == ONES_ROW, 1.0, vt).astype(BF16)
    tk = vt_ref.shape[-1]
    for c in range(vt_ref.shape[1]):
        vt_ref[0, c] = vt[:, c * tk:(c + 1) * tk]
    cos, sin1, sin2 = cos_ref[...], sin1_ref[...], sin2_ref[...]
    cost, sin1t, sin2t = cost_ref[...], sin1t_ref[...], sin2t_ref[...]
    k_up = pltpu.roll(k, HEAD_PAD - ROPE_HALF, 1)
    k_dn = pltpu.roll(k, ROPE_HALF, 1)
    qt_up = jnp.concatenate([qt[ROPE_HALF:], qt[:ROPE_HALF]], axis=0)
    qt_dn = jnp.concatenate([qt[-ROPE_HALF:], qt[:-ROPE_HALF]], axis=0)
    for h in range(N_HEADS):
        sl = slice(h * LANES, (h + 1) * LANES)
        k_ref[:, sl] = (k[:, sl] * cos + k_up[:, sl] * sin1 + k_dn[:, sl] * sin2).astype(BF16)
        qt_ref[0, sl, :] = ((qt[sl] * cost + qt_up[sl] * sin1t + qt_dn[sl] * sin2t) * q_scale).astype(BF16)
    gate_ref[...] = jax.nn.sigmoid(_dot(xb, wgate_ref[...])).astype(BF16)


def _in_proj(x2d, seq, lw, rope):
    t = x2d.shape[0]
    tk = _key_chunk(seq)
    tm = max(tk, min(IN_PROJ_TILE, seq))
    n_seq_tiles = seq // tm
    row = lambda w: pl.BlockSpec((tm, w), lambda i: (i, 0))
    pos = pl.BlockSpec((tm, LANES), lambda i: (i % n_seq_tiles, 0))
    pos_t = pl.BlockSpec((LANES, tm), lambda i: (0, i % n_seq_tiles))
    weights = [lw["attn_g"], lw["wf"], lw["wql"], lw["wkvl"], lw["wkr"], lw["wgate"], lw["qg"], lw["kvg"],
               lw["wuqt"], lw["wuk"], lw["wuvt"], lw["ekr"]]
    out_w = [FOURIER_DIM, HEAD_PAD, 2 * D_MODEL]
    qt_spec = pl.BlockSpec((1, HEAD_PAD, tm), lambda i: (i // n_seq_tiles, 0, i % n_seq_tiles))
    qt_shape = jax.ShapeDtypeStruct((t // seq, HEAD_PAD, seq), BF16)
    vt_spec = pl.BlockSpec((1, tm // tk, N_HEADS * V_ROWS, tk), lambda i: (i // n_seq_tiles, i % n_seq_tiles, 0, 0))
    vt_shape = jax.ShapeDtypeStruct((t // seq, seq // tk, N_HEADS * V_ROWS, tk), BF16)
    return pl.pallas_call(
        functools.partial(_in_proj_kernel, q_scale=(QK_DIM ** -0.5) * math.log2(math.e)),
        grid=(t // tm,),
        in_specs=[row(D_MODEL)] + [_full(w.shape) for w in weights] + [pos] * 3 + [pos_t] * 3,
        out_specs=[row(w) for w in out_w] + [qt_spec, vt_spec],
        out_shape=[jax.ShapeDtypeStruct((t, w), BF16) for w in out_w] + [qt_shape, vt_shape],
        compiler_params=_params("parallel"),
        name="in_proj",
    )(x2d, *weights, *rope, *[r.T for r in rope])


def _dft_a_kernel(u_ref, f1_ref, twc_ref, tws_ref, y_ref, u_scr, yr_scr, yi_scr, *, tj, n1):
    f1 = f1_ref[...]
    groups = [slice(g * LANES, (g + 1) * LANES) for g in range(FOURIER_GROUPS)]

    def put(scr, rows, val):
        for g, lanes in enumerate(groups):
            scr[g, rows, :] = val[:, lanes]

    def strided(scr, start, size, stride):
        return jnp.concatenate([scr[g, pl.ds(start, size, stride=stride), :] for g in range(FOURIER_GROUPS)], axis=1)

    up, yp = tj + STAGE_PAD, n1 + STAGE_PAD
    for i in range(n1):
        put(u_scr, slice(i * up, i * up + tj), u_ref[0, i].astype(F32))
    for j in range(tj):
        u_j = strided(u_scr, j, n1, up).astype(BF16)
        y = _dot(f1, u_j)
        yr, yi = y[:n1], y[n1:]
        c = jnp.concatenate([twc_ref[j]] * FOURIER_GROUPS, axis=1)
        s = jnp.concatenate([tws_ref[j]] * FOURIER_GROUPS, axis=1)
        put(yr_scr, slice(j * yp, j * yp + n1), yr * c + yi * s)
        put(yi_scr, slice(j * yp, j * yp + n1), yi * c - yr * s)
    for k1 in range(n1):
        y_ref[0, k1, 0] = strided(yr_scr, k1, tj, yp).astype(BF16)
        y_ref[0, k1, 1] = strided(yi_scr, k1, tj, yp).astype(BF16)


def _dft_a(u4, f1, twc, tws):
    bsz, n1, n2, _ = u4.shape
    tj = DFT_A_POSITIONS
    return pl.pallas_call(
        functools.partial(_dft_a_kernel, tj=tj, n1=n1),
        grid=(bsz, n2 // tj),
        in_specs=[pl.BlockSpec((1, n1, tj, FOURIER_DIM), lambda b, j: (b, 0, j, 0)), _full(f1.shape),
                  pl.BlockSpec((tj, n1, LANES), lambda b, j: (j, 0, 0)),
                  pl.BlockSpec((tj, n1, LANES), lambda b, j: (j, 0, 0))],
        out_specs=pl.BlockSpec((1, n1, 2, tj, FOURIER_DIM), lambda b, j: (b, 0, 0, j, 0)),
        out_shape=jax.ShapeDtypeStruct((bsz, n1, 2, n2, FOURIER_DIM), BF16),
        scratch_shapes=[pltpu.VMEM((FOURIER_GROUPS, n1 * (tj + STAGE_PAD), LANES), F32)]
        + [pltpu.VMEM((FOURIER_GROUPS, tj * (n1 + STAGE_PAD), LANES), F32)] * 2,
        compiler_params=_params("parallel", "parallel"),
        name="dft_a",
    )(u4, f1, twc, tws)


def _dft_b_kernel(y_ref, m2_ref, wcs_ref, f_ref, x_scr, *, tk1, scale):
    m2 = m2_ref[...]
    for j in range(tk1):
        x = _dot(m2, y_ref[0, j].reshape(2 * FFT_N2, FOURIER_DIM))
        rows = slice(j * FFT_N2, (j + 1) * FFT_N2)
        x_scr[rows, :FOURIER_DIM] = x[:FFT_N2].astype(BF16)
        x_scr[rows, FOURIER_DIM:] = x[FFT_N2:].astype(BF16)
    f = _dot(x_scr[...], wcs_ref[...]) * scale
    f_ref[0] = f.reshape(tk1, FFT_N2, FOURIER_DIM).astype(BF16)


def _dft_b(y5, m2, wcs, seq):
    bsz, n1 = y5.shape[:2]
    tk1 = 8
    return pl.pallas_call(
        functools.partial(_dft_b_kernel, tk1=tk1, scale=(seq * FOURIER_GROUP_DIM) ** -0.5),
        grid=(bsz, n1 // tk1),
        in_specs=[pl.BlockSpec((1, tk1, 2, FFT_N2, FOURIER_DIM), lambda b, j: (b, j, 0, 0, 0)),
                  _full(m2.shape), _full(wcs.shape)],
        out_specs=pl.BlockSpec((1, tk1, FFT_N2, FOURIER_DIM), lambda b, j: (b, j, 0, 0)),
        out_shape=jax.ShapeDtypeStruct((bsz, n1, FFT_N2, FOURIER_DIM), BF16),
        scratch_shapes=[pltpu.VMEM((tk1 * FFT_N2, 2 * FOURIER_DIM), BF16)],
        compiler_params=_params("parallel", "parallel"),
        name="dft_b",
    )(y5, m2, wcs)


def _fourier_mix(u2d, bsz, seq, dft):
    n1 = seq // FFT_N2
    u4 = u2d.reshape(bsz, n1, FFT_N2, FOURIER_DIM)
    y5 = _dft_a(u4, dft["f1"], dft["twc"], dft["tws"])
    f4 = _dft_b(y5, dft["m2"], dft["wcs"], seq)
    return f4.transpose(0, 2, 1, 3).reshape(bsz * seq, FOURIER_DIM)


def _key_chunk(seq):
    fit = ATTN_SCORE_BYTES // (2 * 4 * min(ATTN_Q_TILE, seq))
    return min(ATTN_KEY_CHUNK, fit, seq // 2)


def _attn_kernel(q_ref, k_ref, vt_ref, o_ref, m_ref, acc_ref, s_ref, smax_ref, ot_ref, *, tk, nk):
    tq = q_ref.shape[2]
    strips = [slice(j * ATTN_STRIP, (j + 1) * ATTN_STRIP) for j in range(tq // ATTN_STRIP)]
    for hh in range(HEADS_PER_STEP):
        head = slice(hh * LANES, (hh + 1) * LANES)
        m_ref[...] = jnp.full_like(m_ref, -jnp.inf)
        acc_ref[...] = jnp.zeros_like(acc_ref)

        def scores(kc, slot, sl):
            ks = pl.multiple_of(kc * tk, tk)
            st = _dot(k_ref[0, pl.ds(ks, tk), head], q_ref[0, head, sl])
            s_ref[slot, :, sl] = st
            smax_ref[slot, :, sl] = jnp.max(st, axis=0, keepdims=True)

        def softmax_pv(kc, slot, sl):
            m_prev = m_ref[:, sl]
            m_new = jnp.maximum(m_prev, smax_ref[slot, :, sl])
            alpha = jnp.exp2(m_prev - m_new)
            p = jnp.exp2(s_ref[slot, :, sl] - m_new).astype(BF16)
            vt = vt_ref[0, kc, hh * V_ROWS:(hh + 1) * V_ROWS, :]
            acc_ref[:, sl] = alpha * acc_ref[:, sl] + _dot(vt, p)
            m_ref[:, sl] = m_new

        for sl in strips:
            scores(0, 0, sl)

        def body(j, carry):
            c = 2 * j
            for sl in strips:
                scores(c + 1, 1, sl)
                softmax_pv(c, 0, sl)
            for sl in strips:
                scores(jnp.minimum(c + 2, nk - 1), 0, sl)
                softmax_pv(c + 1, 1, sl)
            return carry

        lax.fori_loop(0, nk // 2, body, 0)
        acc = acc_ref[...]
        ot_ref[hh * V_HEAD_DIM:(hh + 1) * V_HEAD_DIM, :] = acc[:V_HEAD_DIM] / acc[ONES_ROW:ONES_ROW + 1, :]
    o_ref[0] = ot_ref[...].T.astype(BF16)


def _attention(qt3, k3, vt4):
    bsz, seq, _ = k3.shape
    tq = min(ATTN_Q_TILE, seq)
    tk = _key_chunk(seq)
    nk = seq // tk
    hp = HEADS_PER_STEP
    qt = pl.BlockSpec((1, hp * LANES, tq), lambda b, h, i: (b, h, i))
    qo = pl.BlockSpec((1, tq, hp * V_HEAD_DIM), lambda b, h, i: (b, i, h))
    kk = pl.BlockSpec((1, seq, hp * LANES), lambda b, h, i: (b, 0, h))
    vt = pl.BlockSpec((1, nk, hp * V_ROWS, tk), lambda b, h, i: (b, 0, h, 0))
    return pl.pallas_call(
        functools.partial(_attn_kernel, tk=tk, nk=nk),
        grid=(bsz, N_HEADS // hp, seq // tq),
        in_specs=[qt, kk, vt],
        out_specs=qo,
        out_shape=jax.ShapeDtypeStruct((bsz, seq, N_HEADS * V_HEAD_DIM), BF16),
        scratch_shapes=[pltpu.VMEM((1, tq), F32), pltpu.VMEM((V_ROWS, tq), F32), pltpu.VMEM((2, tk, tq), F32),
                        pltpu.VMEM((2, 1, tq), F32), pltpu.VMEM((hp * V_HEAD_DIM, tq), F32)],
        compiler_params=_params("parallel", "parallel", "arbitrary"),
        name="attn",
    )(qt3, k3, vt4)


def _merge_kernel(x_ref, f_ref, o_ref, gate_ref, wfo_ref, wao_ref, wout_ref, fg_ref, wrh_ref, wrl_ref, br_ref,
                  x1_ref, xn_ref, comb_ref, cnt_ref):
    y_f = _dot(f_ref[...], wfo_ref[...])
    y_a = _dot(o_ref[...], wao_ref[...])
    gate = gate_ref[...].astype(F32)
    merged = gate[:, :D_MODEL] * y_f + gate[:, D_MODEL:] * y_a
    x1 = x_ref[...] + _dot(merged.astype(BF16), wout_ref[...])
    x1_ref[...] = x1
    xn = _rms(x1, fg_ref[...])
    xh = xn.astype(BF16)
    xn_ref[...] = xh
    xl = (xn - xh.astype(F32)).astype(BF16)
    logit = _dot(xh, wrh_ref[...]) + _dot(xl, wrh_ref[...]) + _dot(xh, wrl_ref[...]) + br_ref[...]
    lane = lax.broadcasted_iota(jnp.int32, logit.shape, 1)
    neg = jnp.float32(-jnp.inf)

    def top(mask):
        val = jnp.max(jnp.where(mask, logit, neg), axis=1, keepdims=True)
        idx = jnp.min(jnp.where(mask & (logit == val), lane, ROUTER_LANES), axis=1, keepdims=True)
        return val, idx

    is_grp = lane < N_GROUPS
    g_max, g_idx = top(is_grp)
    g_sum = jnp.sum(jnp.where(is_grp, jnp.exp(logit - g_max), 0.0), axis=1, keepdims=True)
    grp_p = 1.0 / g_sum
    lo = EXPERT_LANE0 + g_idx * EXPERTS_PER_GROUP
    in_grp = (lane >= lo) & (lane < lo + EXPERTS_PER_GROUP)
    l1, i1 = top(in_grp)
    l2, i2 = top(in_grp & (lane != i1))
    e2 = jnp.exp(l2 - l1)
    w1 = 1.0 / (1.0 + e2)
    w2 = e2 / (1.0 + e2)
    comb = jnp.where(lane == i1, grp_p * w1, jnp.where(lane == i2, grp_p * w2, 0.0))
    comb_ref[...] = jnp.where(lane == GROUP_LANE, g_idx.astype(F32), comb)
    cnt = jnp.sum(jnp.where(lane == g_idx, 1.0, 0.0), axis=0, keepdims=True)
    cnt_ref[0] = jnp.broadcast_to(cnt, cnt_ref.shape[1:])


def _merge(x2d, f2d, o2d, gate2d, lw):
    t = x2d.shape[0]
    tm = min(MERGE_TILE, t)
    row = lambda w: pl.BlockSpec((tm, w), lambda i: (i, 0))
    weights = [lw["wfo"], lw["wao"], lw["wout"], lw["ffn_g"], lw["wr_hi"], lw["wr_lo"], lw["br"]]
    return pl.pallas_call(
        _merge_kernel,
        grid=(t // tm,),
        in_specs=[row(D_MODEL), row(FOURIER_DIM), row(N_HEADS * V_HEAD_DIM), row(2 * D_MODEL)]
        + [_full(w.shape) for w in weights],
        out_specs=[row(D_MODEL), row(D_MODEL), row(ROUTER_LANES),
                   pl.BlockSpec((1, SUBLANES, ROUTER_LANES), lambda i: (i, 0, 0))],
        out_shape=[jax.ShapeDtypeStruct((t, D_MODEL), F32), jax.ShapeDtypeStruct((t, D_MODEL), BF16),
                   jax.ShapeDtypeStruct((t, ROUTER_LANES), F32),
                   jax.ShapeDtypeStruct((t // tm, SUBLANES, ROUTER_LANES), F32)],
        compiler_params=_params("parallel"),
        name="merge",
    )(x2d, f2d, o2d, gate2d, *weights)


def _moe_kernel(cnt_ref, x1_ref, xn_ref, comb_ref, wg_ref, wu_ref, wd_ref, fin_ref, out_ref,
                acc_ref, dcol_ref, drow_ref, chi_ref, clo_ref, *, final_norm):
    i, g = pl.program_id(0), pl.program_id(1)
    tm = xn_ref.shape[0]

    def schedule(gg):
        n = cnt_ref[i * N_GROUPS + gg]
        full, rem = n // MOE_CHUNK, n % MOE_CHUNK
        plain = jnp.where(rem > 0, jnp.maximum(full - 1, 0), full)
        need = jnp.where(rem > 0, jnp.where(full > 0, MOE_CHUNK + rem, rem), 0)
        tail = jnp.int32(0)
        for size in reversed(MOE_TAILS):
            tail = jnp.where(need <= size, size, tail)
        return plain, jnp.where(need > 0, tail, 0)

    def rows(gg):
        plain, tail = schedule(gg)
        return plain * MOE_CHUNK + tail

    @pl.when(g == 0)
    def _():
        comb = comb_ref[...]
        lane = lax.broadcasted_iota(jnp.int32, comb.shape, 1)
        grp = comb[:, GROUP_LANE:GROUP_LANE + 1].astype(jnp.int32)
        onehot = jnp.where((lane == grp) & (lane < N_GROUPS), 1.0, 0.0)
        earlier = lax.broadcasted_iota(jnp.int32, (tm, tm), 0) > lax.broadcasted_iota(jnp.int32, (tm, tm), 1)
        before = _dot(jnp.where(earlier, 1.0, 0.0).astype(BF16), onehot.astype(BF16))
        dest = jnp.sum(onehot * before, axis=1, keepdims=True)
        first_row = jnp.int32(0)
        for gg in range(N_GROUPS):
            dest = dest + jnp.where(grp == gg, first_row.astype(F32), 0.0)
            first_row = first_row + rows(gg)
        dcol = jnp.broadcast_to(dest, comb.shape)
        dcol_ref[...] = dcol
        drow_ref[...] = dcol.T[:SUBLANES, :]
        w = jnp.where(lane == GROUP_LANE, 0.0, comb)
        chi = w.astype(BF16)
        chi_ref[...] = chi
        clo_ref[...] = (w - chi.astype(F32)).astype(BF16)
        acc_ref[...] = jnp.zeros_like(acc_ref)

    first_row = jnp.int32(0)
    for gg in range(N_GROUPS):
        first_row = first_row + jnp.where(gg < g, rows(gg), 0)
    plain, tail = schedule(g)

    def chunk(start, size):
        row0 = start.astype(F32)
        slot = lax.broadcasted_iota(jnp.int32, (size, tm), 0).astype(F32) + row0
        gather = jnp.where(drow_ref[0:1, :] == slot, 1.0, 0.0).astype(BF16)
        xs = _dot(gather, xn_ref[...]).astype(BF16)
        cs = _dot(gather, chi_ref[...]) + _dot(gather, clo_ref[...])
        lane = lax.broadcasted_iota(jnp.int32, cs.shape, 1)
        y = None
        for e in range(EXPERTS_PER_GROUP):
            c_e = jnp.sum(jnp.where(lane == EXPERT_LANE0 + g * EXPERTS_PER_GROUP + e, cs, 0.0), axis=1, keepdims=True)
            hg = _dot(xs, wg_ref[e])
            hu = _dot(xs, wu_ref[e])
            hs = (hg * jax.nn.sigmoid(hg) * hu * c_e).astype(BF16)
            ye = _dot(hs, wd_ref[e])
            y = ye if y is None else y + ye
        slot_t = lax.broadcasted_iota(jnp.int32, (tm, size), 1).astype(F32) + row0
        scatter = jnp.where(dcol_ref[:, 0:1] == slot_t, 1.0, 0.0).astype(BF16)
        acc_ref[...] += _dot(scatter, y.astype(BF16))

    def plain_chunk(c, carry):
        chunk(first_row + c * MOE_CHUNK, MOE_CHUNK)
        return carry

    lax.fori_loop(0, plain, plain_chunk, 0)
    for size in MOE_TAILS:
        @pl.when(tail == size)
        def _(size=size):
            chunk(first_row + plain * MOE_CHUNK, size)

    @pl.when(g == pl.num_programs(1) - 1)
    def _():
        x2 = x1_ref[...] + acc_ref[...]
        out_ref[...] = _rms(x2, fin_ref[...]) if final_norm else x2


def _moe(x1, xn, comb, cnt, lw, final_g, final_norm):
    t = x1.shape[0]
    tm = min(MOE_TILE, t)
    per = tm // min(MERGE_TILE, t)
    counts = cnt[:, 0, :N_GROUPS].reshape(t // tm, per, N_GROUPS).sum(axis=1).astype(jnp.int32).reshape(-1)
    row = lambda w: pl.BlockSpec((tm, w), lambda i, g, c: (i, 0))
    wspec = lambda a, b: pl.BlockSpec((EXPERTS_PER_GROUP, a, b), lambda i, g, c: (g, 0, 0))
    grid_spec = pltpu.PrefetchScalarGridSpec(
        num_scalar_prefetch=1,
        grid=(t // tm, N_GROUPS),
        in_specs=[row(D_MODEL), row(D_MODEL), row(ROUTER_LANES), wspec(D_MODEL, D_EXPERT), wspec(D_MODEL, D_EXPERT),
                  wspec(D_EXPERT, D_MODEL), pl.BlockSpec((1, D_MODEL), lambda i, g, c: (0, 0))],
        out_specs=row(D_MODEL),
        scratch_shapes=[pltpu.VMEM((tm, D_MODEL), F32), pltpu.VMEM((tm, ROUTER_LANES), F32),
                        pltpu.VMEM((SUBLANES, tm), F32), pltpu.VMEM((tm, ROUTER_LANES), BF16),
                        pltpu.VMEM((tm, ROUTER_LANES), BF16)],
    )
    return pl.pallas_call(
        functools.partial(_moe_kernel, final_norm=final_norm),
        grid_spec=grid_spec,
        out_shape=jax.ShapeDtypeStruct((t, D_MODEL), F32),
        compiler_params=_params("parallel", "arbitrary"),
        name="moe",
    )(counts, x1, xn, comb, lw["wg"], lw["wu"], lw["wd"], final_g)


def _dft_angles(n):
    return 2.0 * np.pi * ((np.arange(n)[:, None] * np.arange(n)[None, :]) % n) / n


def _dft_tables(seq):
    n1, n2, n = seq // FFT_N2, FFT_N2, FOURIER_GROUP_DIM
    a1, a2, ac = _dft_angles(n1), _dft_angles(n2), _dft_angles(n)
    f1 = np.concatenate([np.cos(a1), -np.sin(a1)], axis=0)
    m2 = np.block([[np.cos(a2), np.sin(a2)], [-np.sin(a2), np.cos(a2)]])
    wcs = np.zeros((2 * FOURIER_DIM, FOURIER_DIM), np.float64)
    for g in range(FOURIER_GROUPS):
        sl = slice(g * n, (g + 1) * n)
        wcs[sl, sl] = np.cos(ac)
        wcs[FOURIER_DIM + g * n:FOURIER_DIM + (g + 1) * n, sl] = np.sin(ac)
    idx = (jnp.arange(n2, dtype=jnp.int32)[:, None] * jnp.arange(n1, dtype=jnp.int32)[None, :]) % seq
    ang = jnp.broadcast_to((idx.astype(F32) * (2.0 * math.pi / seq))[:, :, None], (n2, n1, LANES))
    return {"f1": jnp.asarray(f1, BF16), "m2": jnp.asarray(m2, BF16), "wcs": jnp.asarray(wcs, BF16),
            "twc": jnp.cos(ang), "tws": jnp.sin(ang)}


def _rope_key_placement():
    ekr = np.zeros((LANES, HEAD_PAD), np.float32)
    for h in range(N_HEADS):
        for r in range(QK_ROPE_DIM):
            ekr[r, h * LANES + QK_NOPE_DIM + r] = 1.0
    return jnp.asarray(ekr, BF16)


def _rope_tables(seq):
    inv = 1.0 / (ROPE_BASE ** (jnp.arange(0, QK_ROPE_DIM, 2, dtype=F32) / QK_ROPE_DIM))
    rang = jnp.arange(seq, dtype=F32)[:, None] * inv[None, :]
    c, s = jnp.cos(rang), jnp.sin(rang)
    z = lambda w: jnp.zeros((seq, w), F32)
    tail = LANES - QK_DIM
    cos = jnp.concatenate([jnp.ones((seq, QK_NOPE_DIM), F32), c, c, z(tail)], axis=1)
    sin1 = jnp.concatenate([z(QK_NOPE_DIM), -s, z(ROPE_HALF), z(tail)], axis=1)
    sin2 = jnp.concatenate([z(QK_NOPE_DIM), z(ROPE_HALF), s, z(tail)], axis=1)
    return cos, sin1, sin2


def _pad_heads(w, per_head, lo, hi, width=LANES):
    r = w.shape[0]
    w3 = w.reshape(r, N_HEADS, per_head)[:, :, lo:hi]
    return jnp.pad(w3, ((0, 0), (0, 0), (0, width - (hi - lo)))).reshape(r, N_HEADS * width)


def _layer_weights(l, attn_norm_g, w_in, q_norm_g, kv_norm_g, w_uq, w_ukv, w_fourier_out, w_attn_out, w_out,
                   ffn_norm_g, w_grp, b_grp, w_exp, b_exp, w_gate, w_up, w_down):
    s0, s1, s2, s3 = (FOURIER_DIM, FOURIER_DIM + Q_LORA_RANK, FOURIER_DIM + Q_LORA_RANK + KV_LORA_RANK,
                      FOURIER_DIM + Q_LORA_RANK + KV_LORA_RANK + QK_ROPE_DIM)
    wi = w_in[l]
    wr = jnp.concatenate([w_grp[l], w_exp[l]], axis=1)
    wr = jnp.pad(wr, ((0, 0), (0, ROUTER_LANES - wr.shape[1])))
    wr_hi = wr.astype(BF16)
    br = jnp.concatenate([b_grp[l], b_exp[l]])
    return {
        "attn_g": attn_norm_g[l][None, :],
        "wf": wi[:, :s0].astype(BF16),
        "wql": wi[:, s0:s1].astype(BF16),
        "wkvl": wi[:, s1:s2].astype(BF16),
        "wkr": jnp.pad(wi[:, s2:s3], ((0, 0), (0, LANES - QK_ROPE_DIM))).astype(BF16),
        "wgate": wi[:, s3:].astype(BF16),
        "qg": q_norm_g[l][None, :],
        "kvg": kv_norm_g[l][None, :],
        "wuqt": _pad_heads(w_uq[l], QK_DIM, 0, QK_DIM).T.astype(BF16),
        "wuk": _pad_heads(w_ukv[l], QK_NOPE_DIM + V_HEAD_DIM, 0, QK_NOPE_DIM).astype(BF16),
        "wuvt": _pad_heads(w_ukv[l], QK_NOPE_DIM + V_HEAD_DIM, QK_NOPE_DIM, QK_NOPE_DIM + V_HEAD_DIM,
                           V_ROWS).T.astype(BF16),
        "ekr": _rope_key_placement(),
        "wfo": w_fourier_out[l].astype(BF16),
        "wao": w_attn_out[l].astype(BF16),
        "wout": w_out[l].astype(BF16),
        "ffn_g": ffn_norm_g[l][None, :],
        "wr_hi": wr_hi,
        "wr_lo": (wr - wr_hi.astype(F32)).astype(BF16),
        "br": jnp.pad(br, (0, ROUTER_LANES - br.shape[0]))[None, :],
        "wg": w_gate[l].astype(BF16),
        "wu": w_up[l].astype(BF16),
        "wd": w_down[l].astype(BF16),
    }


def _trunk(x, layers, final_g, tables):
    bsz, seq, d = x.shape
    t = bsz * seq
    dft, rope = tables
    x2d = x.reshape(t, d)
    for l, lw in enumerate(layers):
        u, k, gate, qt, vt = _in_proj(x2d, seq, lw, rope)
        shp = lambda z: z.reshape(bsz, seq, z.shape[-1])
        f = _fourier_mix(u, bsz, seq, dft)
        o = _attention(qt, shp(k), vt).reshape(t, N_HEADS * V_HEAD_DIM)
        x1, xn, comb, cnt = _merge(x2d, f, o, gate, lw)
        x2d = _moe(x1, xn, comb, cnt, lw, final_g, final_norm=(l == len(layers) - 1))
    return x2d.reshape(bsz, seq, d)


def kernel(x_prompt, x_sample, attn_norm_g, w_in, q_norm_g, kv_norm_g, w_uq, w_ukv, w_fourier_out, w_attn_out, w_out, ffn_norm_g, w_grp, b_grp, w_exp, b_exp, w_gate, w_up, w_down, final_norm_g):
    depth = w_in.shape[0]
    layers = [_layer_weights(l, attn_norm_g, w_in, q_norm_g, kv_norm_g, w_uq, w_ukv, w_fourier_out, w_attn_out,
                             w_out, ffn_norm_g, w_grp, b_grp, w_exp, b_exp, w_gate, w_up, w_down)
              for l in range(depth)]
    final_g = final_norm_g[None, :]
    outs = []
    for x in (x_prompt, x_sample):
        seq = x.shape[1]
        outs.append(_trunk(x, layers, final_g, (_dft_tables(seq), _rope_tables(seq))))
    return tuple(outs)
```

```python
import functools
import math

import numpy as np
import jax
import jax.numpy as jnp
from jax import lax
from jax.experimental import pallas as pl
from jax.experimental.pallas import tpu as pltpu

D_MODEL = 1024
FOURIER_GROUPS = 4
FOURIER_GROUP_DIM = 128
FOURIER_DIM = FOURIER_GROUPS * FOURIER_GROUP_DIM
N_HEADS = 8
QK_NOPE_DIM = 64
QK_ROPE_DIM = 32
V_HEAD_DIM = 64
Q_LORA_RANK = 384
KV_LORA_RANK = 256
QK_DIM = QK_NOPE_DIM + QK_ROPE_DIM
ROPE_BASE = 10000.0
N_GROUPS = 4
EXPERTS_PER_GROUP = 8
N_EXPERTS = N_GROUPS * EXPERTS_PER_GROUP
D_EXPERT = 256
EPS = 1e-6

LANES = 128
HEAD_PAD = N_HEADS * LANES
ROPE_HALF = QK_ROPE_DIM // 2
ONES_ROW = V_HEAD_DIM
V_ROWS = 80
FFT_N2 = 128
DFT_A_POSITIONS = 16
STAGE_PAD = 8
ROUTER_LANES = LANES
EXPERT_LANE0 = N_GROUPS
GROUP_LANE = 0
SUBLANES = 8
ATTN_Q_TILE = 8192
ATTN_STRIP = 256
ATTN_KEY_CHUNK = 512
ATTN_SCORE_BYTES = 16 * 1024 * 1024
HEADS_PER_STEP = LANES // V_HEAD_DIM
IN_PROJ_TILE = 1024
MERGE_TILE = 1024
MOE_TILE = 1024
MOE_CHUNK = 256
MOE_TAILS = (256, 320, 512)
VMEM_LIMIT = 56 * 1024 * 1024

BF16 = jnp.bfloat16
F32 = jnp.float32


def _dot(a, b):
    return jnp.dot(a, b, preferred_element_type=F32)


def _rms(x, g):
    return x * lax.rsqrt(jnp.mean(x * x, axis=-1, keepdims=True) + EPS) * g


def _params(*sem):
    return pltpu.CompilerParams(dimension_semantics=sem, vmem_limit_bytes=VMEM_LIMIT)


def _full(shape):
    return pl.BlockSpec(shape, lambda *_: (0,) * len(shape))


def _in_proj_kernel(x_ref, g_ref, wf_ref, wql_ref, wkvl_ref, wkr_ref, wgate_ref, qg_ref, kvg_ref,
                    wuqt_ref, wuk_ref, wuvt_ref, ekr_ref, cos_ref, sin1_ref, sin2_ref, cost_ref, sin1t_ref, sin2t_ref,
                    u_ref, k_ref, gate_ref, qt_ref, vt_ref, *, q_scale):
    xb = _rms(x_ref[...], g_ref[...]).astype(BF16)
    u_ref[...] = _dot(xb, wf_ref[...]).astype(BF16)
    qn = _rms(_dot(xb, wql_ref[...]), qg_ref[...]).astype(BF16)
    kvn = _rms(_dot(xb, wkvl_ref[...]), kvg_ref[...]).astype(BF16)
    ukr = _dot(xb, wkr_ref[...]).astype(BF16)
    k = _dot(kvn, wuk_ref[...]) + _dot(ukr, ekr_ref[...])
    nt = (((1,), (1,)), ((), ()))
    qt = lax.dot_general(wuqt_ref[...], qn, nt, preferred_element_type=F32)
    vt = lax.dot_general(wuvt_ref[...], kvn, nt, preferred_element_type=F32)
    head_row = lax.broadcasted_iota(jnp.int32, vt.shape, 0) % V_ROWS
    vt = jnp.where(head_row == ONES_ROW, 1.0, vt).astype(BF16)
    tk = vt_ref.shape[-1]
    for c in range(vt_ref.shape[1]):
        vt_ref[0, c] = vt[:, c * tk:(c + 1) * tk]
    cos, sin1, sin2 = cos_ref[...], sin1_ref[...], sin2_ref[...]
    cost, sin1t, sin2t = cost_ref[...], sin1t_ref[...], sin2t_ref[...]
    k_up = pltpu.roll(k, HEAD_PAD - ROPE_HALF, 1)
    k_dn = pltpu.roll(k, ROPE_HALF, 1)
    qt_up = jnp.concatenate([qt[ROPE_HALF:], qt[:ROPE_HALF]], axis=0)
    qt_dn = jnp.concatenate([qt[-ROPE_HALF:], qt[:-ROPE_HALF]], axis=0)
    for h in range(N_HEADS):
        sl = slice(h * LANES, (h + 1) * LANES)
        k_ref[:, sl] = (k[:, sl] * cos + k_up[:, sl] * sin1 + k_dn[:, sl] * sin2).astype(BF16)
        qt_ref[0, sl, :] = ((qt[sl] * cost + qt_up[sl] * sin1t + qt_dn[sl] * sin2t) * q_scale).astype(BF16)
    gate_ref[...] = jax.nn.sigmoid(_dot(xb, wgate_ref[...])).astype(BF16)


def _in_proj(x2d, seq, lw, rope):
    t = x2d.shape[0]
    tk = _key_chunk(seq)
    tm = max(tk, min(IN_PROJ_TILE, seq))
    n_seq_tiles = seq // tm
    row = lambda w: pl.BlockSpec((tm, w), lambda i: (i, 0))
    pos = pl.BlockSpec((tm, LANES), lambda i: (i % n_seq_tiles, 0))
    pos_t = pl.BlockSpec((LANES, tm), lambda i: (0, i % n_seq_tiles))
    weights = [lw["attn_g"], lw["wf"], lw["wql"], lw["wkvl"], lw["wkr"], lw["wgate"], lw["qg"], lw["kvg"],
               lw["wuqt"], lw["wuk"], lw["wuvt"], lw["ekr"]]
    out_w = [FOURIER_DIM, HEAD_PAD, 2 * D_MODEL]
    qt_spec = pl.BlockSpec((1, HEAD_PAD, tm), lambda i: (i // n_seq_tiles, 0, i % n_seq_tiles))
    qt_shape = jax.ShapeDtypeStruct((t // seq, HEAD_PAD, seq), BF16)
    vt_spec = pl.BlockSpec((1, tm // tk, N_HEADS * V_ROWS, tk), lambda i: (i // n_seq_tiles, i % n_seq_tiles, 0, 0))
    vt_shape = jax.ShapeDtypeStruct((t // seq, seq // tk, N_HEADS * V_ROWS, tk), BF16)
    return pl.pallas_call(
        functools.partial(_in_proj_kernel, q_scale=(QK_DIM ** -0.5) * math.log2(math.e)),
        grid=(t // tm,),
        in_specs=[row(D_MODEL)] + [_full(w.shape) for w in weights] + [pos] * 3 + [pos_t] * 3,
        out_specs=[row(w) for w in out_w] + [qt_spec, vt_spec],
        out_shape=[jax.ShapeDtypeStruct((t, w), BF16) for w in out_w] + [qt_shape, vt_shape],
        compiler_params=_params("parallel"),
        name="in_proj",
    )(x2d, *weights, *rope, *[r.T for r in rope])


def _dft_a_kernel(u_ref, f1_ref, twc_ref, tws_ref, y_ref, u_scr, yr_scr, yi_scr, *, tj, n1):
    f1 = f1_ref[...]
    groups = [slice(g * LANES, (g + 1) * LANES) for g in range(FOURIER_GROUPS)]

    def put(scr, rows, val):
        for g, lanes in enumerate(groups):
            scr[g, rows, :] = val[:, lanes]

    def strided(scr, start, size, stride):
        return jnp.concatenate([scr[g, pl.ds(start, size, stride=stride), :] for g in range(FOURIER_GROUPS)], axis=1)

    up, yp = tj + STAGE_PAD, n1 + STAGE_PAD
    for i in range(n1):
        put(u_scr, slice(i * up, i * up + tj), u_ref[0, i].astype(F32))
    for j in range(tj):
        u_j = strided(u_scr, j, n1, up).astype(BF16)
        y = _dot(f1, u_j)
        yr, yi = y[:n1], y[n1:]
        c = jnp.concatenate([twc_ref[j]] * FOURIER_GROUPS, axis=1)
        s = jnp.concatenate([tws_ref[j]] * FOURIER_GROUPS, axis=1)
        put(yr_scr, slice(j * yp, j * yp + n1), yr * c + yi * s)
        put(yi_scr, slice(j * yp, j * yp + n1), yi * c - yr * s)
    for k1 in range(n1):
        y_ref[0, k1, 0] = strided(yr_scr, k1, tj, yp).astype(BF16)
        y_ref[0, k1, 1] = strided(yi_scr, k1, tj, yp).astype(BF16)


def _dft_a(u4, f1, twc, tws):
    bsz, n1, n2, _ = u4.shape
    tj = DFT_A_POSITIONS
    return pl.pallas_call(
        functools.partial(_dft_a_kernel, tj=tj, n1=n1),
        grid=(bsz, n2 // tj),
        in_specs=[pl.BlockSpec((1, n1, tj, FOURIER_DIM), lambda b, j: (b, 0, j, 0)), _full(f1.shape),
                  pl.BlockSpec((tj, n1, LANES), lambda b, j: (j, 0, 0)),
                  pl.BlockSpec((tj, n1, LANES), lambda b, j: (j, 0, 0))],
        out_specs=pl.BlockSpec((1, n1, 2, tj, FOURIER_DIM), lambda b, j: (b, 0, 0, j, 0)),
        out_shape=jax.ShapeDtypeStruct((bsz, n1, 2, n2, FOURIER_DIM), BF16),
        scratch_shapes=[pltpu.VMEM((FOURIER_GROUPS, n1 * (tj + STAGE_PAD), LANES), F32)]
        + [pltpu.VMEM((FOURIER_GROUPS, tj * (n1 + STAGE_PAD), LANES), F32)] * 2,
        compiler_params=_params("parallel", "parallel"),
        name="dft_a",
    )(u4, f1, twc, tws)


def _dft_b_kernel(y_ref, m2_ref, wcs_ref, f_ref, x_scr, *, tk1, scale):
    m2 = m2_ref[...]
    for j in range(tk1):
        x = _dot(m2, y_ref[0, j].reshape(2 * FFT_N2, FOURIER_DIM))
        rows = slice(j * FFT_N2, (j + 1) * FFT_N2)
        for g in range(FOURIER_GROUPS):
            lanes = slice(g * FOURIER_GROUP_DIM, (g + 1) * FOURIER_GROUP_DIM)
            x_scr[rows, 2 * g * FOURIER_GROUP_DIM:(2 * g + 1) * FOURIER_GROUP_DIM] = x[:FFT_N2, lanes].astype(BF16)
            x_scr[rows, (2 * g + 1) * FOURIER_GROUP_DIM:(2 * g + 2) * FOURIER_GROUP_DIM] = x[FFT_N2:, lanes].astype(BF16)
    wcs = wcs_ref[...]
    f = jnp.concatenate([_dot(x_scr[:, 2 * g * FOURIER_GROUP_DIM:(2 * g + 2) * FOURIER_GROUP_DIM], wcs)
                         for g in range(FOURIER_GROUPS)], axis=1) * scale
    f_ref[0] = f.reshape(tk1, FFT_N2, FOURIER_DIM).astype(BF16)


def _dft_b(y5, m2, wcs, seq):
    bsz, n1 = y5.shape[:2]
    tk1 = 8
    return pl.pallas_call(
        functools.partial(_dft_b_kernel, tk1=tk1, scale=(seq * FOURIER_GROUP_DIM) ** -0.5),
        grid=(bsz, n1 // tk1),
        in_specs=[pl.BlockSpec((1, tk1, 2, FFT_N2, FOURIER_DIM), lambda b, j: (b, j, 0, 0, 0)),
                  _full(m2.shape), _full(wcs.shape)],
        out_specs=pl.BlockSpec((1, tk1, FFT_N2, FOURIER_DIM), lambda b, j: (b, j, 0, 0)),
        out_shape=jax.ShapeDtypeStruct((bsz, n1, FFT_N2, FOURIER_DIM), BF16),
        scratch_shapes=[pltpu.VMEM((tk1 * FFT_N2, 2 * FOURIER_DIM), BF16)],
        compiler_params=_params("parallel", "parallel"),
        name="dft_b",
    )(y5, m2, wcs)


def _fourier_mix(u2d, bsz, seq, dft):
    n1 = seq // FFT_N2
    u4 = u2d.reshape(bsz, n1, FFT_N2, FOURIER_DIM)
    y5 = _dft_a(u4, dft["f1"], dft["twc"], dft["tws"])
    f4 = _dft_b(y5, dft["m2"], dft["wcs"], seq)
    return f4.transpose(0, 2, 1, 3).reshape(bsz * seq, FOURIER_DIM)


def _key_chunk(seq):
    fit = ATTN_SCORE_BYTES // (2 * 4 * min(ATTN_Q_TILE, seq))
    return min(ATTN_KEY_CHUNK, fit, seq // 2)


def _attn_kernel(q_ref, k_ref, vt_ref, o_ref, m_ref, acc_ref, s_ref, smax_ref, ot_ref, *, tk, nk):
    tq = q_ref.shape[2]
    strips = [slice(j * ATTN_STRIP, (j + 1) * ATTN_STRIP) for j in range(tq // ATTN_STRIP)]
    for hh in range(HEADS_PER_STEP):
        head = slice(hh * LANES, (hh + 1) * LANES)
        m_ref[...] = jnp.full_like(m_ref, -jnp.inf)
        acc_ref[...] = jnp.zeros_like(acc_ref)

        def scores(kc, slot, sl):
            ks = pl.multiple_of(kc * tk, tk)
            st = _dot(k_ref[0, pl.ds(ks, tk), head], q_ref[0, head, sl])
            s_ref[slot, :, sl] = st
            smax_ref[slot, :, sl] = jnp.max(st, axis=0, keepdims=True)

        def softmax_pv(kc, slot, sl):
            m_prev = m_ref[:, sl]
            m_new = jnp.maximum(m_prev, smax_ref[slot, :, sl])
            alpha = jnp.exp2(m_prev - m_new)
            p = jnp.exp2(s_ref[slot, :, sl] - m_new).astype(BF16)
            vt = vt_ref[0, kc, hh * V_ROWS:(hh + 1) * V_ROWS, :]
            acc_ref[:, sl] = alpha * acc_ref[:, sl] + _dot(vt, p)
            m_ref[:, sl] = m_new

        for sl in strips:
            scores(0, 0, sl)

        def body(j, carry):
            c = 2 * j
            for sl in strips:
                scores(c + 1, 1, sl)
                softmax_pv(c, 0, sl)
            for sl in strips:
                scores(jnp.minimum(c + 2, nk - 1), 0, sl)
                softmax_pv(c + 1, 1, sl)
            return carry

        lax.fori_loop(0, nk // 2, body, 0)
        acc = acc_ref[...]
        ot_ref[hh * V_HEAD_DIM:(hh + 1) * V_HEAD_DIM, :] = acc[:V_HEAD_DIM] / acc[ONES_ROW:ONES_ROW + 1, :]
    o_ref[0] = ot_ref[...].T.astype(BF16)


def _attention(qt3, k3, vt4):
    bsz, seq, _ = k3.shape
    tq = min(ATTN_Q_TILE, seq)
    tk = _key_chunk(seq)
    nk = seq // tk
    hp = HEADS_PER_STEP
    qt = pl.BlockSpec((1, hp * LANES, tq), lambda b, h, i: (b, h, i))
    qo = pl.BlockSpec((1, tq, hp * V_HEAD_DIM), lambda b, h, i: (b, i, h))
    kk = pl.BlockSpec((1, seq, hp * LANES), lambda b, h, i: (b, 0, h))
    vt = pl.BlockSpec((1, nk, hp * V_ROWS, tk), lambda b, h, i: (b, 0, h, 0))
    return pl.pallas_call(
        functools.partial(_attn_kernel, tk=tk, nk=nk),
        grid=(bsz, N_HEADS // hp, seq // tq),
        in_specs=[qt, kk, vt],
        out_specs=qo,
        out_shape=jax.ShapeDtypeStruct((bsz, seq, N_HEADS * V_HEAD_DIM), BF16),
        scratch_shapes=[pltpu.VMEM((1, tq), F32), pltpu.VMEM((V_ROWS, tq), F32), pltpu.VMEM((2, tk, tq), F32),
                        pltpu.VMEM((2, 1, tq), F32), pltpu.VMEM((hp * V_HEAD_DIM, tq), F32)],
        compiler_params=_params("parallel", "parallel", "arbitrary"),
        name="attn",
    )(qt3, k3, vt4)


def _merge_kernel(x_ref, f_ref, o_ref, gate_ref, wfo_ref, wao_ref, wout_ref, fg_ref, wrh_ref, wrl_ref, br_ref,
                  x1_ref, xn_ref, comb_ref, cnt_ref):
    y_f = _dot(f_ref[...], wfo_ref[...])
    y_a = _dot(o_ref[...], wao_ref[...])
    gate = gate_ref[...].astype(F32)
    merged = gate[:, :D_MODEL] * y_f + gate[:, D_MODEL:] * y_a
    x1 = x_ref[...] + _dot(merged.astype(BF16), wout_ref[...])
    x1_ref[...] = x1
    xn = _rms(x1, fg_ref[...])
    xh = xn.astype(BF16)
    xn_ref[...] = xh
    xl = (xn - xh.astype(F32)).astype(BF16)
    logit = _dot(xh, wrh_ref[...]) + _dot(xl, wrh_ref[...]) + _dot(xh, wrl_ref[...]) + br_ref[...]
    lane = lax.broadcasted_iota(jnp.int32, logit.shape, 1)
    neg = jnp.float32(-jnp.inf)

    def top(mask):
        val = jnp.max(jnp.where(mask, logit, neg), axis=1, keepdims=True)
        idx = jnp.min(jnp.where(mask & (logit == val), lane, ROUTER_LANES), axis=1, keepdims=True)
        return val, idx

    is_grp = lane < N_GROUPS
    g_max, g_idx = top(is_grp)
    g_sum = jnp.sum(jnp.where(is_grp, jnp.exp(logit - g_max), 0.0), axis=1, keepdims=True)
    grp_p = 1.0 / g_sum
    lo = EXPERT_LANE0 + g_idx * EXPERTS_PER_GROUP
    in_grp = (lane >= lo) & (lane < lo + EXPERTS_PER_GROUP)
    l1, i1 = top(in_grp)
    l2, i2 = top(in_grp & (lane != i1))
    e2 = jnp.exp(l2 - l1)
    w1 = 1.0 / (1.0 + e2)
    w2 = e2 / (1.0 + e2)
    comb = jnp.where(lane == i1, grp_p * w1, jnp.where(lane == i2, grp_p * w2, 0.0))
    comb_ref[...] = jnp.where(lane == GROUP_LANE, g_idx.astype(F32), comb)
    cnt = jnp.sum(jnp.where(lane == g_idx, 1.0, 0.0), axis=0, keepdims=True)
    cnt_ref[0] = jnp.broadcast_to(cnt, cnt_ref.shape[1:])


def _merge(x2d, f2d, o2d, gate2d, lw):
    t = x2d.shape[0]
    tm = min(MERGE_TILE, t)
    row = lambda w: pl.BlockSpec((tm, w), lambda i: (i, 0))
    weights = [lw["wfo"], lw["wao"], lw["wout"], lw["ffn_g"], lw["wr_hi"], lw["wr_lo"], lw["br"]]
    return pl.pallas_call(
        _merge_kernel,
        grid=(t // tm,),
        in_specs=[row(D_MODEL), row(FOURIER_DIM), row(N_HEADS * V_HEAD_DIM), row(2 * D_MODEL)]
        + [_full(w.shape) for w in weights],
        out_specs=[row(D_MODEL), row(D_MODEL), row(ROUTER_LANES),
                   pl.BlockSpec((1, SUBLANES, ROUTER_LANES), lambda i: (i, 0, 0))],
        out_shape=[jax.ShapeDtypeStruct((t, D_MODEL), F32), jax.ShapeDtypeStruct((t, D_MODEL), BF16),
                   jax.ShapeDtypeStruct((t, ROUTER_LANES), F32),
                   jax.ShapeDtypeStruct((t // tm, SUBLANES, ROUTER_LANES), F32)],
        compiler_params=_params("parallel"),
        name="merge",
    )(x2d, f2d, o2d, gate2d, *weights)


def _moe_kernel(cnt_ref, x1_ref, xn_ref, comb_ref, wg_ref, wu_ref, wd_ref, fin_ref, out_ref,
                acc_ref, dcol_ref, drow_ref, chi_ref, clo_ref, *, final_norm):
    i, g = pl.program_id(0), pl.program_id(1)
    tm = xn_ref.shape[0]

    def schedule(gg):
        n = cnt_ref[i * N_GROUPS + gg]
        full, rem = n // MOE_CHUNK, n % MOE_CHUNK
        plain = jnp.where(rem > 0, jnp.maximum(full - 1, 0), full)
        need = jnp.where(rem > 0, jnp.where(full > 0, MOE_CHUNK + rem, rem), 0)
        tail = jnp.int32(0)
        for size in reversed(MOE_TAILS):
            tail = jnp.where(need <= size, size, tail)
        return plain, jnp.where(need > 0, tail, 0)

    def rows(gg):
        plain, tail = schedule(gg)
        return plain * MOE_CHUNK + tail

    @pl.when(g == 0)
    def _():
        comb = comb_ref[...]
        lane = lax.broadcasted_iota(jnp.int32, comb.shape, 1)
        grp = comb[:, GROUP_LANE:GROUP_LANE + 1].astype(jnp.int32)
        onehot = jnp.where((lane == grp) & (lane < N_GROUPS), 1.0, 0.0)
        earlier = lax.broadcasted_iota(jnp.int32, (tm, tm), 0) > lax.broadcasted_iota(jnp.int32, (tm, tm), 1)
        before = _dot(jnp.where(earlier, 1.0, 0.0).astype(BF16), onehot.astype(BF16))
        dest = jnp.sum(onehot * before, axis=1, keepdims=True)
        first_row = jnp.int32(0)
        for gg in range(N_GROUPS):
            dest = dest + jnp.where(grp == gg, first_row.astype(F32), 0.0)
            first_row = first_row + rows(gg)
        dcol = jnp.broadcast_to(dest, comb.shape)
        dcol_ref[...] = dcol
        drow_ref[...] = dcol.T[:SUBLANES, :]
        w = jnp.where(lane == GROUP_LANE, 0.0, comb)
        chi = w.astype(BF16)
        chi_ref[...] = chi
        clo_ref[...] = (w - chi.astype(F32)).astype(BF16)
        acc_ref[...] = jnp.zeros_like(acc_ref)

    first_row = jnp.int32(0)
    for gg in range(N_GROUPS):
        first_row = first_row + jnp.where(gg < g, rows(gg), 0)
    plain, tail = schedule(g)

    def chunk(start, size):
        row0 = start.astype(F32)
        slot = lax.broadcasted_iota(jnp.int32, (size, tm), 0).astype(F32) + row0
        gather = jnp.where(drow_ref[0:1, :] == slot, 1.0, 0.0).astype(BF16)
        xs = _dot(gather, xn_ref[...]).astype(BF16)
        cs = _dot(gather, chi_ref[...]) + _dot(gather, clo_ref[...])
        lane = lax.broadcasted_iota(jnp.int32, cs.shape, 1)
        y = None
        for e in range(EXPERTS_PER_GROUP):
            c_e = jnp.sum(jnp.where(lane == EXPERT_LANE0 + g * EXPERTS_PER_GROUP + e, cs, 0.0), axis=1, keepdims=True)
            hg = _dot(xs, wg_ref[e])
            hu = _dot(xs, wu_ref[e])
            hs = (hg * jax.nn.sigmoid(hg) * hu * c_e).astype(BF16)
            ye = _dot(hs, wd_ref[e])
            y = ye if y is None else y + ye
        slot_t = lax.broadcasted_iota(jnp.int32, (tm, size), 1).astype(F32) + row0
        scatter = jnp.where(dcol_ref[:, 0:1] == slot_t, 1.0, 0.0).astype(BF16)
        acc_ref[...] += _dot(scatter, y.astype(BF16))

    def plain_chunk(c, carry):
        chunk(first_row + c * MOE_CHUNK, MOE_CHUNK)
        return carry

    lax.fori_loop(0, plain, plain_chunk, 0)
    for size in MOE_TAILS:
        @pl.when(tail == size)
        def _(size=size):
            chunk(first_row + plain * MOE_CHUNK, size)

    @pl.when(g == pl.num_programs(1) - 1)
    def _():
        x2 = x1_ref[...] + acc_ref[...]
        out_ref[...] = _rms(x2, fin_ref[...]) if final_norm else x2


def _moe(x1, xn, comb, cnt, lw, final_g, final_norm):
    t = x1.shape[0]
    tm = min(MOE_TILE, t)
    per = tm // min(MERGE_TILE, t)
    counts = cnt[:, 0, :N_GROUPS].reshape(t // tm, per, N_GROUPS).sum(axis=1).astype(jnp.int32).reshape(-1)
    row = lambda w: pl.BlockSpec((tm, w), lambda i, g, c: (i, 0))
    wspec = lambda a, b: pl.BlockSpec((EXPERTS_PER_GROUP, a, b), lambda i, g, c: (g, 0, 0))
    grid_spec = pltpu.PrefetchScalarGridSpec(
        num_scalar_prefetch=1,
        grid=(t // tm, N_GROUPS),
        in_specs=[row(D_MODEL), row(D_MODEL), row(ROUTER_LANES), wspec(D_MODEL, D_EXPERT), wspec(D_MODEL, D_EXPERT),
                  wspec(D_EXPERT, D_MODEL), pl.BlockSpec((1, D_MODEL), lambda i, g, c: (0, 0))],
        out_specs=row(D_MODEL),
        scratch_shapes=[pltpu.VMEM((tm, D_MODEL), F32), pltpu.VMEM((tm, ROUTER_LANES), F32),
                        pltpu.VMEM((SUBLANES, tm), F32), pltpu.VMEM((tm, ROUTER_LANES), BF16),
                        pltpu.VMEM((tm, ROUTER_LANES), BF16)],
    )
    return pl.pallas_call(
        functools.partial(_moe_kernel, final_norm=final_norm),
        grid_spec=grid_spec,
        out_shape=jax.ShapeDtypeStruct((t, D_MODEL), F32),
        compiler_params=_params("parallel", "arbitrary"),
        name="moe",
    )(counts, x1, xn, comb, lw["wg"], lw["wu"], lw["wd"], final_g)


def _dft_angles(n):
    return 2.0 * np.pi * ((np.arange(n)[:, None] * np.arange(n)[None, :]) % n) / n


def _dft_tables(seq):
    n1, n2, n = seq // FFT_N2, FFT_N2, FOURIER_GROUP_DIM
    a1, a2, ac = _dft_angles(n1), _dft_angles(n2), _dft_angles(n)
    f1 = np.concatenate([np.cos(a1), -np.sin(a1)], axis=0)
    m2 = np.block([[np.cos(a2), np.sin(a2)], [-np.sin(a2), np.cos(a2)]])
    wcs = np.concatenate([np.cos(ac), np.sin(ac)], axis=0)
    idx = (jnp.arange(n2, dtype=jnp.int32)[:, None] * jnp.arange(n1, dtype=jnp.int32)[None, :]) % seq
    ang = jnp.broadcast_to((idx.astype(F32) * (2.0 * math.pi / seq))[:, :, None], (n2, n1, LANES))
    return {"f1": jnp.asarray(f1, BF16), "m2": jnp.asarray(m2, BF16), "wcs": jnp.asarray(wcs, BF16),
            "twc": jnp.cos(ang), "tws": jnp.sin(ang)}


def _rope_key_placement():
    ekr = np.zeros((LANES, HEAD_PAD), np.float32)
    for h in range(N_HEADS):
        for r in range(QK_ROPE_DIM):
            ekr[r, h * LANES + QK_NOPE_DIM + r] = 1.0
    return jnp.asarray(ekr, BF16)


def _rope_tables(seq):
    inv = 1.0 / (ROPE_BASE ** (jnp.arange(0, QK_ROPE_DIM, 2, dtype=F32) / QK_ROPE_DIM))
    rang = jnp.arange(seq, dtype=F32)[:, None] * inv[None, :]
    c, s = jnp.cos(rang), jnp.sin(rang)
    z = lambda w: jnp.zeros((seq, w), F32)
    tail = LANES - QK_DIM
    cos = jnp.concatenate([jnp.ones((seq, QK_NOPE_DIM), F32), c, c, z(tail)], axis=1)
    sin1 = jnp.concatenate([z(QK_NOPE_DIM), -s, z(ROPE_HALF), z(tail)], axis=1)
    sin2 = jnp.concatenate([z(QK_NOPE_DIM), z(ROPE_HALF), s, z(tail)], axis=1)
    return cos, sin1, sin2


def _pad_heads(w, per_head, lo, hi, width=LANES):
    r = w.shape[0]
    w3 = w.reshape(r, N_HEADS, per_head)[:, :, lo:hi]
    return jnp.pad(w3, ((0, 0), (0, 0), (0, width - (hi - lo)))).reshape(r, N_HEADS * width)


def _layer_weights(l, attn_norm_g, w_in, q_norm_g, kv_norm_g, w_uq, w_ukv, w_fourier_out, w_attn_out, w_out,
                   ffn_norm_g, w_grp, b_grp, w_exp, b_exp, w_gate, w_up, w_down):
    s0, s1, s2, s3 = (FOURIER_DIM, FOURIER_DIM + Q_LORA_RANK, FOURIER_DIM + Q_LORA_RANK + KV_LORA_RANK,
                      FOURIER_DIM + Q_LORA_RANK + KV_LORA_RANK + QK_ROPE_DIM)
    wi = w_in[l]
    wr = jnp.concatenate([w_grp[l], w_exp[l]], axis=1)
    wr = jnp.pad(wr, ((0, 0), (0, ROUTER_LANES - wr.shape[1])))
    wr_hi = wr.astype(BF16)
    br = jnp.concatenate([b_grp[l], b_exp[l]])
    return {
        "attn_g": attn_norm_g[l][None, :],
        "wf": wi[:, :s0].astype(BF16),
        "wql": wi[:, s0:s1].astype(BF16),
        "wkvl": wi[:, s1:s2].astype(BF16),
        "wkr": jnp.pad(wi[:, s2:s3], ((0, 0), (0, LANES - QK_ROPE_DIM))).astype(BF16),
        "wgate": wi[:, s3:].astype(BF16),
        "qg": q_norm_g[l][None, :],
        "kvg": kv_norm_g[l][None, :],
        "wuqt": _pad_heads(w_uq[l], QK_DIM, 0, QK_DIM).T.astype(BF16),
        "wuk": _pad_heads(w_ukv[l], QK_NOPE_DIM + V_HEAD_DIM, 0, QK_NOPE_DIM).astype(BF16),
        "wuvt": _pad_heads(w_ukv[l], QK_NOPE_DIM + V_HEAD_DIM, QK_NOPE_DIM, QK_NOPE_DIM + V_HEAD_DIM,
                           V_ROWS).T.astype(BF16),
        "ekr": _rope_key_placement(),
        "wfo": w_fourier_out[l].astype(BF16),
        "wao": w_attn_out[l].astype(BF16),
        "wout": w_out[l].astype(BF16),
        "ffn_g": ffn_norm_g[l][None, :],
        "wr_hi": wr_hi,
        "wr_lo": (wr - wr_hi.astype(F32)).astype(BF16),
        "br": jnp.pad(br, (0, ROUTER_LANES - br.shape[0]))[None, :],
        "wg": w_gate[l].astype(BF16),
        "wu": w_up[l].astype(BF16),
        "wd": w_down[l].astype(BF16),
    }


def _trunk(x, layers, final_g, tables):
    bsz, seq, d = x.shape
    t = bsz * seq
    dft, rope = tables
    x2d = x.reshape(t, d)
    for l, lw in enumerate(layers):
        u, k, gate, qt, vt = _in_proj(x2d, seq, lw, rope)
        shp = lambda z: z.reshape(bsz, seq, z.shape[-1])
        f = _fourier_mix(u, bsz, seq, dft)
        o = _attention(qt, shp(k), vt).reshape(t, N_HEADS * V_HEAD_DIM)
        x1, xn, comb, cnt = _merge(x2d, f, o, gate, lw)
        x2d = _moe(x1, xn, comb, cnt, lw, final_g, final_norm=(l == len(layers) - 1))
    return x2d.reshape(bsz, seq, d)


def kernel(x_prompt, x_sample, attn_norm_g, w_in, q_norm_g, kv_norm_g, w_uq, w_ukv, w_fourier_out, w_attn_out, w_out, ffn_norm_g, w_grp, b_grp, w_exp, b_exp, w_gate, w_up, w_down, final_norm_g):
    depth = w_in.shape[0]
    layers = [_layer_weights(l, attn_norm_g, w_in, q_norm_g, kv_norm_g, w_uq, w_ukv, w_fourier_out, w_attn_out,
                             w_out, ffn_norm_g, w_grp, b_grp, w_exp, b_exp, w_gate, w_up, w_down)
              for l in range(depth)]
    final_g = final_norm_g[None, :]
    outs = []
    for x in (x_prompt, x_sample):
        seq = x.shape[1]
        outs.append(_trunk(x, layers, final_g, (_dft_tables(seq), _rope_tables(seq))))
    return tuple(outs)
```

```python
import functools
import math

import numpy as np
import jax
import jax.numpy as jnp
from jax import lax
from jax.experimental import pallas as pl
from jax.experimental.pallas import tpu as pltpu

D_MODEL = 1024
FOURIER_GROUPS = 4
FOURIER_GROUP_DIM = 128
FOURIER_DIM = FOURIER_GROUPS * FOURIER_GROUP_DIM
N_HEADS = 8
QK_NOPE_DIM = 64
QK_ROPE_DIM = 32
V_HEAD_DIM = 64
Q_LORA_RANK = 384
KV_LORA_RANK = 256
QK_DIM = QK_NOPE_DIM + QK_ROPE_DIM
ROPE_BASE = 10000.0
N_GROUPS = 4
EXPERTS_PER_GROUP = 8
N_EXPERTS = N_GROUPS * EXPERTS_PER_GROUP
D_EXPERT = 256
EPS = 1e-6

LANES = 128
HEAD_PAD = N_HEADS * LANES
ROPE_HALF = QK_ROPE_DIM // 2
ONES_ROW = V_HEAD_DIM
V_ROWS = 80
FFT_N2 = 128
DFT_A_POSITIONS = 16
STAGE_PAD = 8
ROUTER_LANES = LANES
EXPERT_LANE0 = N_GROUPS
GROUP_LANE = 0
SUBLANES = 8
ATTN_Q_TILE = 8192
ATTN_STRIP = 256
ATTN_KEY_CHUNK = 512
ATTN_SCORE_BYTES = 16 * 1024 * 1024
HEADS_PER_STEP = LANES // V_HEAD_DIM
IN_PROJ_TILE = 1024
MERGE_TILE = 1024
MOE_TILE = 1024
MOE_CHUNK = 256
MOE_TAILS = (256, 320, 512)
VMEM_LIMIT = 56 * 1024 * 1024

BF16 = jnp.bfloat16
F32 = jnp.float32


def _dot(a, b):
    return jnp.dot(a, b, preferred_element_type=F32)


def _rms(x, g):
    return x * lax.rsqrt(jnp.mean(x * x, axis=-1, keepdims=True) + EPS) * g


def _params(*sem):
    return pltpu.CompilerParams(dimension_semantics=sem, vmem_limit_bytes=VMEM_LIMIT)


def _full(shape):
    return pl.BlockSpec(shape, lambda *_: (0,) * len(shape))


def _in_proj_kernel(x_ref, g_ref, wf_ref, wql_ref, wkvl_ref, wkr_ref, wgate_ref, qg_ref, kvg_ref,
                    wuqt_ref, wuk_ref, wuvt_ref, ekr_ref, cos_ref, sin1_ref, sin2_ref, cost_ref, sin1t_ref, sin2t_ref,
                    u_ref, k_ref, gate_ref, qt_ref, vt_ref, *, q_scale):
    xb = _rms(x_ref[...], g_ref[...]).astype(BF16)
    u_ref[...] = _dot(xb, wf_ref[...]).astype(BF16)
    qn = _rms(_dot(xb, wql_ref[...]), qg_ref[...]).astype(BF16)
    kvn = _rms(_dot(xb, wkvl_ref[...]), kvg_ref[...]).astype(BF16)
    ukr = _dot(xb, wkr_ref[...]).astype(BF16)
    k = _dot(kvn, wuk_ref[...]) + _dot(ukr, ekr_ref[...])
    nt = (((1,), (1,)), ((), ()))
    qt = lax.dot_general(wuqt_ref[...], qn, nt, preferred_element_type=F32)
    vt = lax.dot_general(wuvt_ref[...], kvn, nt, preferred_element_type=F32)
    head_row = lax.broadcasted_iota(jnp.int32, vt.shape, 0) % V_ROWS
    vt = jnp.where(head_row == ONES_ROW, 1.0, vt).astype(BF16)
    tk = vt_ref.shape[-1]
    for c in range(vt_ref.shape[1]):
        vt_ref[0, c] = vt[:, c * tk:(c + 1) * tk]
    cos, sin1, sin2 = cos_ref[...], sin1_ref[...], sin2_ref[...]
    cost, sin1t, sin2t = cost_ref[...], sin1t_ref[...], sin2t_ref[...]
    k_up = pltpu.roll(k, HEAD_PAD - ROPE_HALF, 1)
    k_dn = pltpu.roll(k, ROPE_HALF, 1)
    qt_up = jnp.concatenate([qt[ROPE_HALF:], qt[:ROPE_HALF]], axis=0)
    qt_dn = jnp.concatenate([qt[-ROPE_HALF:], qt[:-ROPE_HALF]], axis=0)
    for h in range(N_HEADS):
        sl = slice(h * LANES, (h + 1) * LANES)
        k_ref[:, sl] = (k[:, sl] * cos + k_up[:, sl] * sin1 + k_dn[:, sl] * sin2).astype(BF16)
        qt_ref[0, sl, :] = ((qt[sl] * cost + qt_up[sl] * sin1t + qt_dn[sl] * sin2t) * q_scale).astype(BF16)
    gate_ref[...] = jax.nn.sigmoid(_dot(xb, wgate_ref[...])).astype(BF16)


def _in_proj(x2d, seq, lw, rope):
    t = x2d.shape[0]
    tk = _key_chunk(seq)
    tm = max(tk, min(IN_PROJ_TILE, seq))
    n_seq_tiles = seq // tm
    row = lambda w: pl.BlockSpec((tm, w), lambda i: (i, 0))
    pos = pl.BlockSpec((tm, LANES), lambda i: (i % n_seq_tiles, 0))
    pos_t = pl.BlockSpec((LANES, tm), lambda i: (0, i % n_seq_tiles))
    weights = [lw["attn_g"], lw["wf"], lw["wql"], lw["wkvl"], lw["wkr"], lw["wgate"], lw["qg"], lw["kvg"],
               lw["wuqt"], lw["wuk"], lw["wuvt"], lw["ekr"]]
    out_w = [FOURIER_DIM, HEAD_PAD, 2 * D_MODEL]
    qt_spec = pl.BlockSpec((1, HEAD_PAD, tm), lambda i: (i // n_seq_tiles, 0, i % n_seq_tiles))
    qt_shape = jax.ShapeDtypeStruct((t // seq, HEAD_PAD, seq), BF16)
    vt_spec = pl.BlockSpec((1, tm // tk, N_HEADS * V_ROWS, tk), lambda i: (i // n_seq_tiles, i % n_seq_tiles, 0, 0))
    vt_shape = jax.ShapeDtypeStruct((t // seq, seq // tk, N_HEADS * V_ROWS, tk), BF16)
    return pl.pallas_call(
        functools.partial(_in_proj_kernel, q_scale=(QK_DIM ** -0.5) * math.log2(math.e)),
        grid=(t // tm,),
        in_specs=[row(D_MODEL)] + [_full(w.shape) for w in weights] + [pos] * 3 + [pos_t] * 3,
        out_specs=[row(w) for w in out_w] + [qt_spec, vt_spec],
        out_shape=[jax.ShapeDtypeStruct((t, w), BF16) for w in out_w] + [qt_shape, vt_shape],
        compiler_params=_params("parallel"),
        name="in_proj",
    )(x2d, *weights, *rope, *[r.T for r in rope])


def _dft_a_kernel(u_ref, f1_ref, twc_ref, tws_ref, y_ref, u_scr, yr_scr, yi_scr, *, tj, n1):
    f1 = f1_ref[...]
    groups = [slice(g * LANES, (g + 1) * LANES) for g in range(FOURIER_GROUPS)]

    def put(scr, rows, val):
        for g, lanes in enumerate(groups):
            scr[g, rows, :] = val[:, lanes]

    def strided(scr, start, size, stride):
        return jnp.concatenate([scr[g, pl.ds(start, size, stride=stride), :] for g in range(FOURIER_GROUPS)], axis=1)

    up, yp = tj + STAGE_PAD, n1 + STAGE_PAD
    for i in range(n1):
        put(u_scr, slice(i * up, i * up + tj), u_ref[0, i].astype(F32))
    for j in range(tj):
        u_j = strided(u_scr, j, n1, up).astype(BF16)
        y = _dot(f1, u_j)
        yr, yi = y[:n1], y[n1:]
        c = jnp.concatenate([twc_ref[j]] * FOURIER_GROUPS, axis=1)
        s = jnp.concatenate([tws_ref[j]] * FOURIER_GROUPS, axis=1)
        put(yr_scr, slice(j * yp, j * yp + n1), yr * c + yi * s)
        put(yi_scr, slice(j * yp, j * yp + n1), yi * c - yr * s)
    for k1 in range(n1):
        y_ref[0, k1, 0] = strided(yr_scr, k1, tj, yp).astype(BF16)
        y_ref[0, k1, 1] = strided(yi_scr, k1, tj, yp).astype(BF16)


def _dft_a(u4, f1, twc, tws):
    bsz, n1, n2, _ = u4.shape
    tj = DFT_A_POSITIONS
    return pl.pallas_call(
        functools.partial(_dft_a_kernel, tj=tj, n1=n1),
        grid=(bsz, n2 // tj),
        in_specs=[pl.BlockSpec((1, n1, tj, FOURIER_DIM), lambda b, j: (b, 0, j, 0)), _full(f1.shape),
                  pl.BlockSpec((tj, n1, LANES), lambda b, j: (j, 0, 0)),
                  pl.BlockSpec((tj, n1, LANES), lambda b, j: (j, 0, 0))],
        out_specs=pl.BlockSpec((1, n1, 2, tj, FOURIER_DIM), lambda b, j: (b, 0, 0, j, 0)),
        out_shape=jax.ShapeDtypeStruct((bsz, n1, 2, n2, FOURIER_DIM), BF16),
        scratch_shapes=[pltpu.VMEM((FOURIER_GROUPS, n1 * (tj + STAGE_PAD), LANES), F32)]
        + [pltpu.VMEM((FOURIER_GROUPS, tj * (n1 + STAGE_PAD), LANES), F32)] * 2,
        compiler_params=_params("parallel", "parallel"),
        name="dft_a",
    )(u4, f1, twc, tws)


def _dft_b_kernel(y_ref, m2_ref, wcs_ref, f_ref, x_scr, *, tk1, scale):
    m2 = m2_ref[...]
    for j in range(tk1):
        x = _dot(m2, y_ref[0, j].reshape(2 * FFT_N2, FOURIER_DIM))
        rows = slice(j * FFT_N2, (j + 1) * FFT_N2)
        for g in range(FOURIER_GROUPS):
            lanes = slice(g * FOURIER_GROUP_DIM, (g + 1) * FOURIER_GROUP_DIM)
            x_scr[rows, 2 * g * FOURIER_GROUP_DIM:(2 * g + 1) * FOURIER_GROUP_DIM] = x[:FFT_N2, lanes].astype(BF16)
            x_scr[rows, (2 * g + 1) * FOURIER_GROUP_DIM:(2 * g + 2) * FOURIER_GROUP_DIM] = x[FFT_N2:, lanes].astype(BF16)
    wcs = wcs_ref[...]
    f = jnp.concatenate([_dot(x_scr[:, 2 * g * FOURIER_GROUP_DIM:(2 * g + 2) * FOURIER_GROUP_DIM], wcs)
                         for g in range(FOURIER_GROUPS)], axis=1) * scale
    f_ref[0] = f.reshape(tk1, FFT_N2, FOURIER_DIM).astype(BF16)


def _dft_b(y5, m2, wcs, seq):
    bsz, n1 = y5.shape[:2]
    tk1 = 8
    return pl.pallas_call(
        functools.partial(_dft_b_kernel, tk1=tk1, scale=(seq * FOURIER_GROUP_DIM) ** -0.5),
        grid=(bsz, n1 // tk1),
        in_specs=[pl.BlockSpec((1, tk1, 2, FFT_N2, FOURIER_DIM), lambda b, j: (b, j, 0, 0, 0)),
                  _full(m2.shape), _full(wcs.shape)],
        out_specs=pl.BlockSpec((1, tk1, FFT_N2, FOURIER_DIM), lambda b, j: (b, j, 0, 0)),
        out_shape=jax.ShapeDtypeStruct((bsz, n1, FFT_N2, FOURIER_DIM), BF16),
        scratch_shapes=[pltpu.VMEM((tk1 * FFT_N2, 2 * FOURIER_DIM), BF16)],
        compiler_params=_params("parallel", "parallel"),
        name="dft_b",
    )(y5, m2, wcs)


def _fourier_mix(u2d, bsz, seq, dft):
    n1 = seq // FFT_N2
    u4 = u2d.reshape(bsz, n1, FFT_N2, FOURIER_DIM)
    y5 = _dft_a(u4, dft["f1"], dft["twc"], dft["tws"])
    f4 = _dft_b(y5, dft["m2"], dft["wcs"], seq)
    return f4.transpose(0, 2, 1, 3).reshape(bsz * seq, FOURIER_DIM)


def _key_chunk(seq):
    fit = ATTN_SCORE_BYTES // (2 * 4 * min(ATTN_Q_TILE, seq))
    return min(ATTN_KEY_CHUNK, fit, seq // 2)


def _attn_kernel(q_ref, k_ref, vt_ref, o_ref, m_ref, acc_ref, s_ref, smax_ref, ot_ref, *, tk, nk):
    tq = q_ref.shape[2]
    strips = [slice(j * ATTN_STRIP, (j + 1) * ATTN_STRIP) for j in range(tq // ATTN_STRIP)]
    for hh in range(HEADS_PER_STEP):
        head = slice(hh * LANES, (hh + 1) * LANES)
        m_ref[...] = jnp.full_like(m_ref, -jnp.inf)
        acc_ref[...] = jnp.zeros_like(acc_ref)

        def scores(kc, slot, sl):
            ks = pl.multiple_of(kc * tk, tk)
            st = _dot(k_ref[0, pl.ds(ks, tk), head], q_ref[0, head, sl])
            s_ref[slot, :, sl] = st
            smax_ref[slot, :, sl] = jnp.max(st, axis=0, keepdims=True)

        def softmax_pv(kc, slot, sl):
            m_prev = m_ref[:, sl]
            m_new = jnp.maximum(m_prev, smax_ref[slot, :, sl])
            alpha = jnp.exp2(m_prev - m_new)
            p = jnp.exp2(s_ref[slot, :, sl] - m_new).astype(BF16)
            vt = vt_ref[0, kc, hh * V_ROWS:(hh + 1) * V_ROWS, :]
            acc_ref[:, sl] = alpha * acc_ref[:, sl] + _dot(vt, p)
            m_ref[:, sl] = m_new

        for sl in strips:
            scores(0, 0, sl)

        def body(j, carry):
            c = 2 * j
            for sl in strips:
                scores(c + 1, 1, sl)
                softmax_pv(c, 0, sl)
            for sl in strips:
                scores(jnp.minimum(c + 2, nk - 1), 0, sl)
                softmax_pv(c + 1, 1, sl)
            return carry

        lax.fori_loop(0, nk // 2, body, 0)
        acc = acc_ref[...]
        ot_ref[hh * V_HEAD_DIM:(hh + 1) * V_HEAD_DIM, :] = acc[:V_HEAD_DIM] / acc[ONES_ROW:ONES_ROW + 1, :]
    o_ref[0] = ot_ref[...].T.astype(BF16)


def _attention(qt3, k3, vt4):
    bsz, seq, _ = k3.shape
    tq = min(ATTN_Q_TILE, seq)
    tk = _key_chunk(seq)
    nk = seq // tk
    hp = HEADS_PER_STEP
    qt = pl.BlockSpec((1, hp * LANES, tq), lambda b, h, i: (b, h, i))
    qo = pl.BlockSpec((1, tq, hp * V_HEAD_DIM), lambda b, h, i: (b, i, h))
    kk = pl.BlockSpec((1, seq, hp * LANES), lambda b, h, i: (b, 0, h))
    vt = pl.BlockSpec((1, nk, hp * V_ROWS, tk), lambda b, h, i: (b, 0, h, 0))
    return pl.pallas_call(
        functools.partial(_attn_kernel, tk=tk, nk=nk),
        grid=(bsz, N_HEADS // hp, seq // tq),
        in_specs=[qt, kk, vt],
        out_specs=qo,
        out_shape=jax.ShapeDtypeStruct((bsz, seq, N_HEADS * V_HEAD_DIM), BF16),
        scratch_shapes=[pltpu.VMEM((1, tq), F32), pltpu.VMEM((V_ROWS, tq), F32), pltpu.VMEM((2, tk, tq), F32),
                        pltpu.VMEM((2, 1, tq), F32), pltpu.VMEM((hp * V_HEAD_DIM, tq), F32)],
        compiler_params=_params("parallel", "parallel", "arbitrary"),
        name="attn",
    )(qt3, k3, vt4)


def _merge_kernel(x_ref, f_ref, o_ref, gate_ref, wfo_ref, wao_ref, wout_ref, fg_ref, wr_ref, br_ref,
                  x1_ref, xn_ref, comb_ref, cnt_ref):
    y_f = _dot(f_ref[...], wfo_ref[...])
    y_a = _dot(o_ref[...], wao_ref[...])
    gate = gate_ref[...].astype(F32)
    merged = gate[:, :D_MODEL] * y_f + gate[:, D_MODEL:] * y_a
    x1 = x_ref[...] + _dot(merged.astype(BF16), wout_ref[...])
    x1_ref[...] = x1
    xn = _rms(x1, fg_ref[...])
    xh = xn.astype(BF16)
    xn_ref[...] = xh
    xl = (xn - xh.astype(F32)).astype(BF16)
    wr = wr_ref[...]
    hh_hl = _dot(xh, wr)
    logit = hh_hl[:, :ROUTER_LANES] + hh_hl[:, ROUTER_LANES:] + _dot(xl, wr[:, :ROUTER_LANES]) + br_ref[...]
    lane = lax.broadcasted_iota(jnp.int32, logit.shape, 1)
    neg = jnp.float32(-jnp.inf)

    def top(mask):
        val = jnp.max(jnp.where(mask, logit, neg), axis=1, keepdims=True)
        idx = jnp.min(jnp.where(mask & (logit == val), lane, ROUTER_LANES), axis=1, keepdims=True)
        return val, idx

    is_grp = lane < N_GROUPS
    g_max, g_idx = top(is_grp)
    g_sum = jnp.sum(jnp.where(is_grp, jnp.exp(logit - g_max), 0.0), axis=1, keepdims=True)
    grp_p = 1.0 / g_sum
    lo = EXPERT_LANE0 + g_idx * EXPERTS_PER_GROUP
    in_grp = (lane >= lo) & (lane < lo + EXPERTS_PER_GROUP)
    l1, i1 = top(in_grp)
    l2, i2 = top(in_grp & (lane != i1))
    e2 = jnp.exp(l2 - l1)
    w1 = 1.0 / (1.0 + e2)
    w2 = e2 / (1.0 + e2)
    comb = jnp.where(lane == i1, grp_p * w1, jnp.where(lane == i2, grp_p * w2, 0.0))
    comb_ref[...] = jnp.where(lane == GROUP_LANE, g_idx.astype(F32), comb)
    cnt = jnp.sum(jnp.where(lane == g_idx, 1.0, 0.0), axis=0, keepdims=True)
    cnt_ref[0] = jnp.broadcast_to(cnt, cnt_ref.shape[1:])


def _merge(x2d, f2d, o2d, gate2d, lw):
    t = x2d.shape[0]
    tm = min(MERGE_TILE, t)
    row = lambda w: pl.BlockSpec((tm, w), lambda i: (i, 0))
    weights = [lw["wfo"], lw["wao"], lw["wout"], lw["ffn_g"], lw["wr"], lw["br"]]
    return pl.pallas_call(
        _merge_kernel,
        grid=(t // tm,),
        in_specs=[row(D_MODEL), row(FOURIER_DIM), row(N_HEADS * V_HEAD_DIM), row(2 * D_MODEL)]
        + [_full(w.shape) for w in weights],
        out_specs=[row(D_MODEL), row(D_MODEL), row(ROUTER_LANES),
                   pl.BlockSpec((1, SUBLANES, ROUTER_LANES), lambda i: (i, 0, 0))],
        out_shape=[jax.ShapeDtypeStruct((t, D_MODEL), F32), jax.ShapeDtypeStruct((t, D_MODEL), BF16),
                   jax.ShapeDtypeStruct((t, ROUTER_LANES), F32),
                   jax.ShapeDtypeStruct((t // tm, SUBLANES, ROUTER_LANES), F32)],
        compiler_params=_params("parallel"),
        name="merge",
    )(x2d, f2d, o2d, gate2d, *weights)


def _moe_kernel(cnt_ref, x1_ref, xn_ref, comb_ref, wg_ref, wu_ref, wd_ref, fin_ref, out_ref,
                acc_ref, dcol_ref, drow_ref, csplit_ref, *, final_norm):
    i, g = pl.program_id(0), pl.program_id(1)
    tm = xn_ref.shape[0]

    def schedule(gg):
        n = cnt_ref[i * N_GROUPS + gg]
        full, rem = n // MOE_CHUNK, n % MOE_CHUNK
        plain = jnp.where(rem > 0, jnp.maximum(full - 1, 0), full)
        need = jnp.where(rem > 0, jnp.where(full > 0, MOE_CHUNK + rem, rem), 0)
        tail = jnp.int32(0)
        for size in reversed(MOE_TAILS):
            tail = jnp.where(need <= size, size, tail)
        return plain, jnp.where(need > 0, tail, 0)

    def rows(gg):
        plain, tail = schedule(gg)
        return plain * MOE_CHUNK + tail

    @pl.when(g == 0)
    def _():
        comb = comb_ref[...]
        lane = lax.broadcasted_iota(jnp.int32, comb.shape, 1)
        grp = comb[:, GROUP_LANE:GROUP_LANE + 1].astype(jnp.int32)
        onehot = jnp.where((lane == grp) & (lane < N_GROUPS), 1.0, 0.0)
        earlier = lax.broadcasted_iota(jnp.int32, (tm, tm), 0) > lax.broadcasted_iota(jnp.int32, (tm, tm), 1)
        before = _dot(jnp.where(earlier, 1.0, 0.0).astype(BF16), onehot.astype(BF16))
        dest = jnp.sum(onehot * before, axis=1, keepdims=True)
        first_row = jnp.int32(0)
        for gg in range(N_GROUPS):
            dest = dest + jnp.where(grp == gg, first_row.astype(F32), 0.0)
            first_row = first_row + rows(gg)
        dcol = jnp.broadcast_to(dest, comb.shape)
        dcol_ref[...] = dcol
        drow_ref[...] = dcol.T[:SUBLANES, :]
        w = jnp.where(lane == GROUP_LANE, 0.0, comb)
        chi = w.astype(BF16)
        csplit_ref[:, :ROUTER_LANES] = chi
        csplit_ref[:, ROUTER_LANES:] = (w - chi.astype(F32)).astype(BF16)
        acc_ref[...] = jnp.zeros_like(acc_ref)

    first_row = jnp.int32(0)
    for gg in range(N_GROUPS):
        first_row = first_row + jnp.where(gg < g, rows(gg), 0)
    plain, tail = schedule(g)

    def chunk(start, size):
        row0 = start.astype(F32)
        slot = lax.broadcasted_iota(jnp.int32, (size, tm), 0).astype(F32) + row0
        gather = jnp.where(drow_ref[0:1, :] == slot, 1.0, 0.0).astype(BF16)
        xs = _dot(gather, xn_ref[...]).astype(BF16)
        cs2 = _dot(gather, csplit_ref[...])
        cs = cs2[:, :ROUTER_LANES] + cs2[:, ROUTER_LANES:]
        lane = lax.broadcasted_iota(jnp.int32, cs.shape, 1)
        y = None
        for e in range(EXPERTS_PER_GROUP):
            c_e = jnp.sum(jnp.where(lane == EXPERT_LANE0 + g * EXPERTS_PER_GROUP + e, cs, 0.0), axis=1, keepdims=True)
            hg = _dot(xs, wg_ref[e])
            hu = _dot(xs, wu_ref[e])
            hs = (hg * jax.nn.sigmoid(hg) * hu * c_e).astype(BF16)
            ye = _dot(hs, wd_ref[e])
            y = ye if y is None else y + ye
        slot_t = lax.broadcasted_iota(jnp.int32, (tm, size), 1).astype(F32) + row0
        scatter = jnp.where(dcol_ref[:, 0:1] == slot_t, 1.0, 0.0).astype(BF16)
        acc_ref[...] += _dot(scatter, y.astype(BF16))

    def plain_chunk(c, carry):
        chunk(first_row + c * MOE_CHUNK, MOE_CHUNK)
        return carry

    lax.fori_loop(0, plain, plain_chunk, 0)
    for size in MOE_TAILS:
        @pl.when(tail == size)
        def _(size=size):
            chunk(first_row + plain * MOE_CHUNK, size)

    @pl.when(g == pl.num_programs(1) - 1)
    def _():
        x2 = x1_ref[...] + acc_ref[...]
        out_ref[...] = _rms(x2, fin_ref[...]) if final_norm else x2


def _moe(x1, xn, comb, cnt, lw, final_g, final_norm):
    t = x1.shape[0]
    tm = min(MOE_TILE, t)
    per = tm // min(MERGE_TILE, t)
    counts = cnt[:, 0, :N_GROUPS].reshape(t // tm, per, N_GROUPS).sum(axis=1).astype(jnp.int32).reshape(-1)
    row = lambda w: pl.BlockSpec((tm, w), lambda i, g, c: (i, 0))
    wspec = lambda a, b: pl.BlockSpec((EXPERTS_PER_GROUP, a, b), lambda i, g, c: (g, 0, 0))
    grid_spec = pltpu.PrefetchScalarGridSpec(
        num_scalar_prefetch=1,
        grid=(t // tm, N_GROUPS),
        in_specs=[row(D_MODEL), row(D_MODEL), row(ROUTER_LANES), wspec(D_MODEL, D_EXPERT), wspec(D_MODEL, D_EXPERT),
                  wspec(D_EXPERT, D_MODEL), pl.BlockSpec((1, D_MODEL), lambda i, g, c: (0, 0))],
        out_specs=row(D_MODEL),
        scratch_shapes=[pltpu.VMEM((tm, D_MODEL), F32), pltpu.VMEM((tm, ROUTER_LANES), F32),
                        pltpu.VMEM((SUBLANES, tm), F32), pltpu.VMEM((tm, 2 * ROUTER_LANES), BF16)],
    )
    return pl.pallas_call(
        functools.partial(_moe_kernel, final_norm=final_norm),
        grid_spec=grid_spec,
        out_shape=jax.ShapeDtypeStruct((t, D_MODEL), F32),
        compiler_params=_params("parallel", "arbitrary"),
        name="moe",
    )(counts, x1, xn, comb, lw["wg"], lw["wu"], lw["wd"], final_g)


def _dft_angles(n):
    return 2.0 * np.pi * ((np.arange(n)[:, None] * np.arange(n)[None, :]) % n) / n


def _dft_tables(seq):
    n1, n2, n = seq // FFT_N2, FFT_N2, FOURIER_GROUP_DIM
    a1, a2, ac = _dft_angles(n1), _dft_angles(n2), _dft_angles(n)
    f1 = np.concatenate([np.cos(a1), -np.sin(a1)], axis=0)
    m2 = np.block([[np.cos(a2), np.sin(a2)], [-np.sin(a2), np.cos(a2)]])
    wcs = np.concatenate([np.cos(ac), np.sin(ac)], axis=0)
    idx = (jnp.arange(n2, dtype=jnp.int32)[:, None] * jnp.arange(n1, dtype=jnp.int32)[None, :]) % seq
    ang = jnp.broadcast_to((idx.astype(F32) * (2.0 * math.pi / seq))[:, :, None], (n2, n1, LANES))
    return {"f1": jnp.asarray(f1, BF16), "m2": jnp.asarray(m2, BF16), "wcs": jnp.asarray(wcs, BF16),
            "twc": jnp.cos(ang), "tws": jnp.sin(ang)}


def _rope_key_placement():
    ekr = np.zeros((LANES, HEAD_PAD), np.float32)
    for h in range(N_HEADS):
        for r in range(QK_ROPE_DIM):
            ekr[r, h * LANES + QK_NOPE_DIM + r] = 1.0
    return jnp.asarray(ekr, BF16)


def _rope_tables(seq):
    inv = 1.0 / (ROPE_BASE ** (jnp.arange(0, QK_ROPE_DIM, 2, dtype=F32) / QK_ROPE_DIM))
    rang = jnp.arange(seq, dtype=F32)[:, None] * inv[None, :]
    c, s = jnp.cos(rang), jnp.sin(rang)
    z = lambda w: jnp.zeros((seq, w), F32)
    tail = LANES - QK_DIM
    cos = jnp.concatenate([jnp.ones((seq, QK_NOPE_DIM), F32), c, c, z(tail)], axis=1)
    sin1 = jnp.concatenate([z(QK_NOPE_DIM), -s, z(ROPE_HALF), z(tail)], axis=1)
    sin2 = jnp.concatenate([z(QK_NOPE_DIM), z(ROPE_HALF), s, z(tail)], axis=1)
    return cos, sin1, sin2


def _pad_heads(w, per_head, lo, hi, width=LANES):
    r = w.shape[0]
    w3 = w.reshape(r, N_HEADS, per_head)[:, :, lo:hi]
    return jnp.pad(w3, ((0, 0), (0, 0), (0, width - (hi - lo)))).reshape(r, N_HEADS * width)


def _layer_weights(l, attn_norm_g, w_in, q_norm_g, kv_norm_g, w_uq, w_ukv, w_fourier_out, w_attn_out, w_out,
                   ffn_norm_g, w_grp, b_grp, w_exp, b_exp, w_gate, w_up, w_down):
    s0, s1, s2, s3 = (FOURIER_DIM, FOURIER_DIM + Q_LORA_RANK, FOURIER_DIM + Q_LORA_RANK + KV_LORA_RANK,
                      FOURIER_DIM + Q_LORA_RANK + KV_LORA_RANK + QK_ROPE_DIM)
    wi = w_in[l]
    wr = jnp.concatenate([w_grp[l], w_exp[l]], axis=1)
    wr = jnp.pad(wr, ((0, 0), (0, ROUTER_LANES - wr.shape[1])))
    wr_hi = wr.astype(BF16)
    br = jnp.concatenate([b_grp[l], b_exp[l]])
    return {
        "attn_g": attn_norm_g[l][None, :],
        "wf": wi[:, :s0].astype(BF16),
        "wql": wi[:, s0:s1].astype(BF16),
        "wkvl": wi[:, s1:s2].astype(BF16),
        "wkr": jnp.pad(wi[:, s2:s3], ((0, 0), (0, LANES - QK_ROPE_DIM))).astype(BF16),
        "wgate": wi[:, s3:].astype(BF16),
        "qg": q_norm_g[l][None, :],
        "kvg": kv_norm_g[l][None, :],
        "wuqt": _pad_heads(w_uq[l], QK_DIM, 0, QK_DIM).T.astype(BF16),
        "wuk": _pad_heads(w_ukv[l], QK_NOPE_DIM + V_HEAD_DIM, 0, QK_NOPE_DIM).astype(BF16),
        "wuvt": _pad_heads(w_ukv[l], QK_NOPE_DIM + V_HEAD_DIM, QK_NOPE_DIM, QK_NOPE_DIM + V_HEAD_DIM,
                           V_ROWS).T.astype(BF16),
        "ekr": _rope_key_placement(),
        "wfo": w_fourier_out[l].astype(BF16),
        "wao": w_attn_out[l].astype(BF16),
        "wout": w_out[l].astype(BF16),
        "ffn_g": ffn_norm_g[l][None, :],
        "wr": jnp.concatenate([wr_hi, (wr - wr_hi.astype(F32)).astype(BF16)], axis=1),
        "br": jnp.pad(br, (0, ROUTER_LANES - br.shape[0]))[None, :],
        "wg": w_gate[l].astype(BF16),
        "wu": w_up[l].astype(BF16),
        "wd": w_down[l].astype(BF16),
    }


def _trunk(x, layers, final_g, tables):
    bsz, seq, d = x.shape
    t = bsz * seq
    dft, rope = tables
    x2d = x.reshape(t, d)
    for l, lw in enumerate(layers):
        u, k, gate, qt, vt = _in_proj(x2d, seq, lw, rope)
        shp = lambda z: z.reshape(bsz, seq, z.shape[-1])
        f = _fourier_mix(u, bsz, seq, dft)
        o = _attention(qt, shp(k), vt).reshape(t, N_HEADS * V_HEAD_DIM)
        x1, xn, comb, cnt = _merge(x2d, f, o, gate, lw)
        x2d = _moe(x1, xn, comb, cnt, lw, final_g, final_norm=(l == len(layers) - 1))
    return x2d.reshape(bsz, seq, d)


def kernel(x_prompt, x_sample, attn_norm_g, w_in, q_norm_g, kv_norm_g, w_uq, w_ukv, w_fourier_out, w_attn_out, w_out, ffn_norm_g, w_grp, b_grp, w_exp, b_exp, w_gate, w_up, w_down, final_norm_g):
    depth = w_in.shape[0]
    layers = [_layer_weights(l, attn_norm_g, w_in, q_norm_g, kv_norm_g, w_uq, w_ukv, w_fourier_out, w_attn_out,
                             w_out, ffn_norm_g, w_grp, b_grp, w_exp, b_exp, w_gate, w_up, w_down)
              for l in range(depth)]
    final_g = final_norm_g[None, :]
    outs = []
    for x in (x_prompt, x_sample):
        seq = x.shape[1]
        outs.append(_trunk(x, layers, final_g, (_dft_tables(seq), _rope_tables(seq))))
    return tuple(outs)
```
